```python
import jax, jax.numpy as jnp
from jax import lax
import numpy as np

D_MODEL = 1024
BATCH = 2
SEQ = 8192
DEPTH = 1
DEC_BATCH = 32
DEC_SEQ = 1
PAST_LEN = 16384
PAGE_SIZE = 128

FOX_HEADS = 8
FOX_HEAD_DIM = 64
FOX_WIDTH = FOX_HEADS * FOX_HEAD_DIM
FOX_GATE_BIAS = 4.0
Q_BLOCK = 128
GLA_HEADS = 4
GLA_DK = 64
GLA_DV = 128
GLA_KW = GLA_HEADS * GLA_DK
GLA_VW = GLA_HEADS * GLA_DV
GLA_GATE_RANK = 16
GLA_GATE_NORM = 16.0
GLA_CHUNK = 32
MIX_WIDTH = FOX_WIDTH + GLA_VW
D_FF = 4 * D_MODEL
EPS = 1e-6
SPLIT_SIZES = (FOX_WIDTH, FOX_WIDTH, FOX_WIDTH, FOX_HEADS, GLA_KW, GLA_KW, GLA_VW, GLA_GATE_RANK, GLA_VW)
IN_COLS = FOX_WIDTH * 3 + FOX_HEADS + GLA_KW * 2 + GLA_VW * 2 + GLA_GATE_RANK

kernel_name = 'fox_gla_parallel_heads_decode_step'


def rmsnorm(x, g):
    xf = x.astype(jnp.float32)
    y = xf * lax.rsqrt(jnp.mean(xf * xf, axis=-1, keepdims=True) + EPS)
    return (y * g.astype(jnp.float32)).astype(x.dtype)


def project(xn, w_in, fox_b_f, gla_w_gate_up, gla_b_gate):
    B, L, _ = xn.shape
    z = xn @ w_in
    offs = np.cumsum(SPLIT_SIZES)[:-1].tolist()
    fq, fk, fv, ff, gq, gk, gv, glr, gg = jnp.split(z, offs, axis=-1)
    fox_shape = (B, L, FOX_HEADS, FOX_HEAD_DIM)
    fox_logf = jax.nn.log_sigmoid(ff.astype(jnp.float32) + fox_b_f.astype(jnp.float32))
    gla_log_a = jax.nn.log_sigmoid((glr @ gla_w_gate_up).astype(jnp.float32)
                                   + gla_b_gate.astype(jnp.float32)) / GLA_GATE_NORM
    gla_log_a = gla_log_a.reshape(B, L, GLA_HEADS, GLA_DK)
    return (fq.reshape(fox_shape), fk.reshape(fox_shape), fv.reshape(fox_shape), fox_logf,
            gq.reshape(B, L, GLA_HEADS, GLA_DK), gk.reshape(B, L, GLA_HEADS, GLA_DK),
            gv.reshape(B, L, GLA_HEADS, GLA_DV), gla_log_a, gg)


def fox_prompt(q, k, v, logf):
    B, S, H, dh = q.shape
    scale = dh ** -0.5
    c = jnp.cumsum(logf, axis=1).swapaxes(1, 2)
    nb = S // Q_BLOCK
    qb = q.reshape(B, nb, Q_BLOCK, H, dh).swapaxes(0, 1)
    cb = c.reshape(B, H, nb, Q_BLOCK).transpose(2, 0, 1, 3)
    kpos = jnp.arange(S)

    def one_block(args):
        i, q_i, c_i = args
        s = jnp.einsum('bqhd,bkhd->bhqk', q_i, k).astype(jnp.float32) * scale
        s = s + (c_i[..., :, None] - c[:, :, None, :])
        qpos = i * Q_BLOCK + jnp.arange(Q_BLOCK)
        s = jnp.where(kpos[None, :] <= qpos[:, None], s, -jnp.inf)
        p = jax.nn.softmax(s, axis=-1)
        return jnp.einsum('bhqk,bkhd->bqhd', p.astype(v.dtype), v)

    o = lax.map(one_block, (jnp.arange(nb), qb, cb))
    return o.swapaxes(0, 1).reshape(B, S, H, dh)


def fox_sample(q, k_new, v_new, logf_new, cache_k, cache_v, cache_logf, page_table):
    Bd, L, H, dh = q.shape
    n_pages = page_table.shape[1]
    past = n_pages * PAGE_SIZE
    scale = dh ** -0.5
    k_past = cache_k[page_table].reshape(Bd, past, H, dh).astype(q.dtype)
    v_past = cache_v[page_table].reshape(Bd, past, H, dh).astype(q.dtype)
    lf_past = cache_logf[page_table].reshape(Bd, past, H).astype(jnp.float32)
    k_all = jnp.concatenate([k_past, k_new], axis=1)
    v_all = jnp.concatenate([v_past, v_new], axis=1)
    c = jnp.cumsum(jnp.concatenate([lf_past, logf_new], axis=1), axis=1).swapaxes(1, 2)
    s = jnp.einsum('bqhd,bkhd->bhqk', q, k_all).astype(jnp.float32) * scale
    s = s + (c[:, :, past:, None] - c[:, :, None, :])
    qpos = past + jnp.arange(L)
    kpos = jnp.arange(past + L)
    s = jnp.where(kpos[None, :] <= qpos[:, None], s, -jnp.inf)
    p = jax.nn.softmax(s, axis=-1)
    return jnp.einsum('bhqk,bkhd->bqhd', p.astype(v_all.dtype), v_all)


def gla_chunked(q, k, v, log_a, s0):
    B, L, H, dk = q.shape
    dv = v.shape[-1]
    c = min(GLA_CHUNK, L)
    pad = (-L) % c
    q = q.astype(jnp.float32) * (dk ** -0.5)
    k = k.astype(jnp.float32)
    v = v.astype(jnp.float32)
    if pad:
        pw = ((0, 0), (0, pad), (0, 0), (0, 0))
        q, k, v, log_a = (jnp.pad(t, pw) for t in (q, k, v, log_a))
    n = (L + pad) // c

    def to_chunks(t):
        return t.reshape(B, n, c, *t.shape[2:]).swapaxes(0, 1)

    tril = jnp.tril(jnp.ones((c, c), dtype=bool))

    def step(S, inp):
        qc, kc, vc, ac = inp
        b = jnp.cumsum(ac, axis=1)
        b_ref = b[:, c // 2:c // 2 + 1]
        b_last = b[:, -1:]
        A = jnp.einsum('bihd,bjhd->bhij', qc * jnp.exp(b - b_ref), kc * jnp.exp(b_ref - b))
        A = jnp.where(tril, A, 0.0)
        o = (jnp.einsum('bhij,bjhv->bihv', A, vc)
             + jnp.einsum('bihd,bhdv->bihv', qc * jnp.exp(b), S))
        S = (S * jnp.exp(b_last[:, 0])[..., None]
             + jnp.einsum('bjhd,bjhv->bhdv', kc * jnp.exp(b_last - b), vc))
        return S, o

    S_fin, o = lax.scan(step, s0.astype(jnp.float32),
                        (to_chunks(q), to_chunks(k), to_chunks(v), to_chunks(log_a)))
    o = o.swapaxes(0, 1).reshape(B, n * c, H, dv)[:, :L]
    return o, S_fin


def merge_and_ffn(h, fox_o, gla_o, gla_g, gla_norm_g, w_o, norm2_g, w_up, w_down):
    B, L, _ = h.shape
    go = rmsnorm(gla_o, gla_norm_g).astype(h.dtype) * jax.nn.silu(gla_g).reshape(B, L, GLA_HEADS, GLA_DV)
    mixed = jnp.concatenate([fox_o.reshape(B, L, FOX_WIDTH).astype(h.dtype),
                             go.reshape(B, L, GLA_VW)], axis=-1)
    h = h + mixed @ w_o
    u = jnp.square(jax.nn.relu(rmsnorm(h, norm2_g) @ w_up))
    return h + u @ w_down


def setup_inputs(seed: int = 0) -> dict:
    key = jax.random.key(seed)
    ks = jax.random.split(key, 20)
    nrm = jax.random.normal
    n_pages = PAST_LEN // PAGE_SIZE
    n_phys = (DEC_BATCH * n_pages * 5) // 4
    page_table = jax.random.permutation(ks[6], n_phys)[:DEC_BATCH * n_pages]
    page_table = page_table.reshape(DEC_BATCH, n_pages).astype(jnp.int32)
    return {
        'x_prompt': nrm(ks[0], (BATCH, SEQ, D_MODEL), jnp.float32),
        'x_sample': nrm(ks[1], (DEC_BATCH, DEC_SEQ, D_MODEL), jnp.float32),
        'cache_k': nrm(ks[2], (DEPTH, n_phys, PAGE_SIZE, FOX_HEADS, FOX_HEAD_DIM), jnp.float32),
        'cache_v': nrm(ks[3], (DEPTH, n_phys, PAGE_SIZE, FOX_HEADS, FOX_HEAD_DIM), jnp.float32),
        'cache_logf': jax.nn.log_sigmoid(FOX_GATE_BIAS + nrm(ks[4], (DEPTH, n_phys, PAGE_SIZE, FOX_HEADS), jnp.float32)),
        'state_gla': 0.5 * nrm(ks[5], (DEPTH, DEC_BATCH, GLA_HEADS, GLA_DK, GLA_DV), jnp.float32),
        'page_table': page_table,
        'norm1_g': 1.0 + 0.02 * nrm(ks[7], (DEPTH, D_MODEL), jnp.float32),
        'w_in': nrm(ks[8], (DEPTH, D_MODEL, IN_COLS), jnp.float32) * D_MODEL ** -0.5,
        'fox_b_f': FOX_GATE_BIAS + 0.5 * nrm(ks[9], (DEPTH, FOX_HEADS), jnp.float32),
        'gla_w_gate_up': nrm(ks[10], (DEPTH, GLA_GATE_RANK, GLA_KW), jnp.float32) * GLA_GATE_RANK ** -0.5,
        'gla_b_gate': 0.1 * nrm(ks[11], (DEPTH, GLA_KW), jnp.float32),
        'gla_norm_g': 1.0 + 0.02 * nrm(ks[12], (DEPTH, GLA_DV), jnp.float32),
        'w_o': nrm(ks[13], (DEPTH, MIX_WIDTH, D_MODEL), jnp.float32) * MIX_WIDTH ** -0.5,
        'norm2_g': 1.0 + 0.02 * nrm(ks[14], (DEPTH, D_MODEL), jnp.float32),
        'w_up': nrm(ks[15], (DEPTH, D_MODEL, D_FF), jnp.float32) * D_MODEL ** -0.5,
        'w_down': nrm(ks[16], (DEPTH, D_FF, D_MODEL), jnp.float32) * D_FF ** -0.5,
        'final_g': 1.0 + 0.02 * nrm(ks[17], (D_MODEL,), jnp.float32),
    }


def reference(x_prompt, x_sample, cache_k, cache_v, cache_logf, state_gla, page_table,
              norm1_g, w_in, fox_b_f, gla_w_gate_up, gla_b_gate, gla_norm_g, w_o,
              norm2_g, w_up, w_down, final_g):
    hp, hs = x_prompt, x_sample
    kp_l, vp_l, fp_l, sp_l = [], [], [], []
    ks_l, vs_l, fs_l, ss_l = [], [], [], []
    for l in range(DEPTH):
        xn = rmsnorm(hp, norm1_g[l])
        fq, fk, fv, flf, gq, gk, gv, gla_a, gg = project(xn, w_in[l], fox_b_f[l], gla_w_gate_up[l], gla_b_gate[l])
        fox_o = fox_prompt(fq, fk, fv, flf)
        s_init = jnp.zeros((hp.shape[0], GLA_HEADS, GLA_DK, GLA_DV), jnp.float32)
        gla_o, s_p = gla_chunked(gq, gk, gv, gla_a, s_init)
        hp = merge_and_ffn(hp, fox_o, gla_o, gg, gla_norm_g[l], w_o[l], norm2_g[l], w_up[l], w_down[l])
        kp_l.append(fk); vp_l.append(fv); fp_l.append(flf); sp_l.append(s_p)
        xn = rmsnorm(hs, norm1_g[l])
        fq, fk, fv, flf, gq, gk, gv, gla_a, gg = project(xn, w_in[l], fox_b_f[l], gla_w_gate_up[l], gla_b_gate[l])
        fox_o = fox_sample(fq, fk, fv, flf, cache_k[l], cache_v[l], cache_logf[l], page_table)
        gla_o, s_s = gla_chunked(gq, gk, gv, gla_a, state_gla[l])
        hs = merge_and_ffn(hs, fox_o, gla_o, gg, gla_norm_g[l], w_o[l], norm2_g[l], w_up[l], w_down[l])
        ks_l.append(fk); vs_l.append(fv); fs_l.append(flf); ss_l.append(s_s)
    y_prompt = rmsnorm(hp, final_g)
    y_sample = rmsnorm(hs, final_g)
    return (y_prompt, y_sample,
            jnp.stack(kp_l), jnp.stack(vp_l), jnp.stack(fp_l), jnp.stack(sp_l),
            jnp.stack(ks_l), jnp.stack(vs_l), jnp.stack(fs_l), jnp.stack(ss_l))
```

```python
import functools

import jax
import jax.numpy as jnp
from jax import lax
from jax.experimental import pallas as pl
from jax.experimental.pallas import tpu as pltpu

F32 = jnp.float32
BF16 = jnp.bfloat16

LANES = 128
SUBLANES = 8
VMEM_LIMIT_BYTES = 56 * 1024 * 1024

EPS = 1e-6
GLA_GATE_NORM = 16.0
GLA_CHUNK = 128
GLA_SUB = 32
PROJ_TM = 512
ATT_T = 512
FFN_TM = 512
FFN_CHUNK = 1024
DEC_G = 8
DEC_NSLOT = 3


def _dot(a, b):
    return jnp.dot(a, b, preferred_element_type=F32)


def _dot_nt(a, b):
    return lax.dot_general(a, b, (((1,), (1,)), ((), ())), preferred_element_type=F32)


def _split3(x):
    hi = x.astype(BF16).astype(F32)
    r = x - hi
    mid = r.astype(BF16).astype(F32)
    lo = r - mid
    return hi, mid, lo


def _log_sigmoid(x):
    return jnp.minimum(x, 0.0) - jnp.log1p(jnp.exp(-jnp.abs(x)))


def _silu(x):
    return x / (1.0 + jnp.exp(-x))


def _rms(x, g):
    return x * lax.rsqrt(jnp.mean(x * x, axis=-1, keepdims=True) + EPS) * g


def _full_spec(shape):
    n = len(shape)
    return pl.BlockSpec(shape, lambda *_: (0,) * n)


def _proj_prompt_kernel(x_ref, g1_ref, wtok_ref, wdm_ref, bf_ref, wgate_ref, bgate_ref, gnorm_ref,
                        q_out, kT_out, vT_out, lfT_out, cT_out, go_out, sfin_out,
                        carry_s, bdt_s, uincl_s, lincl_s, gq_s, gk_s, gv_s, gg_s, la_s,
                        *, tm, fox_w, fox_dh, gla_kw, gla_vw, n_heads_gla):
    t = pl.program_id(1)
    nt = pl.num_programs(1)
    dk = gla_kw // n_heads_gla
    dv = gla_vw // n_heads_gla

    @pl.when(jnp.logical_and(pl.program_id(0) == 0, t == 0))
    def _():
        r = lax.broadcasted_iota(jnp.int32, (tm, tm), 0)
        c = lax.broadcasted_iota(jnp.int32, (tm, tm), 1)
        uincl_s[...] = jnp.where(r <= c, 1.0, 0.0).astype(BF16)
        r = lax.broadcasted_iota(jnp.int32, (GLA_CHUNK, GLA_CHUNK), 0)
        c = lax.broadcasted_iota(jnp.int32, (GLA_CHUNK, GLA_CHUNK), 1)
        lincl_s[...] = jnp.where(c <= r, 1.0, 0.0).astype(BF16)

    @pl.when(t == 0)
    def _():
        carry_s[...] = jnp.zeros_like(carry_s)
        bdt_s[...] = jnp.zeros_like(bdt_s)

    x = x_ref[0]
    xn = _rms(x, g1_ref[...]).astype(BF16)

    z = _dot_nt(xn, wtok_ref[...])
    o0 = 0
    q_out[0] = (z[:, o0:o0 + fox_w] * (fox_dh ** -0.5)).astype(BF16)
    o0 += fox_w
    gq_s[...] = z[:, o0:o0 + gla_kw] * (dk ** -0.5)
    o0 += gla_kw
    gk_s[...] = z[:, o0:o0 + gla_kw]
    o0 += gla_kw
    gv_s[...] = z[:, o0:o0 + gla_vw]
    o0 += gla_vw
    gg_s[...] = z[:, o0:o0 + gla_vw]
    o0 += gla_vw
    glr = z[:, o0:o0 + LANES].astype(BF16)
    pre = _dot(glr, wgate_ref[...]) + bgate_ref[...]
    la_s[...] = _log_sigmoid(pre) * (1.0 / GLA_GATE_NORM)

    zt = _dot_nt(wdm_ref[...], xn)
    kT_out[0] = zt[0:fox_w]
    vT_out[0] = zt[fox_w:2 * fox_w]
    lf = _log_sigmoid(zt[2 * fox_w:2 * fox_w + SUBLANES] + bf_ref[...])
    lfT_out[0] = lf
    hi, mid, lo = _split3(lf)
    stack = jnp.concatenate([hi, mid, lo, jnp.zeros_like(hi)], axis=0).astype(BF16)
    cs = _dot(stack, uincl_s[...])
    cs = cs[0:8] + cs[8:16] + cs[16:24]
    carry = carry_s[...]
    cT_out[0] = cs + carry[:, 0:1]
    tot = _dot(stack, jnp.ones((tm, LANES), BF16))
    carry_s[...] = carry + tot[0:8] + tot[8:16] + tot[16:24]

    nsub = GLA_CHUNK // GLA_SUB
    rowi = lax.broadcasted_iota(jnp.int32, (GLA_CHUNK, gla_kw), 0)
    lanei = lax.broadcasted_iota(jnp.int32, (GLA_CHUNK, gla_kw), 1)
    ar = lax.broadcasted_iota(jnp.int32, (GLA_CHUNK, GLA_CHUNK), 0)
    ac = lax.broadcasted_iota(jnp.int32, (GLA_CHUNK, GLA_CHUNK), 1)
    tri_blk = jnp.logical_and(ar // GLA_SUB == ac // GLA_SUB, ar >= ac)
    br = lax.broadcasted_iota(jnp.int32, (gla_vw, gla_kw), 0)
    bc = lax.broadcasted_iota(jnp.int32, (gla_vw, gla_kw), 1)
    bd_mask = (br // dv) == (bc // dk)

    def chunk_body(ci, _):
        r0 = pl.multiple_of(ci * GLA_CHUNK, GLA_CHUNK)
        la_c = la_s[pl.ds(r0, GLA_CHUNK), :]
        gq_c = gq_s[pl.ds(r0, GLA_CHUNK), :]
        gk_c = gk_s[pl.ds(r0, GLA_CHUNK), :]
        gv_c = gv_s[pl.ds(r0, GLA_CHUNK), :]
        gg_c = gg_s[pl.ds(r0, GLA_CHUNK), :]
        h3, m3, l3 = _split3(la_c)
        st = jnp.concatenate([h3, m3, l3], axis=1).astype(BF16)
        bb = _dot(lincl_s[...], st)
        b = bb[:, 0:gla_kw] + bb[:, gla_kw:2 * gla_kw] + bb[:, 2 * gla_kw:3 * gla_kw]
        bmid_l, bend_l, b0_l = [], [], []
        for i in range(nsub):
            s0 = i * GLA_SUB
            bmid_l.append(b[s0 + GLA_SUB // 2:s0 + GLA_SUB // 2 + 1])
            bend_l.append(b[s0 + GLA_SUB - 1:s0 + GLA_SUB])
            b0_l.append(jnp.zeros((1, gla_kw), F32) if i == 0 else b[s0 - 1:s0])
        bc_rows = lambda rows: jnp.concatenate(
            [jnp.broadcast_to(r, (GLA_SUB, gla_kw)) for r in rows], axis=0)
        bmid, bend, b0 = bc_rows(bmid_l), bc_rows(bend_l), bc_rows(b0_l)
        qt = (gq_c * jnp.exp(b - bmid)).astype(BF16)
        kt = (gk_c * jnp.exp(bmid - b)).astype(BF16)
        qp = gq_c * jnp.exp(b - b0)
        kd = gk_c * jnp.exp(bend - b)
        gv_bf = gv_c.astype(BF16)
        gvT_bf = gv_c.T.astype(BF16)

        o_inter = jnp.zeros((GLA_CHUNK, gla_vw), F32)
        for i in range(nsub):
            rm = jnp.logical_and(rowi >= i * GLA_SUB, rowi < (i + 1) * GLA_SUB)
            bdt = bdt_s[...]
            o_inter = o_inter + _dot_nt(jnp.where(rm, qp, 0.0).astype(BF16), bdt.astype(BF16))
            ut = _dot(gvT_bf, jnp.where(rm, kd, 0.0).astype(BF16))
            decay = jnp.exp(bend_l[i] - b0_l[i])
            bdt_s[...] = bdt * decay + jnp.where(bd_mask, ut, 0.0)

        for h in range(n_heads_gla):
            hm = jnp.logical_and(lanei >= h * dk, lanei < (h + 1) * dk)
            a = _dot_nt(jnp.where(hm, qt, jnp.zeros_like(qt)), kt)
            a = jnp.where(tri_blk, a, 0.0).astype(BF16)
            o_h = _dot(a, gv_bf[:, h * dv:(h + 1) * dv]) + o_inter[:, h * dv:(h + 1) * dv]
            o_n = _rms(o_h, gnorm_ref[...])
            go = o_n * _silu(gg_c[:, h * dv:(h + 1) * dv])
            go_out[0, pl.ds(r0, GLA_CHUNK), h * dv:(h + 1) * dv] = go.astype(BF16)
        return 0

    lax.fori_loop(0, tm // GLA_CHUNK, chunk_body, 0)

    @pl.when(t == nt - 1)
    def _():
        bd = bdt_s[...].T
        for h in range(n_heads_gla):
            sfin_out[0, h] = bd[h * dk:(h + 1) * dk, h * dv:(h + 1) * dv]


def _attn_prompt_kernel(q_ref, kT_ref, vT_ref, c_ref, o_ref, kbf_s, vbf_s, m_s, l_s, acc_s, *, t_blk, dh):
    i = pl.program_id(2)

    @pl.when(i == 0)
    def _():
        kbf_s[...] = kT_ref[0].astype(BF16)
        vbf_s[...] = vT_ref[0].astype(BF16)

    q2 = q_ref[0]
    lane = lax.broadcasted_iota(jnp.int32, (t_blk, 2 * dh), 1)
    row = lax.broadcasted_iota(jnp.int32, (t_blk, t_blk), 0)
    col = lax.broadcasted_iota(jnp.int32, (t_blk, t_blk), 1)
    t0 = pl.multiple_of(i * t_blk, t_blk)
    outs = []
    for h in range(2):
        qh = jnp.where(jnp.logical_and(lane >= h * dh, lane < (h + 1) * dh), q2, jnp.zeros_like(q2))
        ref_c = c_ref[0, 0, h:h + 1, pl.ds(t0, LANES)][:, 0:1]
        m_s[...] = jnp.full_like(m_s, -jnp.inf)
        l_s[...] = jnp.zeros_like(l_s)
        acc_s[...] = jnp.zeros_like(acc_s)

        def step(j, masked):
            k0 = pl.multiple_of(j * t_blk, t_blk)
            s = _dot(qh, kbf_s[:, pl.ds(k0, t_blk)])
            z = s - (c_ref[0, 0, h:h + 1, pl.ds(k0, t_blk)] - ref_c)
            if masked:
                z = jnp.where(col <= row, z, -jnp.inf)
            m_old = m_s[...]
            m_new = jnp.maximum(m_old, jnp.max(z, axis=1, keepdims=True))
            alpha = jnp.exp(m_old - m_new)
            p = jnp.exp(z - m_new)
            l_s[...] = alpha * l_s[...] + jnp.sum(p, axis=1, keepdims=True)
            pv = _dot_nt(p.astype(BF16), vbf_s[:, pl.ds(k0, t_blk)])
            acc_s[...] = alpha * acc_s[...] + pv
            m_s[...] = m_new

        def body(j, carry):
            step(j, False)
            return carry

        lax.fori_loop(0, i, body, 0)
        step(i, True)
        outs.append(acc_s[...] / l_s[...])
    o_ref[0] = jnp.where(lane < dh, outs[0], outs[1]).astype(BF16)


def _ffn_kernel(x_ref, fo_ref, go_ref, wo_ref, wup_ref, wdn_ref, g2_ref, gf_ref, y_ref, u_s, *, fox_w, d_ff):
    x = x_ref[...]
    h = x + (_dot(fo_ref[...], wo_ref[0:fox_w, :]) + _dot(go_ref[...], wo_ref[fox_w:, :]))
    hn = _rms(h, g2_ref[...]).astype(BF16)
    for c in range(d_ff // FFN_CHUNK):
        u = _dot(hn, wup_ref[:, c * FFN_CHUNK:(c + 1) * FFN_CHUNK])
        u_s[:, c * FFN_CHUNK:(c + 1) * FFN_CHUNK] = jnp.square(jnp.maximum(u, 0.0)).astype(BF16)
    y_ref[...] = _rms(h + _dot(u_s[...], wdn_ref[...]), gf_ref[...])


def _ffn_call(x2d, fo, go, wo, wup, wdn, g2, gf, tm):
    n, d = x2d.shape
    fox_w = fo.shape[1]
    d_ff = wup.shape[1]
    const = lambda shape: pl.BlockSpec(shape, lambda i: (0, 0), pipeline_mode=pl.Buffered(1))
    return pl.pallas_call(
        functools.partial(_ffn_kernel, fox_w=fox_w, d_ff=d_ff),
        grid=(n // tm,),
        in_specs=[
            pl.BlockSpec((tm, d), lambda i: (i, 0)),
            pl.BlockSpec((tm, fox_w), lambda i: (i, 0)),
            pl.BlockSpec((tm, go.shape[1]), lambda i: (i, 0)),
            const(wo.shape), const(wup.shape), const(wdn.shape), const(g2.shape), const(gf.shape),
        ],
        out_specs=pl.BlockSpec((tm, d), lambda i: (i, 0)),
        out_shape=jax.ShapeDtypeStruct((n, d), F32),
        scratch_shapes=[pltpu.VMEM((tm, d_ff), BF16)],
        compiler_params=pltpu.CompilerParams(dimension_semantics=("arbitrary",),
                                             vmem_limit_bytes=VMEM_LIMIT_BYTES),
        name="merge_ffn",
    )(x2d, fo, go, wo, wup, wdn, g2, gf)


def _proj_sample_kernel(x_ref, g1_ref, wtok_ref, wdm_ref, bf_ref, wgate_ref, bgate_ref,
                        q_out, k_out, v_out, lf_out, gq_out, gk_out, gv_out, gg_out, la_out,
                        *, fox_w, fox_dh, gla_kw, gla_vw, n_heads_gla, n_heads_fox):
    dk = gla_kw // n_heads_gla
    xn = _rms(x_ref[...], g1_ref[...]).astype(BF16)
    z = _dot_nt(xn, wtok_ref[...])
    o0 = 0
    q_out[...] = z[:, o0:o0 + fox_w] * (fox_dh ** -0.5)
    o0 += fox_w
    gq_out[...] = z[:, o0:o0 + gla_kw] * (dk ** -0.5)
    o0 += gla_kw
    gk_out[...] = z[:, o0:o0 + gla_kw]
    o0 += gla_kw
    gv_out[...] = z[:, o0:o0 + gla_vw]
    o0 += gla_vw
    gg_out[...] = z[:, o0:o0 + gla_vw]
    o0 += gla_vw
    glr = z[:, o0:o0 + LANES].astype(BF16)
    la_out[...] = _log_sigmoid(_dot(glr, wgate_ref[...]) + bgate_ref[...]) * (1.0 / GLA_GATE_NORM)
    z2 = _dot_nt(xn, wdm_ref[...])
    k_out[...] = z2[:, 0:fox_w]
    v_out[...] = z2[:, fox_w:2 * fox_w]
    lf_out[...] = _log_sigmoid(z2[:, 2 * fox_w:2 * fox_w + n_heads_fox] + bf_ref[...])


def _decode_kernel(pt_ref, q_ref, knew_ref, vnew_ref, lfnew_ref, kc_hbm, vc_hbm, lfc_hbm, o_ref,
                   ring, lfbuf, rev_s, zbuf, acc_s, ustrict_s, sem_ring, sem_lf,
                   *, n_pages, n_b, n_heads, dh):
    b = pl.program_id(0)
    page = LANES
    hw = n_heads * dh
    nch = n_pages // DEC_G
    per_b = 2 * nch
    total = n_b * per_b

    def start_chunk(g):
        bg = g // per_b
        c = g - bg * per_b
        slot = lax.rem(g, DEC_NSLOT)

        @pl.when(c < nch)
        def _():
            for j in range(DEC_G):
                p = n_pages - 1 - (c * DEC_G + j)
                pltpu.make_async_copy(kc_hbm.at[pt_ref[bg, p]], ring.at[slot, j], sem_ring.at[slot]).start()

        @pl.when(c >= nch)
        def _():
            for j in range(DEC_G):
                p = (c - nch) * DEC_G + j
                pltpu.make_async_copy(vc_hbm.at[pt_ref[bg, p]], ring.at[slot, j], sem_ring.at[slot]).start()

    def wait_chunk(g):
        slot = lax.rem(g, DEC_NSLOT)
        for j in range(DEC_G):
            pltpu.make_async_copy(kc_hbm.at[0], ring.at[slot, j], sem_ring.at[slot]).wait()

    def start_lf(bb):
        sl = lax.rem(bb, 2)

        def body(p, _):
            pltpu.make_async_copy(lfc_hbm.at[pt_ref[bb, p]], lfbuf.at[sl, p], sem_lf.at[sl]).start()
            return 0

        lax.fori_loop(0, n_pages, body, 0)

    def wait_lf(bb):
        sl = lax.rem(bb, 2)

        def body(p, _):
            pltpu.make_async_copy(lfc_hbm.at[0], lfbuf.at[sl, p], sem_lf.at[sl]).wait()
            return 0

        lax.fori_loop(0, n_pages, body, 0)

    g0 = b * per_b

    @pl.when(b == 0)
    def _():
        r = lax.broadcasted_iota(jnp.int32, (page, page), 0)
        c = lax.broadcasted_iota(jnp.int32, (page, page), 1)
        ustrict_s[...] = jnp.where(r > c, 1.0, 0.0).astype(BF16)
        start_lf(b)
        for g in range(DEC_NSLOT - 1):
            start_chunk(g0 + g)

    wait_lf(b)

    @pl.when(b + 1 < n_b)
    def _():
        start_lf(b + 1)

    sl = lax.rem(b, 2)
    lf2d = lfbuf[sl].reshape(n_pages * n_heads, page)
    hi, mid, lo = _split3(lf2d)
    u = ustrict_s[...]
    rev = _dot(hi.astype(BF16), u) + _dot(mid.astype(BF16), u) + _dot(lo.astype(BF16), u)
    rev_s[...] = rev.reshape(n_pages, n_heads, page)

    qb = jnp.broadcast_to(q_ref[0], (hw, page))

    def head_sum(x):
        return jnp.sum(x.reshape(n_heads, dh, page), axis=1)

    def head_bcast(x):
        return jnp.broadcast_to(x[:, None, :], (n_heads, dh, page)).reshape(hw, page)

    def k_body(c, carry):
        g = g0 + c

        @pl.when(g + (DEC_NSLOT - 1) < total)
        def _():
            start_chunk(g + (DEC_NSLOT - 1))

        wait_chunk(g)
        slot = lax.rem(g, DEC_NSLOT)
        for j in range(DEC_G):
            p = n_pages - 1 - (c * DEC_G + j)
            s = head_sum(ring[slot, j] * qb)
            revp = rev_s[p]
            zbuf[:, pl.ds(pl.multiple_of(p * page, page), page)] = s + revp + carry
            carry = carry + jnp.broadcast_to(revp[:, 0:1] + lfbuf[sl, p][:, 0:1], (n_heads, page))
        return carry

    carry0 = jnp.broadcast_to(lfnew_ref[0], (n_heads, page))
    lax.fori_loop(0, nch, k_body, carry0)

    z_all = zbuf[...]
    z_new = head_sum(jnp.broadcast_to(knew_ref[0], (hw, page)) * qb)
    m = jnp.maximum(jnp.max(z_all, axis=1, keepdims=True), z_new[:, 0:1])
    p_all = jnp.exp(z_all - m)
    zbuf[...] = p_all
    p_new = jnp.exp(z_new - m)
    l = jnp.sum(p_all, axis=1, keepdims=True) + p_new

    acc_s[...] = jnp.zeros_like(acc_s)

    def v_body(c, carry):
        g = g0 + nch + c

        @pl.when(g + (DEC_NSLOT - 1) < total)
        def _():
            start_chunk(g + (DEC_NSLOT - 1))

        wait_chunk(g)
        slot = lax.rem(g, DEC_NSLOT)
        for j in range(DEC_G):
            p = c * DEC_G + j
            pp = zbuf[:, pl.ds(pl.multiple_of(p * page, page), page)]
            acc_s[...] += ring[slot, j] * head_bcast(pp)
        return carry

    lax.fori_loop(0, nch, v_body, 0)

    num = jnp.sum(acc_s[...], axis=1, keepdims=True) + head_bcast(p_new) * jnp.broadcast_to(vnew_ref[0], (hw, page))
    o_ref[0] = (num / head_bcast(l))[:, 0:1]


def _gla_sample_kernel(s_ref, la_ref, k_ref, q_ref, v_ref, gg_ref, gnorm_ref, s_out, go_out, *, n_heads, dk, dv):
    for h in range(n_heads):
        la = la_ref[0, h * dk:(h + 1) * dk, :]
        kk = k_ref[0, h * dk:(h + 1) * dk, :]
        qq = q_ref[0, h * dk:(h + 1) * dk, :]
        vv = v_ref[0, :, h * dv:(h + 1) * dv]
        s_new = s_ref[0, h] * jnp.exp(la) + kk * vv
        s_out[0, h] = s_new
        o = jnp.sum(qq * s_new, axis=0, keepdims=True)
        o_n = _rms(o, gnorm_ref[...])
        go_out[0, :, h * dv:(h + 1) * dv] = o_n * _silu(gg_ref[0, :, h * dv:(h + 1) * dv])


def kernel(x_prompt, x_sample, cache_k, cache_v, cache_logf, state_gla, page_table, norm1_g, w_in, fox_b_f,
           gla_w_gate_up, gla_b_gate, gla_norm_g, w_o, norm2_g, w_up, w_down, final_g):
    B, S, D = x_prompt.shape
    Bd = x_sample.shape[0]
    depth, n_phys, page, H, dh = cache_k.shape
    _, _, Hg, dk, dv = state_gla.shape
    assert depth == 1 and x_sample.shape[1] == 1 and page == LANES
    fox_w = H * dh
    gla_kw = Hg * dk
    gla_vw = Hg * dv
    rank = gla_w_gate_up.shape[1]
    n_pages = page_table.shape[1]
    d_ff = w_up.shape[2]

    wt = jnp.transpose(w_in[0])
    o_fq, o_fk, o_fv = 0, fox_w, 2 * fox_w
    o_ff = 3 * fox_w
    o_gq = o_ff + H
    o_gk = o_gq + gla_kw
    o_gv = o_gk + gla_kw
    o_glr = o_gv + gla_vw
    o_gg = o_glr + rank
    w_tok = jnp.concatenate([
        wt[o_fq:o_fq + fox_w], wt[o_gq:o_gq + gla_kw], wt[o_gk:o_gk + gla_kw], wt[o_gv:o_gv + gla_vw],
        wt[o_gg:o_gg + gla_vw], wt[o_glr:o_glr + rank], jnp.zeros((LANES - rank, D), F32)], axis=0).astype(BF16)
    w_dm = jnp.concatenate([
        wt[o_fk:o_fk + 2 * fox_w], wt[o_ff:o_ff + H], jnp.zeros((2 * SUBLANES - H, D), F32)], axis=0).astype(BF16)
    w_gate = jnp.concatenate([gla_w_gate_up[0], jnp.zeros((LANES - rank, gla_kw), F32)], axis=0).astype(BF16)
    wo_bf = w_o[0].astype(BF16)
    wup_bf = w_up[0].astype(BF16)
    wdn_bf = w_down[0].astype(BF16)
    g1 = norm1_g.reshape(1, D)
    g2 = norm2_g.reshape(1, D)
    gf = final_g.reshape(1, D)
    bgate = gla_b_gate.reshape(1, gla_kw)
    gnorm = gla_norm_g.reshape(1, dv)

    cparams = lambda sem: pltpu.CompilerParams(dimension_semantics=sem, vmem_limit_bytes=VMEM_LIMIT_BYTES)
    single = lambda shape: pl.BlockSpec(shape, lambda *_: (0,) * len(shape), pipeline_mode=pl.Buffered(1))

    tm = PROJ_TM
    nt = S // tm
    n_tok_cols = w_tok.shape[0]
    n_dm_rows = w_dm.shape[0]
    q_p, kT_p, vT_p, lfT_p, cT_p, go_p, sfin_p = pl.pallas_call(
        functools.partial(_proj_prompt_kernel, tm=tm, fox_w=fox_w, fox_dh=dh, gla_kw=gla_kw, gla_vw=gla_vw,
                          n_heads_gla=Hg),
        grid=(B, nt),
        in_specs=[
            pl.BlockSpec((1, tm, D), lambda b, t: (b, t, 0)),
            single((1, D)), single((n_tok_cols, D)), single((n_dm_rows, D)), single((H, 1)),
            single((LANES, gla_kw)), single((1, gla_kw)), single((1, dv)),
        ],
        out_specs=[
            pl.BlockSpec((1, tm, fox_w), lambda b, t: (b, t, 0)),
            pl.BlockSpec((1, fox_w, tm), lambda b, t: (b, 0, t)),
            pl.BlockSpec((1, fox_w, tm), lambda b, t: (b, 0, t)),
            pl.BlockSpec((1, H, tm), lambda b, t: (b, 0, t)),
            pl.BlockSpec((1, H, tm), lambda b, t: (b, 0, t)),
            pl.BlockSpec((1, tm, gla_vw), lambda b, t: (b, t, 0)),
            pl.BlockSpec((1, Hg, dk, dv), lambda b, t: (b, 0, 0, 0)),
        ],
        out_shape=[
            jax.ShapeDtypeStruct((B, S, fox_w), BF16),
            jax.ShapeDtypeStruct((B, fox_w, S), F32),
            jax.ShapeDtypeStruct((B, fox_w, S), F32),
            jax.ShapeDtypeStruct((B, H, S), F32),
            jax.ShapeDtypeStruct((B, H, S), F32),
            jax.ShapeDtypeStruct((B, S, gla_vw), BF16),
            jax.ShapeDtypeStruct((B, Hg, dk, dv), F32),
        ],
        scratch_shapes=[
            pltpu.VMEM((H, LANES), F32),
            pltpu.VMEM((gla_vw, gla_kw), F32),
            pltpu.VMEM((tm, tm), BF16),
            pltpu.VMEM((GLA_CHUNK, GLA_CHUNK), BF16),
            pltpu.VMEM((tm, gla_kw), F32), pltpu.VMEM((tm, gla_kw), F32),
            pltpu.VMEM((tm, gla_vw), F32), pltpu.VMEM((tm, gla_vw), F32), pltpu.VMEM((tm, gla_kw), F32),
        ],
        compiler_params=cparams(("arbitrary", "arbitrary")),
        name="proj_gla_prompt",
    )(x_prompt, g1, w_tok, w_dm, fox_b_f.reshape(H, 1), w_gate, bgate, gnorm)

    t_blk = ATT_T
    nq = S // t_blk
    n_pairs = fox_w // LANES
    c4 = cT_p.reshape(B, n_pairs, 2, S)
    fo_p = pl.pallas_call(
        functools.partial(_attn_prompt_kernel, t_blk=t_blk, dh=dh),
        grid=(B, n_pairs, nq),
        in_specs=[
            pl.BlockSpec((1, t_blk, LANES), lambda b, p, i: (b, i, p)),
            pl.BlockSpec((1, LANES, S), lambda b, p, i: (b, p, 0)),
            pl.BlockSpec((1, LANES, S), lambda b, p, i: (b, p, 0)),
            pl.BlockSpec((1, 1, 2, S), lambda b, p, i: (b, p, 0, 0)),
        ],
        out_specs=pl.BlockSpec((1, t_blk, LANES), lambda b, p, i: (b, i, p)),
        out_shape=jax.ShapeDtypeStruct((B, S, fox_w), BF16),
        scratch_shapes=[
            pltpu.VMEM((LANES, S), BF16), pltpu.VMEM((LANES, S), BF16),
            pltpu.VMEM((t_blk, 1), F32), pltpu.VMEM((t_blk, 1), F32), pltpu.VMEM((t_blk, LANES), F32),
        ],
        compiler_params=cparams(("arbitrary", "arbitrary", "arbitrary")),
        name="fox_attn_prompt",
    )(q_p, kT_p, vT_p, c4)

    y_p = _ffn_call(x_prompt.reshape(B * S, D), fo_p.reshape(B * S, fox_w), go_p.reshape(B * S, gla_vw),
                    wo_bf, wup_bf, wdn_bf, g2, gf, FFN_TM).reshape(B, S, D)

    xs = x_sample.reshape(Bd, D)
    full = lambda shape: pl.BlockSpec(shape, lambda: (0,) * len(shape))
    s_shapes = [(Bd, fox_w), (Bd, fox_w), (Bd, fox_w), (Bd, H), (Bd, gla_kw), (Bd, gla_kw), (Bd, gla_vw),
                (Bd, gla_vw), (Bd, gla_kw)]
    q_s, k_s, v_s, lf_s, gq_s, gk_s, gv_s, gg_s, la_s = pl.pallas_call(
        functools.partial(_proj_sample_kernel, fox_w=fox_w, fox_dh=dh, gla_kw=gla_kw, gla_vw=gla_vw, n_heads_gla=Hg,
                          n_heads_fox=H),
        in_specs=[full((Bd, D)), full((1, D)), full((n_tok_cols, D)), full((n_dm_rows, D)), full((1, H)),
                  full((LANES, gla_kw)), full((1, gla_kw))],
        out_specs=[full(s) for s in s_shapes],
        out_shape=[jax.ShapeDtypeStruct(s, F32) for s in s_shapes],
        compiler_params=pltpu.CompilerParams(vmem_limit_bytes=VMEM_LIMIT_BYTES),
        name="proj_sample",
    )(xs, g1, w_tok, w_dm, fox_b_f.reshape(1, H), w_gate, bgate)

    kc = jnp.transpose(cache_k[0], (0, 2, 3, 1)).reshape(n_phys, fox_w, page)
    vc = jnp.transpose(cache_v[0], (0, 2, 3, 1)).reshape(n_phys, fox_w, page)
    lfc = jnp.transpose(cache_logf[0], (0, 2, 1))
    col = lambda a: a.reshape(Bd, a.shape[1], 1)
    grid_spec = pltpu.PrefetchScalarGridSpec(
        num_scalar_prefetch=1,
        grid=(Bd,),
        in_specs=[
            pl.BlockSpec((1, fox_w, 1), lambda b, pt: (b, 0, 0)),
            pl.BlockSpec((1, fox_w, 1), lambda b, pt: (b, 0, 0)),
            pl.BlockSpec((1, fox_w, 1), lambda b, pt: (b, 0, 0)),
            pl.BlockSpec((1, H, 1), lambda b, pt: (b, 0, 0)),
            pl.BlockSpec(memory_space=pl.ANY), pl.BlockSpec(memory_space=pl.ANY), pl.BlockSpec(memory_space=pl.ANY),
        ],
        out_specs=pl.BlockSpec((1, fox_w, 1), lambda b, pt: (b, 0, 0)),
        scratch_shapes=[
            pltpu.VMEM((DEC_NSLOT, DEC_G, fox_w, page), F32),
            pltpu.VMEM((2, n_pages, H, page), F32),
            pltpu.VMEM((n_pages, H, page), F32),
            pltpu.VMEM((H, n_pages * page), F32),
            pltpu.VMEM((fox_w, page), F32),
            pltpu.VMEM((page, page), BF16),
            pltpu.SemaphoreType.DMA((DEC_NSLOT,)),
            pltpu.SemaphoreType.DMA((2,)),
        ],
    )
    fo_s = pl.pallas_call(
        functools.partial(_decode_kernel, n_pages=n_pages, n_b=Bd, n_heads=H, dh=dh),
        grid_spec=grid_spec,
        out_shape=jax.ShapeDtypeStruct((Bd, fox_w, 1), F32),
        compiler_params=cparams(("arbitrary",)),
        name="fox_decode",
    )(page_table, col(q_s), col(k_s), col(v_s), col(lf_s), kc, vc, lfc)
    fo_s = fo_s.reshape(Bd, fox_w).astype(BF16)

    s_new, go_s = pl.pallas_call(
        functools.partial(_gla_sample_kernel, n_heads=Hg, dk=dk, dv=dv),
        grid=(Bd,),
        in_specs=[
            pl.BlockSpec((1, Hg, dk, dv), lambda b: (b, 0, 0, 0)),
            pl.BlockSpec((1, gla_kw, 1), lambda b: (b, 0, 0)),
            pl.BlockSpec((1, gla_kw, 1), lambda b: (b, 0, 0)),
            pl.BlockSpec((1, gla_kw, 1), lambda b: (b, 0, 0)),
            pl.BlockSpec((1, 1, gla_vw), lambda b: (b, 0, 0)),
            pl.BlockSpec((1, 1, gla_vw), lambda b: (b, 0, 0)),
            pl.BlockSpec((1, dv), lambda b: (0, 0)),
        ],
        out_specs=[
            pl.BlockSpec((1, Hg, dk, dv), lambda b: (b, 0, 0, 0)),
            pl.BlockSpec((1, 1, gla_vw), lambda b: (b, 0, 0)),
        ],
        out_shape=[jax.ShapeDtypeStruct((Bd, Hg, dk, dv), F32), jax.ShapeDtypeStruct((Bd, 1, gla_vw), F32)],
        compiler_params=cparams(("arbitrary",)),
        name="gla_sample",
    )(state_gla[0], col(la_s), col(gk_s), col(gq_s), gv_s.reshape(Bd, 1, gla_vw), gg_s.reshape(Bd, 1, gla_vw), gnorm)
    go_s = go_s.reshape(Bd, gla_vw).astype(BF16)

    y_s = _ffn_call(xs, fo_s, go_s, wo_bf, wup_bf, wdn_bf, g2, gf, Bd).reshape(Bd, 1, D)

    new_k_p = jnp.transpose(kT_p.reshape(1, B, H, dh, S), (0, 1, 4, 2, 3))
    new_v_p = jnp.transpose(vT_p.reshape(1, B, H, dh, S), (0, 1, 4, 2, 3))
    new_lf_p = jnp.transpose(lfT_p, (0, 2, 1)).reshape(1, B, S, H)
    return (y_p, y_s, new_k_p, new_v_p, new_lf_p, sfin_p.reshape(1, B, Hg, dk, dv),
            k_s.reshape(1, Bd, 1, H, dh), v_s.reshape(1, Bd, 1, H, dh), lf_s.reshape(1, Bd, 1, H),
            s_new.reshape(1, Bd, Hg, dk, dv))
```

```python
import functools

import jax
import jax.numpy as jnp
from jax import lax
from jax.experimental import pallas as pl
from jax.experimental.pallas import tpu as pltpu

F32 = jnp.float32
BF16 = jnp.bfloat16

LANES = 128
SUBLANES = 8
VMEM_LIMIT_BYTES = 56 * 1024 * 1024

EPS = 1e-6
LOG2E = 1.4426950408889634
N_AUG = 3
GLA_GATE_NORM = 16.0
GLA_CHUNK = 128
GLA_SUB = 32
PROJ_TM = 512
ATT_T = 512
FFN_TM = 512
FFN_CHUNK = 1024
DEC_G = 8
DEC_NSLOT = 3


def _dot(a, b):
    return jnp.dot(a, b, preferred_element_type=F32)


def _dot_nt(a, b):
    return lax.dot_general(a, b, (((1,), (1,)), ((), ())), preferred_element_type=F32)


def _split3(x):
    hi = x.astype(BF16).astype(F32)
    r = x - hi
    mid = r.astype(BF16).astype(F32)
    lo = r - mid
    return hi, mid, lo


def _log_sigmoid(x):
    return jnp.minimum(x, 0.0) - jnp.log1p(jnp.exp(-jnp.abs(x)))


def _silu(x):
    return x / (1.0 + jnp.exp(-x))


def _rms(x, g):
    return x * lax.rsqrt(jnp.mean(x * x, axis=-1, keepdims=True) + EPS) * g


def _full_spec(shape):
    n = len(shape)
    return pl.BlockSpec(shape, lambda *_: (0,) * n)


def _proj_prompt_kernel(x_ref, g1_ref, wtok_ref, wdm_ref, bf_ref, wgate_ref, bgate_ref, gnorm_ref,
                        q_out, kT_out, vT_out, lfT_out, cT_out, go_out, sfin_out,
                        carry_s, bdt_s, uincl_s, lincl_s, gq_s, gk_s, gv_s, gg_s, la_s,
                        *, tm, fox_w, fox_dh, gla_kw, gla_vw, n_heads_gla):
    t = pl.program_id(1)
    nt = pl.num_programs(1)
    dk = gla_kw // n_heads_gla
    dv = gla_vw // n_heads_gla

    @pl.when(jnp.logical_and(pl.program_id(0) == 0, t == 0))
    def _():
        r = lax.broadcasted_iota(jnp.int32, (tm, tm), 0)
        c = lax.broadcasted_iota(jnp.int32, (tm, tm), 1)
        uincl_s[...] = jnp.where(r <= c, 1.0, 0.0).astype(BF16)
        r = lax.broadcasted_iota(jnp.int32, (GLA_CHUNK, GLA_CHUNK), 0)
        c = lax.broadcasted_iota(jnp.int32, (GLA_CHUNK, GLA_CHUNK), 1)
        lincl_s[...] = jnp.where(c <= r, 1.0, 0.0).astype(BF16)

    @pl.when(t == 0)
    def _():
        carry_s[...] = jnp.zeros_like(carry_s)
        bdt_s[...] = jnp.zeros_like(bdt_s)

    x = x_ref[0]
    xn = _rms(x, g1_ref[...]).astype(BF16)

    z = _dot_nt(xn, wtok_ref[...])
    o0 = 0
    q_out[0] = (z[:, o0:o0 + fox_w] * (fox_dh ** -0.5 * LOG2E)).astype(BF16)
    o0 += fox_w
    gq_s[...] = z[:, o0:o0 + gla_kw] * (dk ** -0.5)
    o0 += gla_kw
    gk_s[...] = z[:, o0:o0 + gla_kw]
    o0 += gla_kw
    gv_s[...] = z[:, o0:o0 + gla_vw]
    o0 += gla_vw
    gg_s[...] = z[:, o0:o0 + gla_vw]
    o0 += gla_vw
    glr = z[:, o0:o0 + LANES].astype(BF16)
    pre = _dot(glr, wgate_ref[...]) + bgate_ref[...]
    la_s[...] = _log_sigmoid(pre) * (1.0 / GLA_GATE_NORM)

    zt = _dot_nt(wdm_ref[...], xn)
    kT_out[0] = zt[0:fox_w]
    vT_out[0] = zt[fox_w:2 * fox_w]
    lf = _log_sigmoid(zt[2 * fox_w:2 * fox_w + SUBLANES] + bf_ref[...])
    lfT_out[0] = lf
    hi, mid, lo = _split3(lf)
    stack = jnp.concatenate([hi, mid, lo, jnp.zeros_like(hi)], axis=0).astype(BF16)
    cs = _dot(stack, uincl_s[...])
    cs = cs[0:8] + cs[8:16] + cs[16:24]
    carry = carry_s[...]
    cT_out[0] = cs + carry[:, 0:1]
    tot = _dot(stack, jnp.ones((tm, LANES), BF16))
    carry_s[...] = carry + tot[0:8] + tot[8:16] + tot[16:24]

    nsub = GLA_CHUNK // GLA_SUB
    rowi = lax.broadcasted_iota(jnp.int32, (GLA_CHUNK, gla_kw), 0)
    lanei = lax.broadcasted_iota(jnp.int32, (GLA_CHUNK, gla_kw), 1)
    ar = lax.broadcasted_iota(jnp.int32, (GLA_CHUNK, GLA_CHUNK), 0)
    ac = lax.broadcasted_iota(jnp.int32, (GLA_CHUNK, GLA_CHUNK), 1)
    tri_blk = jnp.logical_and(ar // GLA_SUB == ac // GLA_SUB, ar >= ac)
    br = lax.broadcasted_iota(jnp.int32, (gla_vw, gla_kw), 0)
    bc = lax.broadcasted_iota(jnp.int32, (gla_vw, gla_kw), 1)
    bd_mask = (br // dv) == (bc // dk)

    def chunk_body(ci, _):
        r0 = pl.multiple_of(ci * GLA_CHUNK, GLA_CHUNK)
        la_c = la_s[pl.ds(r0, GLA_CHUNK), :]
        gq_c = gq_s[pl.ds(r0, GLA_CHUNK), :]
        gk_c = gk_s[pl.ds(r0, GLA_CHUNK), :]
        gv_c = gv_s[pl.ds(r0, GLA_CHUNK), :]
        gg_c = gg_s[pl.ds(r0, GLA_CHUNK), :]
        h3, m3, l3 = _split3(la_c)
        st = jnp.concatenate([h3, m3, l3], axis=1).astype(BF16)
        bb = _dot(lincl_s[...], st)
        b = bb[:, 0:gla_kw] + bb[:, gla_kw:2 * gla_kw] + bb[:, 2 * gla_kw:3 * gla_kw]
        bmid_l, bend_l, b0_l = [], [], []
        for i in range(nsub):
            s0 = i * GLA_SUB
            bmid_l.append(b[s0 + GLA_SUB // 2:s0 + GLA_SUB // 2 + 1])
            bend_l.append(b[s0 + GLA_SUB - 1:s0 + GLA_SUB])
            b0_l.append(jnp.zeros((1, gla_kw), F32) if i == 0 else b[s0 - 1:s0])
        bc_rows = lambda rows: jnp.concatenate(
            [jnp.broadcast_to(r, (GLA_SUB, gla_kw)) for r in rows], axis=0)
        bmid, bend, b0 = bc_rows(bmid_l), bc_rows(bend_l), bc_rows(b0_l)
        qt = (gq_c * jnp.exp(b - bmid)).astype(BF16)
        kt = (gk_c * jnp.exp(bmid - b)).astype(BF16)
        qp = gq_c * jnp.exp(b - b0)
        kd = gk_c * jnp.exp(bend - b)
        gv_bf = gv_c.astype(BF16)
        gvT_bf = gv_c.T.astype(BF16)

        o_inter = jnp.zeros((GLA_CHUNK, gla_vw), F32)
        for i in range(nsub):
            rm = jnp.logical_and(rowi >= i * GLA_SUB, rowi < (i + 1) * GLA_SUB)
            bdt = bdt_s[...]
            o_inter = o_inter + _dot_nt(jnp.where(rm, qp, 0.0).astype(BF16), bdt.astype(BF16))
            ut = _dot(gvT_bf, jnp.where(rm, kd, 0.0).astype(BF16))
            decay = jnp.exp(bend_l[i] - b0_l[i])
            bdt_s[...] = bdt * decay + jnp.where(bd_mask, ut, 0.0)

        for h in range(n_heads_gla):
            hm = jnp.logical_and(lanei >= h * dk, lanei < (h + 1) * dk)
            a = _dot_nt(jnp.where(hm, qt, jnp.zeros_like(qt)), kt)
            a = jnp.where(tri_blk, a, 0.0).astype(BF16)
            o_h = _dot(a, gv_bf[:, h * dv:(h + 1) * dv]) + o_inter[:, h * dv:(h + 1) * dv]
            o_n = _rms(o_h, gnorm_ref[...])
            go = o_n * _silu(gg_c[:, h * dv:(h + 1) * dv])
            go_out[0, pl.ds(r0, GLA_CHUNK), h * dv:(h + 1) * dv] = go.astype(BF16)
        return 0

    lax.fori_loop(0, tm // GLA_CHUNK, chunk_body, 0)

    @pl.when(t == nt - 1)
    def _():
        bd = bdt_s[...].T
        for h in range(n_heads_gla):
            sfin_out[0, h] = bd[h * dk:(h + 1) * dk, h * dv:(h + 1) * dv]


def _attn_prompt_kernel(q_ref, kT_ref, vT_ref, c_ref, o_ref, kaug_s, vaug_s, qa_s, m_s, acc_s, *, t_blk, dh):
    i = pl.program_id(2)
    n_t = kaug_s.shape[2] // t_blk
    hw2 = 2 * dh

    @pl.when(i == 0)
    def _():
        vaug_s[0:hw2, :] = vT_ref[0].astype(BF16)
        vaug_s[hw2:, :] = jnp.ones((vaug_s.shape[0] - hw2, vaug_s.shape[1]), BF16)
        rowk = lax.broadcasted_iota(jnp.int32, (hw2, t_blk), 0)

        def tile_body(kb, carry):
            k0 = pl.multiple_of(kb * t_blk, t_blk)
            kt = kT_ref[0, :, pl.ds(k0, t_blk)]
            for h in range(2):
                crow = c_ref[0, 0, h:h + 1, pl.ds(k0, t_blk)]
                d = (crow - crow[:, 0:1]) * LOG2E
                parts = _split3(d)
                spare = (1 - h) * dh
                aug = jnp.zeros((hw2, t_blk), F32)
                for n, part in enumerate(parts):
                    aug = jnp.where(rowk == spare + n, jnp.broadcast_to(-part, (hw2, t_blk)), aug)
                head_rows = jnp.logical_and(rowk >= h * dh, rowk < (h + 1) * dh)
                kaug_s[h, :, pl.ds(k0, t_blk)] = jnp.where(head_rows, kt, aug).astype(BF16)
            return carry

        lax.fori_loop(0, n_t, tile_body, 0)

    q2 = q_ref[0]
    lane = lax.broadcasted_iota(jnp.int32, (t_blk, hw2), 1)
    row = lax.broadcasted_iota(jnp.int32, (t_blk, t_blk), 0)
    col = lax.broadcasted_iota(jnp.int32, (t_blk, t_blk), 1)
    t0 = pl.multiple_of(i * t_blk, t_blk)
    c_q = []
    for h in range(2):
        spare = (1 - h) * dh
        head_l = jnp.logical_and(lane >= h * dh, lane < (h + 1) * dh)
        ones_l = jnp.logical_and(lane >= spare, lane < spare + N_AUG)
        qa_s[h] = jnp.where(head_l, q2, jnp.where(ones_l, 1.0, 0.0).astype(BF16))
        m_s[h] = jnp.full(m_s.shape[1:], -jnp.inf, F32)
        acc_s[h] = jnp.zeros(acc_s.shape[1:], F32)
        c_q.append(c_ref[0, 0, h:h + 1, pl.ds(t0, LANES)][:, 0:1])

    def step(j, masked):
        k0 = pl.multiple_of(j * t_blk, t_blk)
        for h in range(2):
            s = _dot(qa_s[h], kaug_s[h, :, pl.ds(k0, t_blk)])
            if masked:
                s = jnp.where(col <= row, s, -jnp.inf)
            off = (c_ref[0, 0, h:h + 1, pl.ds(k0, LANES)][:, 0:1] - c_q[h]) * LOG2E
            m_old = m_s[h]
            m_new = jnp.maximum(m_old, jnp.max(s, axis=1, keepdims=True) - off)
            p = jnp.exp2(s - (m_new + off))
            alpha = jnp.exp2(m_old - m_new)
            pv = _dot_nt(p.astype(BF16), vaug_s[:, pl.ds(k0, t_blk)])
            acc_s[h] = alpha * acc_s[h] + pv
            m_s[h] = m_new

    def body(j, carry):
        step(j, False)
        return carry

    lax.fori_loop(0, i, body, 0)
    step(i, True)
    outs = [acc_s[h][:, 0:hw2] / acc_s[h][:, hw2:hw2 + 1] for h in range(2)]
    o_ref[0] = jnp.where(lane < dh, outs[0], outs[1]).astype(BF16)


def _ffn_kernel(x_ref, fo_ref, go_ref, wo_ref, wup_ref, wdn_ref, g2_ref, gf_ref, y_ref, u_s, *, fox_w, d_ff):
    x = x_ref[...]
    h = x + (_dot(fo_ref[...], wo_ref[0:fox_w, :]) + _dot(go_ref[...], wo_ref[fox_w:, :]))
    hn = _rms(h, g2_ref[...]).astype(BF16)
    for c in range(d_ff // FFN_CHUNK):
        u = _dot(hn, wup_ref[:, c * FFN_CHUNK:(c + 1) * FFN_CHUNK])
        u_s[:, c * FFN_CHUNK:(c + 1) * FFN_CHUNK] = jnp.square(jnp.maximum(u, 0.0)).astype(BF16)
    y_ref[...] = _rms(h + _dot(u_s[...], wdn_ref[...]), gf_ref[...])


def _ffn_call(x2d, fo, go, wo, wup, wdn, g2, gf, tm):
    n, d = x2d.shape
    fox_w = fo.shape[1]
    d_ff = wup.shape[1]
    const = lambda shape: pl.BlockSpec(shape, lambda i: (0, 0), pipeline_mode=pl.Buffered(1))
    return pl.pallas_call(
        functools.partial(_ffn_kernel, fox_w=fox_w, d_ff=d_ff),
        grid=(n // tm,),
        in_specs=[
            pl.BlockSpec((tm, d), lambda i: (i, 0)),
            pl.BlockSpec((tm, fox_w), lambda i: (i, 0)),
            pl.BlockSpec((tm, go.shape[1]), lambda i: (i, 0)),
            const(wo.shape), const(wup.shape), const(wdn.shape), const(g2.shape), const(gf.shape),
        ],
        out_specs=pl.BlockSpec((tm, d), lambda i: (i, 0)),
        out_shape=jax.ShapeDtypeStruct((n, d), F32),
        scratch_shapes=[pltpu.VMEM((tm, d_ff), BF16)],
        compiler_params=pltpu.CompilerParams(dimension_semantics=("arbitrary",),
                                             vmem_limit_bytes=VMEM_LIMIT_BYTES),
        name="merge_ffn",
    )(x2d, fo, go, wo, wup, wdn, g2, gf)


def _proj_sample_kernel(x_ref, g1_ref, wtok_ref, wdm_ref, bf_ref, wgate_ref, bgate_ref,
                        q_out, k_out, v_out, lf_out, gq_out, gk_out, gv_out, gg_out, la_out,
                        *, fox_w, fox_dh, gla_kw, gla_vw, n_heads_gla, n_heads_fox):
    dk = gla_kw // n_heads_gla
    xn = _rms(x_ref[...], g1_ref[...]).astype(BF16)
    z = _dot_nt(xn, wtok_ref[...])
    o0 = 0
    q_out[...] = z[:, o0:o0 + fox_w] * (fox_dh ** -0.5)
    o0 += fox_w
    gq_out[...] = z[:, o0:o0 + gla_kw] * (dk ** -0.5)
    o0 += gla_kw
    gk_out[...] = z[:, o0:o0 + gla_kw]
    o0 += gla_kw
    gv_out[...] = z[:, o0:o0 + gla_vw]
    o0 += gla_vw
    gg_out[...] = z[:, o0:o0 + gla_vw]
    o0 += gla_vw
    glr = z[:, o0:o0 + LANES].astype(BF16)
    la_out[...] = _log_sigmoid(_dot(glr, wgate_ref[...]) + bgate_ref[...]) * (1.0 / GLA_GATE_NORM)
    z2 = _dot_nt(xn, wdm_ref[...])
    k_out[...] = z2[:, 0:fox_w]
    v_out[...] = z2[:, fox_w:2 * fox_w]
    lf_out[...] = _log_sigmoid(z2[:, 2 * fox_w:2 * fox_w + n_heads_fox] + bf_ref[...])


def _decode_kernel(pt_ref, q_ref, knew_ref, vnew_ref, lfnew_ref, kc_hbm, vc_hbm, lfc_hbm, o_ref,
                   ring, lfbuf, rev_s, zbuf, acc_s, ustrict_s, sem_ring, sem_lf,
                   *, n_pages, n_b, n_heads, dh):
    b = pl.program_id(0)
    page = LANES
    hw = n_heads * dh
    nch = n_pages // DEC_G
    per_b = 2 * nch
    total = n_b * per_b

    def start_chunk(g):
        bg = g // per_b
        c = g - bg * per_b
        slot = lax.rem(g, DEC_NSLOT)

        @pl.when(c < nch)
        def _():
            for j in range(DEC_G):
                p = n_pages - 1 - (c * DEC_G + j)
                pltpu.make_async_copy(kc_hbm.at[pt_ref[bg, p]], ring.at[slot, j], sem_ring.at[slot]).start()

        @pl.when(c >= nch)
        def _():
            for j in range(DEC_G):
                p = (c - nch) * DEC_G + j
                pltpu.make_async_copy(vc_hbm.at[pt_ref[bg, p]], ring.at[slot, j], sem_ring.at[slot]).start()

    def wait_chunk(g):
        slot = lax.rem(g, DEC_NSLOT)
        for j in range(DEC_G):
            pltpu.make_async_copy(kc_hbm.at[0], ring.at[slot, j], sem_ring.at[slot]).wait()

    def start_lf(bb):
        sl = lax.rem(bb, 2)

        def body(p, _):
            pltpu.make_async_copy(lfc_hbm.at[pt_ref[bb, p]], lfbuf.at[sl, p], sem_lf.at[sl]).start()
            return 0

        lax.fori_loop(0, n_pages, body, 0)

    def wait_lf(bb):
        sl = lax.rem(bb, 2)

        def body(p, _):
            pltpu.make_async_copy(lfc_hbm.at[0], lfbuf.at[sl, p], sem_lf.at[sl]).wait()
            return 0

        lax.fori_loop(0, n_pages, body, 0)

    g0 = b * per_b

    @pl.when(b == 0)
    def _():
        r = lax.broadcasted_iota(jnp.int32, (page, page), 0)
        c = lax.broadcasted_iota(jnp.int32, (page, page), 1)
        ustrict_s[...] = jnp.where(r > c, 1.0, 0.0).astype(BF16)
        start_lf(b)
        for g in range(DEC_NSLOT - 1):
            start_chunk(g0 + g)

    wait_lf(b)

    @pl.when(b + 1 < n_b)
    def _():
        start_lf(b + 1)

    sl = lax.rem(b, 2)
    lf2d = lfbuf[sl].reshape(n_pages * n_heads, page)
    hi, mid, lo = _split3(lf2d)
    u = ustrict_s[...]
    rev = _dot(hi.astype(BF16), u) + _dot(mid.astype(BF16), u) + _dot(lo.astype(BF16), u)
    rev_s[...] = rev.reshape(n_pages, n_heads, page)

    qb = jnp.broadcast_to(q_ref[0], (hw, page))

    def head_sum(x):
        return jnp.sum(x.reshape(n_heads, dh, page), axis=1)

    def head_bcast(x):
        return jnp.broadcast_to(x[:, None, :], (n_heads, dh, page)).reshape(hw, page)

    def k_body(c, carry):
        g = g0 + c

        @pl.when(g + (DEC_NSLOT - 1) < total)
        def _():
            start_chunk(g + (DEC_NSLOT - 1))

        wait_chunk(g)
        slot = lax.rem(g, DEC_NSLOT)
        for j in range(DEC_G):
            p = n_pages - 1 - (c * DEC_G + j)
            s = head_sum(ring[slot, j] * qb)
            revp = rev_s[p]
            zbuf[:, pl.ds(pl.multiple_of(p * page, page), page)] = s + revp + carry
            carry = carry + jnp.broadcast_to(revp[:, 0:1] + lfbuf[sl, p][:, 0:1], (n_heads, page))
        return carry

    carry0 = jnp.broadcast_to(lfnew_ref[0], (n_heads, page))
    lax.fori_loop(0, nch, k_body, carry0)

    z_all = zbuf[...]
    z_new = head_sum(jnp.broadcast_to(knew_ref[0], (hw, page)) * qb)
    m = jnp.maximum(jnp.max(z_all, axis=1, keepdims=True), z_new[:, 0:1])
    p_all = jnp.exp(z_all - m)
    zbuf[...] = p_all
    p_new = jnp.exp(z_new - m)
    l = jnp.sum(p_all, axis=1, keepdims=True) + p_new

    acc_s[...] = jnp.zeros_like(acc_s)

    def v_body(c, carry):
        g = g0 + nch + c

        @pl.when(g + (DEC_NSLOT - 1) < total)
        def _():
            start_chunk(g + (DEC_NSLOT - 1))

        wait_chunk(g)
        slot = lax.rem(g, DEC_NSLOT)
        for j in range(DEC_G):
            p = c * DEC_G + j
            pp = zbuf[:, pl.ds(pl.multiple_of(p * page, page), page)]
            acc_s[...] += ring[slot, j] * head_bcast(pp)
        return carry

    lax.fori_loop(0, nch, v_body, 0)

    num = jnp.sum(acc_s[...], axis=1, keepdims=True) + head_bcast(p_new) * jnp.broadcast_to(vnew_ref[0], (hw, page))
    o_ref[0] = (num / head_bcast(l))[:, 0:1]


def _gla_sample_kernel(s_ref, la_ref, k_ref, q_ref, v_ref, gg_ref, gnorm_ref, s_out, go_out, *, n_heads, dk, dv):
    for h in range(n_heads):
        la = la_ref[0, h * dk:(h + 1) * dk, :]
        kk = k_ref[0, h * dk:(h + 1) * dk, :]
        qq = q_ref[0, h * dk:(h + 1) * dk, :]
        vv = v_ref[0, :, h * dv:(h + 1) * dv]
        s_new = s_ref[0, h] * jnp.exp(la) + kk * vv
        s_out[0, h] = s_new
        o = jnp.sum(qq * s_new, axis=0, keepdims=True)
        o_n = _rms(o, gnorm_ref[...])
        go_out[0, :, h * dv:(h + 1) * dv] = o_n * _silu(gg_ref[0, :, h * dv:(h + 1) * dv])


def kernel(x_prompt, x_sample, cache_k, cache_v, cache_logf, state_gla, page_table, norm1_g, w_in, fox_b_f,
           gla_w_gate_up, gla_b_gate, gla_norm_g, w_o, norm2_g, w_up, w_down, final_g):
    B, S, D = x_prompt.shape
    Bd = x_sample.shape[0]
    depth, n_phys, page, H, dh = cache_k.shape
    _, _, Hg, dk, dv = state_gla.shape
    assert depth == 1 and x_sample.shape[1] == 1 and page == LANES
    fox_w = H * dh
    gla_kw = Hg * dk
    gla_vw = Hg * dv
    rank = gla_w_gate_up.shape[1]
    n_pages = page_table.shape[1]
    d_ff = w_up.shape[2]

    wt = jnp.transpose(w_in[0])
    o_fq, o_fk, o_fv = 0, fox_w, 2 * fox_w
    o_ff = 3 * fox_w
    o_gq = o_ff + H
    o_gk = o_gq + gla_kw
    o_gv = o_gk + gla_kw
    o_glr = o_gv + gla_vw
    o_gg = o_glr + rank
    w_tok = jnp.concatenate([
        wt[o_fq:o_fq + fox_w], wt[o_gq:o_gq + gla_kw], wt[o_gk:o_gk + gla_kw], wt[o_gv:o_gv + gla_vw],
        wt[o_gg:o_gg + gla_vw], wt[o_glr:o_glr + rank], jnp.zeros((LANES - rank, D), F32)], axis=0).astype(BF16)
    w_dm = jnp.concatenate([
        wt[o_fk:o_fk + 2 * fox_w], wt[o_ff:o_ff + H], jnp.zeros((2 * SUBLANES - H, D), F32)], axis=0).astype(BF16)
    w_gate = jnp.concatenate([gla_w_gate_up[0], jnp.zeros((LANES - rank, gla_kw), F32)], axis=0).astype(BF16)
    wo_bf = w_o[0].astype(BF16)
    wup_bf = w_up[0].astype(BF16)
    wdn_bf = w_down[0].astype(BF16)
    g1 = norm1_g.reshape(1, D)
    g2 = norm2_g.reshape(1, D)
    gf = final_g.reshape(1, D)
    bgate = gla_b_gate.reshape(1, gla_kw)
    gnorm = gla_norm_g.reshape(1, dv)

    cparams = lambda sem: pltpu.CompilerParams(dimension_semantics=sem, vmem_limit_bytes=VMEM_LIMIT_BYTES)
    single = lambda shape: pl.BlockSpec(shape, lambda *_: (0,) * len(shape), pipeline_mode=pl.Buffered(1))

    tm = PROJ_TM
    nt = S // tm
    n_tok_cols = w_tok.shape[0]
    n_dm_rows = w_dm.shape[0]
    q_p, kT_p, vT_p, lfT_p, cT_p, go_p, sfin_p = pl.pallas_call(
        functools.partial(_proj_prompt_kernel, tm=tm, fox_w=fox_w, fox_dh=dh, gla_kw=gla_kw, gla_vw=gla_vw,
                          n_heads_gla=Hg),
        grid=(B, nt),
        in_specs=[
            pl.BlockSpec((1, tm, D), lambda b, t: (b, t, 0)),
            single((1, D)), single((n_tok_cols, D)), single((n_dm_rows, D)), single((H, 1)),
            single((LANES, gla_kw)), single((1, gla_kw)), single((1, dv)),
        ],
        out_specs=[
            pl.BlockSpec((1, tm, fox_w), lambda b, t: (b, t, 0)),
            pl.BlockSpec((1, fox_w, tm), lambda b, t: (b, 0, t)),
            pl.BlockSpec((1, fox_w, tm), lambda b, t: (b, 0, t)),
            pl.BlockSpec((1, H, tm), lambda b, t: (b, 0, t)),
            pl.BlockSpec((1, H, tm), lambda b, t: (b, 0, t)),
            pl.BlockSpec((1, tm, gla_vw), lambda b, t: (b, t, 0)),
            pl.BlockSpec((1, Hg, dk, dv), lambda b, t: (b, 0, 0, 0)),
        ],
        out_shape=[
            jax.ShapeDtypeStruct((B, S, fox_w), BF16),
            jax.ShapeDtypeStruct((B, fox_w, S), F32),
            jax.ShapeDtypeStruct((B, fox_w, S), F32),
            jax.ShapeDtypeStruct((B, H, S), F32),
            jax.ShapeDtypeStruct((B, H, S), F32),
            jax.ShapeDtypeStruct((B, S, gla_vw), BF16),
            jax.ShapeDtypeStruct((B, Hg, dk, dv), F32),
        ],
        scratch_shapes=[
            pltpu.VMEM((H, LANES), F32),
            pltpu.VMEM((gla_vw, gla_kw), F32),
            pltpu.VMEM((tm, tm), BF16),
            pltpu.VMEM((GLA_CHUNK, GLA_CHUNK), BF16),
            pltpu.VMEM((tm, gla_kw), F32), pltpu.VMEM((tm, gla_kw), F32),
            pltpu.VMEM((tm, gla_vw), F32), pltpu.VMEM((tm, gla_vw), F32), pltpu.VMEM((tm, gla_kw), F32),
        ],
        compiler_params=cparams(("arbitrary", "arbitrary")),
        name="proj_gla_prompt",
    )(x_prompt, g1, w_tok, w_dm, fox_b_f.reshape(H, 1), w_gate, bgate, gnorm)

    t_blk = ATT_T
    nq = S // t_blk
    n_pairs = fox_w // LANES
    c4 = cT_p.reshape(B, n_pairs, 2, S)
    fo_p = pl.pallas_call(
        functools.partial(_attn_prompt_kernel, t_blk=t_blk, dh=dh),
        grid=(B, n_pairs, nq),
        in_specs=[
            pl.BlockSpec((1, t_blk, LANES), lambda b, p, i: (b, i, p)),
            pl.BlockSpec((1, LANES, S), lambda b, p, i: (b, p, 0)),
            pl.BlockSpec((1, LANES, S), lambda b, p, i: (b, p, 0)),
            pl.BlockSpec((1, 1, 2, S), lambda b, p, i: (b, p, 0, 0)),
        ],
        out_specs=pl.BlockSpec((1, t_blk, LANES), lambda b, p, i: (b, i, p)),
        out_shape=jax.ShapeDtypeStruct((B, S, fox_w), BF16),
        scratch_shapes=[
            pltpu.VMEM((2, LANES, S), BF16),
            pltpu.VMEM((2 * LANES, S), BF16),
            pltpu.VMEM((2, t_blk, LANES), BF16),
            pltpu.VMEM((2, t_blk, 1), F32),
            pltpu.VMEM((2, t_blk, 2 * LANES), F32),
        ],
        compiler_params=cparams(("arbitrary", "arbitrary", "arbitrary")),
        name="fox_attn_prompt",
    )(q_p, kT_p, vT_p, c4)

    y_p = _ffn_call(x_prompt.reshape(B * S, D), fo_p.reshape(B * S, fox_w), go_p.reshape(B * S, gla_vw),
                    wo_bf, wup_bf, wdn_bf, g2, gf, FFN_TM).reshape(B, S, D)

    xs = x_sample.reshape(Bd, D)
    full = lambda shape: pl.BlockSpec(shape, lambda: (0,) * len(shape))
    s_shapes = [(Bd, fox_w), (Bd, fox_w), (Bd, fox_w), (Bd, H), (Bd, gla_kw), (Bd, gla_kw), (Bd, gla_vw),
                (Bd, gla_vw), (Bd, gla_kw)]
    q_s, k_s, v_s, lf_s, gq_s, gk_s, gv_s, gg_s, la_s = pl.pallas_call(
        functools.partial(_proj_sample_kernel, fox_w=fox_w, fox_dh=dh, gla_kw=gla_kw, gla_vw=gla_vw, n_heads_gla=Hg,
                          n_heads_fox=H),
        in_specs=[full((Bd, D)), full((1, D)), full((n_tok_cols, D)), full((n_dm_rows, D)), full((1, H)),
                  full((LANES, gla_kw)), full((1, gla_kw))],
        out_specs=[full(s) for s in s_shapes],
        out_shape=[jax.ShapeDtypeStruct(s, F32) for s in s_shapes],
        compiler_params=pltpu.CompilerParams(vmem_limit_bytes=VMEM_LIMIT_BYTES),
        name="proj_sample",
    )(xs, g1, w_tok, w_dm, fox_b_f.reshape(1, H), w_gate, bgate)

    kc = jnp.transpose(cache_k[0], (0, 2, 3, 1)).reshape(n_phys, fox_w, page)
    vc = jnp.transpose(cache_v[0], (0, 2, 3, 1)).reshape(n_phys, fox_w, page)
    lfc = jnp.transpose(cache_logf[0], (0, 2, 1))
    col = lambda a: a.reshape(Bd, a.shape[1], 1)
    grid_spec = pltpu.PrefetchScalarGridSpec(
        num_scalar_prefetch=1,
        grid=(Bd,),
        in_specs=[
            pl.BlockSpec((1, fox_w, 1), lambda b, pt: (b, 0, 0)),
            pl.BlockSpec((1, fox_w, 1), lambda b, pt: (b, 0, 0)),
            pl.BlockSpec((1, fox_w, 1), lambda b, pt: (b, 0, 0)),
            pl.BlockSpec((1, H, 1), lambda b, pt: (b, 0, 0)),
            pl.BlockSpec(memory_space=pl.ANY), pl.BlockSpec(memory_space=pl.ANY), pl.BlockSpec(memory_space=pl.ANY),
        ],
        out_specs=pl.BlockSpec((1, fox_w, 1), lambda b, pt: (b, 0, 0)),
        scratch_shapes=[
            pltpu.VMEM((DEC_NSLOT, DEC_G, fox_w, page), F32),
            pltpu.VMEM((2, n_pages, H, page), F32),
            pltpu.VMEM((n_pages, H, page), F32),
            pltpu.VMEM((H, n_pages * page), F32),
            pltpu.VMEM((fox_w, page), F32),
            pltpu.VMEM((page, page), BF16),
            pltpu.SemaphoreType.DMA((DEC_NSLOT,)),
            pltpu.SemaphoreType.DMA((2,)),
        ],
    )
    fo_s = pl.pallas_call(
        functools.partial(_decode_kernel, n_pages=n_pages, n_b=Bd, n_heads=H, dh=dh),
        grid_spec=grid_spec,
        out_shape=jax.ShapeDtypeStruct((Bd, fox_w, 1), F32),
        compiler_params=cparams(("arbitrary",)),
        name="fox_decode",
    )(page_table, col(q_s), col(k_s), col(v_s), col(lf_s), kc, vc, lfc)
    fo_s = fo_s.reshape(Bd, fox_w).astype(BF16)

    s_new, go_s = pl.pallas_call(
        functools.partial(_gla_sample_kernel, n_heads=Hg, dk=dk, dv=dv),
        grid=(Bd,),
        in_specs=[
            pl.BlockSpec((1, Hg, dk, dv), lambda b: (b, 0, 0, 0)),
            pl.BlockSpec((1, gla_kw, 1), lambda b: (b, 0, 0)),
            pl.BlockSpec((1, gla_kw, 1), lambda b: (b, 0, 0)),
            pl.BlockSpec((1, gla_kw, 1), lambda b: (b, 0, 0)),
            pl.BlockSpec((1, 1, gla_vw), lambda b: (b, 0, 0)),
            pl.BlockSpec((1, 1, gla_vw), lambda b: (b, 0, 0)),
            pl.BlockSpec((1, dv), lambda b: (0, 0)),
        ],
        out_specs=[
            pl.BlockSpec((1, Hg, dk, dv), lambda b: (b, 0, 0, 0)),
            pl.BlockSpec((1, 1, gla_vw), lambda b: (b, 0, 0)),
        ],
        out_shape=[jax.ShapeDtypeStruct((Bd, Hg, dk, dv), F32), jax.ShapeDtypeStruct((Bd, 1, gla_vw), F32)],
        compiler_params=cparams(("arbitrary",)),
        name="gla_sample",
    )(state_gla[0], col(la_s), col(gk_s), col(gq_s), gv_s.reshape(Bd, 1, gla_vw), gg_s.reshape(Bd, 1, gla_vw), gnorm)
    go_s = go_s.reshape(Bd, gla_vw).astype(BF16)

    y_s = _ffn_call(xs, fo_s, go_s, wo_bf, wup_bf, wdn_bf, g2, gf, Bd).reshape(Bd, 1, D)

    new_k_p = jnp.transpose(kT_p.reshape(1, B, H, dh, S), (0, 1, 4, 2, 3))
    new_v_p = jnp.transpose(vT_p.reshape(1, B, H, dh, S), (0, 1, 4, 2, 3))
    new_lf_p = jnp.transpose(lfT_p, (0, 2, 1)).reshape(1, B, S, H)
    return (y_p, y_s, new_k_p, new_v_p, new_lf_p, sfin_p.reshape(1, B, Hg, dk, dv),
            k_s.reshape(1, Bd, 1, H, dh), v_s.reshape(1, Bd, 1, H, dh), lf_s.reshape(1, Bd, 1, H),
            s_new.reshape(1, Bd, Hg, dk, dv))
```

```python
import functools

import jax
import jax.numpy as jnp
from jax import lax
from jax.experimental import pallas as pl
from jax.experimental.pallas import tpu as pltpu

F32 = jnp.float32
BF16 = jnp.bfloat16

LANES = 128
SUBLANES = 8
VMEM_LIMIT_BYTES = 56 * 1024 * 1024

EPS = 1e-6
LOG2E = 1.4426950408889634
N_AUG = 3
FF_LANE0 = 16
GLA_GATE_NORM = 16.0
GLA_CHUNK = 128
GLA_SUB = 32
PROJ_TM = 512
ATT_T = 512
ATT_KS = 256
FFN_TM = 512
FFN_CHUNK = 1024
DEC_G = 8
DEC_NSLOT = 3


def _dot(a, b):
    return jnp.dot(a, b, preferred_element_type=F32)


def _dot_nt(a, b):
    return lax.dot_general(a, b, (((1,), (1,)), ((), ())), preferred_element_type=F32)


def _split3(x):
    hi = x.astype(BF16).astype(F32)
    r = x - hi
    mid = r.astype(BF16).astype(F32)
    lo = r - mid
    return hi, mid, lo


def _log_sigmoid(x):
    return jnp.minimum(x, 0.0) - jnp.log1p(jnp.exp(-jnp.abs(x)))


def _silu(x):
    return x / (1.0 + jnp.exp(-x))


def _rms(x, g):
    return x * lax.rsqrt(jnp.mean(x * x, axis=-1, keepdims=True) + EPS) * g


def _proj_prompt_kernel(x_ref, g1_ref, wtok_ref, wdm_ref, bf_ref, bfrow_ref, wgate_ref, bgate_ref, gnorm_ref,
                        qT_out, kT_out, vT_out, lfT_out, cT_out, kaug_out, go_out, sfin_out,
                        carry_s, bdt_s, uincl_s, ltm_s, lincl_s, gq_s, gk_s, gv_s, gg_s, la_s,
                        *, tm, fox_w, fox_dh, gla_kw, gla_vw, n_heads_gla):
    t = pl.program_id(1)
    nt = pl.num_programs(1)
    dk = gla_kw // n_heads_gla
    dv = gla_vw // n_heads_gla

    @pl.when(jnp.logical_and(pl.program_id(0) == 0, t == 0))
    def _():
        r = lax.broadcasted_iota(jnp.int32, (tm, tm), 0)
        c = lax.broadcasted_iota(jnp.int32, (tm, tm), 1)
        uincl_s[...] = jnp.where(r <= c, 1.0, 0.0).astype(BF16)
        ltm_s[...] = jnp.where(c <= r, 1.0, 0.0).astype(BF16)
        r = lax.broadcasted_iota(jnp.int32, (GLA_CHUNK, GLA_CHUNK), 0)
        c = lax.broadcasted_iota(jnp.int32, (GLA_CHUNK, GLA_CHUNK), 1)
        lincl_s[...] = jnp.where(c <= r, 1.0, 0.0).astype(BF16)

    @pl.when(t == 0)
    def _():
        carry_s[...] = jnp.zeros_like(carry_s)
        bdt_s[...] = jnp.zeros_like(bdt_s)

    x = x_ref[0]
    xn = _rms(x, g1_ref[...]).astype(BF16)

    z = _dot_nt(xn, wtok_ref[...])
    kz = z[:, 0:fox_w]
    o0 = fox_w
    gq_s[...] = z[:, o0:o0 + gla_kw] * (dk ** -0.5)
    o0 += gla_kw
    gk_s[...] = z[:, o0:o0 + gla_kw]
    o0 += gla_kw
    gv_s[...] = z[:, o0:o0 + gla_vw]
    o0 += gla_vw
    gg_s[...] = z[:, o0:o0 + gla_vw]
    o0 += gla_vw
    misc = z[:, o0:o0 + LANES]
    pre = _dot(misc.astype(BF16), wgate_ref[...]) + bgate_ref[...]
    la_s[...] = _log_sigmoid(pre) * (1.0 / GLA_GATE_NORM)

    lf_tok = _log_sigmoid(misc + bfrow_ref[...])
    st3 = jnp.concatenate(_split3(lf_tok), axis=1).astype(BF16)
    cc = _dot(ltm_s[...], st3)
    cs_tok = cc[:, 0:LANES] + cc[:, LANES:2 * LANES] + cc[:, 2 * LANES:3 * LANES]
    d_tok = (cs_tok - cs_tok[0:1, :]) * LOG2E
    lane_k = lax.broadcasted_iota(jnp.int32, (tm, LANES), 1)
    for h in range(fox_w // fox_dh):
        own = (h % 2) * fox_dh
        spare = (1 - h % 2) * fox_dh
        parts = _split3(jnp.broadcast_to(d_tok[:, FF_LANE0 + h:FF_LANE0 + h + 1], (tm, LANES)))
        aug = jnp.zeros((tm, LANES), F32)
        for n, part in enumerate(parts):
            aug = jnp.where(lane_k == spare + n, -part, aug)
        own_l = jnp.logical_and(lane_k >= own, lane_k < own + fox_dh)
        kaug_out[0, h] = jnp.where(own_l, kz[:, (h // 2) * LANES:(h // 2 + 1) * LANES], aug).astype(BF16)

    zt = _dot_nt(wdm_ref[...], xn)
    qT_out[0] = (zt[0:fox_w] * (fox_dh ** -0.5 * LOG2E)).astype(BF16)
    kT_out[0] = zt[fox_w:2 * fox_w]
    vT_out[0] = zt[2 * fox_w:3 * fox_w]
    lf = _log_sigmoid(zt[3 * fox_w:3 * fox_w + SUBLANES] + bf_ref[...])
    lfT_out[0] = lf
    hi, mid, lo = _split3(lf)
    stack = jnp.concatenate([hi, mid, lo, jnp.zeros_like(hi)], axis=0).astype(BF16)
    cs = _dot(stack, uincl_s[...])
    cs = cs[0:8] + cs[8:16] + cs[16:24]
    carry = carry_s[...]
    cT_out[0] = cs + carry[:, 0:1]
    tot = _dot(stack, jnp.ones((tm, LANES), BF16))
    carry_s[...] = carry + tot[0:8] + tot[8:16] + tot[16:24]

    nsub = GLA_CHUNK // GLA_SUB
    rowi = lax.broadcasted_iota(jnp.int32, (GLA_CHUNK, gla_kw), 0)
    lanei = lax.broadcasted_iota(jnp.int32, (GLA_CHUNK, gla_kw), 1)
    ar = lax.broadcasted_iota(jnp.int32, (GLA_CHUNK, GLA_CHUNK), 0)
    ac = lax.broadcasted_iota(jnp.int32, (GLA_CHUNK, GLA_CHUNK), 1)
    tri_blk = jnp.logical_and(ar // GLA_SUB == ac // GLA_SUB, ar >= ac)
    br = lax.broadcasted_iota(jnp.int32, (gla_vw, gla_kw), 0)
    bc = lax.broadcasted_iota(jnp.int32, (gla_vw, gla_kw), 1)
    bd_mask = (br // dv) == (bc // dk)

    def chunk_body(ci, _):
        r0 = pl.multiple_of(ci * GLA_CHUNK, GLA_CHUNK)
        la_c = la_s[pl.ds(r0, GLA_CHUNK), :]
        gq_c = gq_s[pl.ds(r0, GLA_CHUNK), :]
        gk_c = gk_s[pl.ds(r0, GLA_CHUNK), :]
        gv_c = gv_s[pl.ds(r0, GLA_CHUNK), :]
        gg_c = gg_s[pl.ds(r0, GLA_CHUNK), :]
        h3, m3, l3 = _split3(la_c)
        st = jnp.concatenate([h3, m3, l3], axis=1).astype(BF16)
        bb = _dot(lincl_s[...], st)
        b = bb[:, 0:gla_kw] + bb[:, gla_kw:2 * gla_kw] + bb[:, 2 * gla_kw:3 * gla_kw]
        bmid_l, bend_l, b0_l = [], [], []
        for i in range(nsub):
            s0 = i * GLA_SUB
            bmid_l.append(b[s0 + GLA_SUB // 2:s0 + GLA_SUB // 2 + 1])
            bend_l.append(b[s0 + GLA_SUB - 1:s0 + GLA_SUB])
            b0_l.append(jnp.zeros((1, gla_kw), F32) if i == 0 else b[s0 - 1:s0])
        bc_rows = lambda rows: jnp.concatenate(
            [jnp.broadcast_to(r, (GLA_SUB, gla_kw)) for r in rows], axis=0)
        bmid, bend, b0 = bc_rows(bmid_l), bc_rows(bend_l), bc_rows(b0_l)
        qt = (gq_c * jnp.exp(b - bmid)).astype(BF16)
        kt = (gk_c * jnp.exp(bmid - b)).astype(BF16)
        qp = gq_c * jnp.exp(b - b0)
        kd = gk_c * jnp.exp(bend - b)
        gv_bf = gv_c.astype(BF16)
        gvT_bf = gv_c.T.astype(BF16)

        o_inter = jnp.zeros((GLA_CHUNK, gla_vw), F32)
        for i in range(nsub):
            rm = jnp.logical_and(rowi >= i * GLA_SUB, rowi < (i + 1) * GLA_SUB)
            bdt = bdt_s[...]
            o_inter = o_inter + _dot_nt(jnp.where(rm, qp, 0.0).astype(BF16), bdt.astype(BF16))
            ut = _dot(gvT_bf, jnp.where(rm, kd, 0.0).astype(BF16))
            decay = jnp.exp(bend_l[i] - b0_l[i])
            bdt_s[...] = bdt * decay + jnp.where(bd_mask, ut, 0.0)

        for h in range(n_heads_gla):
            hm = jnp.logical_and(lanei >= h * dk, lanei < (h + 1) * dk)
            a = _dot_nt(jnp.where(hm, qt, jnp.zeros_like(qt)), kt)
            a = jnp.where(tri_blk, a, 0.0).astype(BF16)
            o_h = _dot(a, gv_bf[:, h * dv:(h + 1) * dv]) + o_inter[:, h * dv:(h + 1) * dv]
            o_n = _rms(o_h, gnorm_ref[...])
            go = o_n * _silu(gg_c[:, h * dv:(h + 1) * dv])
            go_out[0, pl.ds(r0, GLA_CHUNK), h * dv:(h + 1) * dv] = go.astype(BF16)
        return 0

    lax.fori_loop(0, tm // GLA_CHUNK, chunk_body, 0)

    @pl.when(t == nt - 1)
    def _():
        bd = bdt_s[...].T
        for h in range(n_heads_gla):
            sfin_out[0, h] = bd[h * dk:(h + 1) * dk, h * dv:(h + 1) * dv]


def _attn_prompt_kernel(qT_ref, kaug_ref, vT_ref, c_ref, oT_ref, vaug_s, qa_s, m_s, acc_s, *, t_blk, dh):
    i = pl.program_id(2)
    hw2 = 2 * dh

    @pl.when(i == 0)
    def _():
        for h in range(2):
            vaug_s[h, 0:dh, :] = vT_ref[0, h * dh:(h + 1) * dh, :].astype(BF16)
            vaug_s[h, dh:, :] = jnp.ones((vaug_s.shape[1] - dh, vaug_s.shape[2]), BF16)

    qT = qT_ref[0]
    rowq = lax.broadcasted_iota(jnp.int32, (hw2, t_blk), 0)
    keyi = lax.broadcasted_iota(jnp.int32, (ATT_KS, t_blk), 0)
    qryi = lax.broadcasted_iota(jnp.int32, (ATT_KS, t_blk), 1)
    t0 = pl.multiple_of(i * t_blk, t_blk)
    c_q = []
    for h in range(2):
        spare = (1 - h) * dh
        own_r = jnp.logical_and(rowq >= h * dh, rowq < (h + 1) * dh)
        ones_r = jnp.logical_and(rowq >= spare, rowq < spare + N_AUG)
        qa_s[h] = jnp.where(own_r, qT, jnp.where(ones_r, 1.0, 0.0).astype(BF16))
        m_s[h] = jnp.full(m_s.shape[1:], -jnp.inf, F32)
        acc_s[h] = jnp.zeros(acc_s.shape[1:], F32)
        c_q.append(c_ref[0, 0, h:h + 1, pl.ds(t0, LANES)][:, 0:1])

    def step(j, masked):
        k0 = pl.multiple_of(j * t_blk, t_blk)
        n_sub = t_blk // ATT_KS
        kks = [pl.multiple_of(k0 + ks * ATT_KS, ATT_KS) for ks in range(n_sub)]
        s_all = [[_dot(kaug_ref[0, 0, h, pl.ds(kks[ks], ATT_KS), :], qa_s[h]) for ks in range(n_sub)]
                 for h in range(2)]
        for h in range(2):
            off = (c_ref[0, 0, h:h + 1, pl.ds(k0, LANES)][:, 0:1] - c_q[h]) * LOG2E
            m_run = m_s[h]
            acc = acc_s[h]
            for ks in range(n_sub):
                s = s_all[h][ks]
                if masked:
                    s = jnp.where(keyi + ks * ATT_KS <= qryi, s, -jnp.inf)
                m_new = jnp.maximum(m_run, jnp.max(s, axis=0, keepdims=True) - off)
                p = jnp.exp2(s - (m_new + off))
                alpha = jnp.exp2(m_run - m_new)
                pv = _dot(vaug_s[h, :, pl.ds(kks[ks], ATT_KS)], p.astype(BF16))
                acc = alpha * acc + pv
                m_run = m_new
            acc_s[h] = acc
            m_s[h] = m_run

    def body(j, carry):
        step(j, False)
        return carry

    lax.fori_loop(0, i, body, 0)
    step(i, True)
    outs = [acc_s[h][0:dh, :] / acc_s[h][dh:dh + 1, :] for h in range(2)]
    oT_ref[0] = jnp.concatenate(outs, axis=0).astype(BF16)


def _ffn_kernel(x_ref, foT_ref, go_ref, wo_ref, wup_ref, wdn_ref, g2_ref, gf_ref, y_ref, u_s, *, fox_w, d_ff):
    x = x_ref[0]
    h = x + (_dot(foT_ref[0].T, wo_ref[0:fox_w, :]) + _dot(go_ref[0], wo_ref[fox_w:, :]))
    hn = _rms(h, g2_ref[...]).astype(BF16)
    for c in range(d_ff // FFN_CHUNK):
        u = _dot(hn, wup_ref[:, c * FFN_CHUNK:(c + 1) * FFN_CHUNK])
        u_s[:, c * FFN_CHUNK:(c + 1) * FFN_CHUNK] = jnp.square(jnp.maximum(u, 0.0)).astype(BF16)
    y_ref[0] = _rms(h + _dot(u_s[...], wdn_ref[...]), gf_ref[...])


def _ffn_call(x3, foT, go, wo, wup, wdn, g2, gf, tm):
    nb, n, d = x3.shape
    fox_w = foT.shape[1]
    d_ff = wup.shape[1]
    const = lambda shape: pl.BlockSpec(shape, lambda b, i: (0, 0), pipeline_mode=pl.Buffered(1))
    return pl.pallas_call(
        functools.partial(_ffn_kernel, fox_w=fox_w, d_ff=d_ff),
        grid=(nb, n // tm),
        in_specs=[
            pl.BlockSpec((1, tm, d), lambda b, i: (b, i, 0)),
            pl.BlockSpec((1, fox_w, tm), lambda b, i: (b, 0, i)),
            pl.BlockSpec((1, tm, go.shape[2]), lambda b, i: (b, i, 0)),
            const(wo.shape), const(wup.shape), const(wdn.shape), const(g2.shape), const(gf.shape),
        ],
        out_specs=pl.BlockSpec((1, tm, d), lambda b, i: (b, i, 0)),
        out_shape=jax.ShapeDtypeStruct((nb, n, d), F32),
        scratch_shapes=[pltpu.VMEM((tm, d_ff), BF16)],
        compiler_params=pltpu.CompilerParams(dimension_semantics=("arbitrary", "arbitrary"),
                                             vmem_limit_bytes=VMEM_LIMIT_BYTES),
        name="merge_ffn",
    )(x3, foT, go, wo, wup, wdn, g2, gf)


def _proj_sample_kernel(x_ref, g1_ref, wtok_ref, wdm_ref, bf_ref, wgate_ref, bgate_ref,
                        q_out, k_out, v_out, lf_out, gq_out, gk_out, gv_out, gg_out, la_out,
                        *, fox_w, fox_dh, gla_kw, gla_vw, n_heads_gla, n_heads_fox):
    dk = gla_kw // n_heads_gla
    xn = _rms(x_ref[...], g1_ref[...]).astype(BF16)
    z = _dot_nt(xn, wtok_ref[...])
    o0 = fox_w
    gq_out[...] = z[:, o0:o0 + gla_kw] * (dk ** -0.5)
    o0 += gla_kw
    gk_out[...] = z[:, o0:o0 + gla_kw]
    o0 += gla_kw
    gv_out[...] = z[:, o0:o0 + gla_vw]
    o0 += gla_vw
    gg_out[...] = z[:, o0:o0 + gla_vw]
    o0 += gla_vw
    glr = z[:, o0:o0 + LANES].astype(BF16)
    la_out[...] = _log_sigmoid(_dot(glr, wgate_ref[...]) + bgate_ref[...]) * (1.0 / GLA_GATE_NORM)
    z2 = _dot_nt(xn, wdm_ref[...])
    q_out[...] = z2[:, 0:fox_w] * (fox_dh ** -0.5)
    k_out[...] = z2[:, fox_w:2 * fox_w]
    v_out[...] = z2[:, 2 * fox_w:3 * fox_w]
    lf_out[...] = _log_sigmoid(z2[:, 3 * fox_w:3 * fox_w + n_heads_fox] + bf_ref[...])


def _decode_kernel(pt_ref, q_ref, knew_ref, vnew_ref, lfnew_ref, kc_hbm, vc_hbm, lfc_hbm, o_ref,
                   ring, lfbuf, rev_s, zbuf, acc_s, ustrict_s, sem_ring, sem_lf,
                   *, n_pages, n_b, n_heads, dh):
    b = pl.program_id(0)
    page = LANES
    hw = n_heads * dh
    nch = n_pages // DEC_G
    per_b = 2 * nch
    total = n_b * per_b

    def start_chunk(g):
        bg = g // per_b
        c = g - bg * per_b
        slot = lax.rem(g, DEC_NSLOT)

        @pl.when(c < nch)
        def _():
            for j in range(DEC_G):
                p = n_pages - 1 - (c * DEC_G + j)
                pltpu.make_async_copy(kc_hbm.at[pt_ref[bg, p]], ring.at[slot, j], sem_ring.at[slot]).start()

        @pl.when(c >= nch)
        def _():
            for j in range(DEC_G):
                p = (c - nch) * DEC_G + j
                pltpu.make_async_copy(vc_hbm.at[pt_ref[bg, p]], ring.at[slot, j], sem_ring.at[slot]).start()

    def wait_chunk(g):
        slot = lax.rem(g, DEC_NSLOT)
        for j in range(DEC_G):
            pltpu.make_async_copy(kc_hbm.at[0], ring.at[slot, j], sem_ring.at[slot]).wait()

    def start_lf(bb):
        sl = lax.rem(bb, 2)

        def body(p, _):
            pltpu.make_async_copy(lfc_hbm.at[pt_ref[bb, p]], lfbuf.at[sl, p], sem_lf.at[sl]).start()
            return 0

        lax.fori_loop(0, n_pages, body, 0)

    def wait_lf(bb):
        sl = lax.rem(bb, 2)

        def body(p, _):
            pltpu.make_async_copy(lfc_hbm.at[0], lfbuf.at[sl, p], sem_lf.at[sl]).wait()
            return 0

        lax.fori_loop(0, n_pages, body, 0)

    g0 = b * per_b

    @pl.when(b == 0)
    def _():
        r = lax.broadcasted_iota(jnp.int32, (page, page), 0)
        c = lax.broadcasted_iota(jnp.int32, (page, page), 1)
        ustrict_s[...] = jnp.where(r > c, 1.0, 0.0).astype(BF16)
        start_lf(b)
        for g in range(DEC_NSLOT - 1):
            start_chunk(g0 + g)

    wait_lf(b)

    @pl.when(b + 1 < n_b)
    def _():
        start_lf(b + 1)

    sl = lax.rem(b, 2)
    lf2d = lfbuf[sl].reshape(n_pages * n_heads, page)
    hi, mid, lo = _split3(lf2d)
    u = ustrict_s[...]
    rev = _dot(hi.astype(BF16), u) + _dot(mid.astype(BF16), u) + _dot(lo.astype(BF16), u)
    rev_s[...] = rev.reshape(n_pages, n_heads, page)

    qb = jnp.broadcast_to(q_ref[0], (hw, page))

    def head_sum(x):
        return jnp.sum(x.reshape(n_heads, dh, page), axis=1)

    def head_bcast(x):
        return jnp.broadcast_to(x[:, None, :], (n_heads, dh, page)).reshape(hw, page)

    def k_body(c, carry):
        g = g0 + c

        @pl.when(g + (DEC_NSLOT - 1) < total)
        def _():
            start_chunk(g + (DEC_NSLOT - 1))

        wait_chunk(g)
        slot = lax.rem(g, DEC_NSLOT)
        for j in range(DEC_G):
            p = n_pages - 1 - (c * DEC_G + j)
            s = head_sum(ring[slot, j] * qb)
            revp = rev_s[p]
            zbuf[:, pl.ds(pl.multiple_of(p * page, page), page)] = s + revp + carry
            carry = carry + jnp.broadcast_to(revp[:, 0:1] + lfbuf[sl, p][:, 0:1], (n_heads, page))
        return carry

    carry0 = jnp.broadcast_to(lfnew_ref[0], (n_heads, page))
    lax.fori_loop(0, nch, k_body, carry0)

    z_all = zbuf[...]
    z_new = head_sum(jnp.broadcast_to(knew_ref[0], (hw, page)) * qb)
    m = jnp.maximum(jnp.max(z_all, axis=1, keepdims=True), z_new[:, 0:1])
    p_all = jnp.exp(z_all - m)
    zbuf[...] = p_all
    p_new = jnp.exp(z_new - m)
    l = jnp.sum(p_all, axis=1, keepdims=True) + p_new

    acc_s[...] = jnp.zeros_like(acc_s)

    def v_body(c, carry):
        g = g0 + nch + c

        @pl.when(g + (DEC_NSLOT - 1) < total)
        def _():
            start_chunk(g + (DEC_NSLOT - 1))

        wait_chunk(g)
        slot = lax.rem(g, DEC_NSLOT)
        for j in range(DEC_G):
            p = c * DEC_G + j
            pp = zbuf[:, pl.ds(pl.multiple_of(p * page, page), page)]
            acc_s[...] += ring[slot, j] * head_bcast(pp)
        return carry

    lax.fori_loop(0, nch, v_body, 0)

    num = jnp.sum(acc_s[...], axis=1, keepdims=True) + head_bcast(p_new) * jnp.broadcast_to(vnew_ref[0], (hw, page))
    o_ref[0] = (num / head_bcast(l))[:, 0:1]


def _gla_sample_kernel(s_ref, la_ref, k_ref, q_ref, v_ref, gg_ref, gnorm_ref, s_out, go_out, *, n_heads, dk, dv):
    for h in range(n_heads):
        la = la_ref[0, h * dk:(h + 1) * dk, :]
        kk = k_ref[0, h * dk:(h + 1) * dk, :]
        qq = q_ref[0, h * dk:(h + 1) * dk, :]
        vv = v_ref[0, :, h * dv:(h + 1) * dv]
        s_new = s_ref[0, h] * jnp.exp(la) + kk * vv
        s_out[0, h] = s_new
        o = jnp.sum(qq * s_new, axis=0, keepdims=True)
        o_n = _rms(o, gnorm_ref[...])
        go_out[0, :, h * dv:(h + 1) * dv] = o_n * _silu(gg_ref[0, :, h * dv:(h + 1) * dv])


def kernel(x_prompt, x_sample, cache_k, cache_v, cache_logf, state_gla, page_table, norm1_g, w_in, fox_b_f,
           gla_w_gate_up, gla_b_gate, gla_norm_g, w_o, norm2_g, w_up, w_down, final_g):
    B, S, D = x_prompt.shape
    Bd = x_sample.shape[0]
    depth, n_phys, page, H, dh = cache_k.shape
    _, _, Hg, dk, dv = state_gla.shape
    assert depth == 1 and x_sample.shape[1] == 1 and page == LANES
    fox_w = H * dh
    gla_kw = Hg * dk
    gla_vw = Hg * dv
    rank = gla_w_gate_up.shape[1]
    n_pages = page_table.shape[1]

    wt = jnp.transpose(w_in[0])
    o_fq, o_fk = 0, fox_w
    o_ff = 3 * fox_w
    o_gq = o_ff + H
    o_gk = o_gq + gla_kw
    o_gv = o_gk + gla_kw
    o_glr = o_gv + gla_vw
    o_gg = o_glr + rank
    misc_pad = LANES - FF_LANE0 - H
    w_tok = jnp.concatenate([
        wt[o_fk:o_fk + fox_w], wt[o_gq:o_gq + gla_kw], wt[o_gk:o_gk + gla_kw], wt[o_gv:o_gv + gla_vw],
        wt[o_gg:o_gg + gla_vw], wt[o_glr:o_glr + rank], jnp.zeros((FF_LANE0 - rank, D), F32),
        wt[o_ff:o_ff + H], jnp.zeros((misc_pad, D), F32)], axis=0).astype(BF16)
    w_dm = jnp.concatenate([
        wt[o_fq:o_fq + 3 * fox_w], wt[o_ff:o_ff + H], jnp.zeros((2 * SUBLANES - H, D), F32)], axis=0).astype(BF16)
    bf_row = jnp.concatenate([jnp.zeros((1, FF_LANE0), F32), fox_b_f.reshape(1, H), jnp.zeros((1, misc_pad), F32)],
                             axis=1)
    w_gate = jnp.concatenate([gla_w_gate_up[0], jnp.zeros((LANES - rank, gla_kw), F32)], axis=0).astype(BF16)
    wo_bf = w_o[0].astype(BF16)
    wup_bf = w_up[0].astype(BF16)
    wdn_bf = w_down[0].astype(BF16)
    g1 = norm1_g.reshape(1, D)
    g2 = norm2_g.reshape(1, D)
    gf = final_g.reshape(1, D)
    bgate = gla_b_gate.reshape(1, gla_kw)
    gnorm = gla_norm_g.reshape(1, dv)

    cparams = lambda sem: pltpu.CompilerParams(dimension_semantics=sem, vmem_limit_bytes=VMEM_LIMIT_BYTES)
    single = lambda shape: pl.BlockSpec(shape, lambda *_: (0,) * len(shape), pipeline_mode=pl.Buffered(1))

    tm = PROJ_TM
    nt = S // tm
    n_tok_cols = w_tok.shape[0]
    n_dm_rows = w_dm.shape[0]
    qT_p, kT_p, vT_p, lfT_p, cT_p, kaug_p, go_p, sfin_p = pl.pallas_call(
        functools.partial(_proj_prompt_kernel, tm=tm, fox_w=fox_w, fox_dh=dh, gla_kw=gla_kw, gla_vw=gla_vw,
                          n_heads_gla=Hg),
        grid=(B, nt),
        in_specs=[
            pl.BlockSpec((1, tm, D), lambda b, t: (b, t, 0)),
            single((1, D)), single((n_tok_cols, D)), single((n_dm_rows, D)), single((H, 1)), single((1, LANES)),
            single((LANES, gla_kw)), single((1, gla_kw)), single((1, dv)),
        ],
        out_specs=[
            pl.BlockSpec((1, fox_w, tm), lambda b, t: (b, 0, t)),
            pl.BlockSpec((1, fox_w, tm), lambda b, t: (b, 0, t)),
            pl.BlockSpec((1, fox_w, tm), lambda b, t: (b, 0, t)),
            pl.BlockSpec((1, H, tm), lambda b, t: (b, 0, t)),
            pl.BlockSpec((1, H, tm), lambda b, t: (b, 0, t)),
            pl.BlockSpec((1, H, tm, LANES), lambda b, t: (b, 0, t, 0)),
            pl.BlockSpec((1, tm, gla_vw), lambda b, t: (b, t, 0)),
            pl.BlockSpec((1, Hg, dk, dv), lambda b, t: (b, 0, 0, 0)),
        ],
        out_shape=[
            jax.ShapeDtypeStruct((B, fox_w, S), BF16),
            jax.ShapeDtypeStruct((B, fox_w, S), F32),
            jax.ShapeDtypeStruct((B, fox_w, S), F32),
            jax.ShapeDtypeStruct((B, H, S), F32),
            jax.ShapeDtypeStruct((B, H, S), F32),
            jax.ShapeDtypeStruct((B, H, S, LANES), BF16),
            jax.ShapeDtypeStruct((B, S, gla_vw), BF16),
            jax.ShapeDtypeStruct((B, Hg, dk, dv), F32),
        ],
        scratch_shapes=[
            pltpu.VMEM((H, LANES), F32),
            pltpu.VMEM((gla_vw, gla_kw), F32),
            pltpu.VMEM((tm, tm), BF16),
            pltpu.VMEM((tm, tm), BF16),
            pltpu.VMEM((GLA_CHUNK, GLA_CHUNK), BF16),
            pltpu.VMEM((tm, gla_kw), F32), pltpu.VMEM((tm, gla_kw), F32),
            pltpu.VMEM((tm, gla_vw), F32), pltpu.VMEM((tm, gla_vw), F32), pltpu.VMEM((tm, gla_kw), F32),
        ],
        compiler_params=cparams(("arbitrary", "arbitrary")),
        name="proj_gla_prompt",
    )(x_prompt, g1, w_tok, w_dm, fox_b_f.reshape(H, 1), bf_row, w_gate, bgate, gnorm)

    t_blk = ATT_T
    assert t_blk == tm
    nq = S // t_blk
    n_pairs = fox_w // LANES
    c4 = cT_p.reshape(B, n_pairs, 2, S)
    k5 = kaug_p.reshape(B, n_pairs, 2, S, LANES)
    n_vrows = dh + 2 * SUBLANES
    foT_p = pl.pallas_call(
        functools.partial(_attn_prompt_kernel, t_blk=t_blk, dh=dh),
        grid=(B, n_pairs, nq),
        in_specs=[
            pl.BlockSpec((1, LANES, t_blk), lambda b, p, i: (b, p, i)),
            pl.BlockSpec((1, 1, 2, S, LANES), lambda b, p, i: (b, p, 0, 0, 0)),
            pl.BlockSpec((1, LANES, S), lambda b, p, i: (b, p, 0)),
            pl.BlockSpec((1, 1, 2, S), lambda b, p, i: (b, p, 0, 0)),
        ],
        out_specs=pl.BlockSpec((1, LANES, t_blk), lambda b, p, i: (b, p, i)),
        out_shape=jax.ShapeDtypeStruct((B, fox_w, S), BF16),
        scratch_shapes=[
            pltpu.VMEM((2, n_vrows, S), BF16),
            pltpu.VMEM((2, LANES, t_blk), BF16),
            pltpu.VMEM((2, 1, t_blk), F32),
            pltpu.VMEM((2, n_vrows, t_blk), F32),
        ],
        compiler_params=cparams(("arbitrary", "arbitrary", "arbitrary")),
        name="fox_attn_prompt",
    )(qT_p, k5, vT_p, c4)

    y_p = _ffn_call(x_prompt, foT_p, go_p, wo_bf, wup_bf, wdn_bf, g2, gf, FFN_TM)

    xs = x_sample.reshape(Bd, D)
    full = lambda shape: pl.BlockSpec(shape, lambda: (0,) * len(shape))
    s_shapes = [(Bd, fox_w), (Bd, fox_w), (Bd, fox_w), (Bd, H), (Bd, gla_kw), (Bd, gla_kw), (Bd, gla_vw),
                (Bd, gla_vw), (Bd, gla_kw)]
    q_s, k_s, v_s, lf_s, gq_s, gk_s, gv_s, gg_s, la_s = pl.pallas_call(
        functools.partial(_proj_sample_kernel, fox_w=fox_w, fox_dh=dh, gla_kw=gla_kw, gla_vw=gla_vw, n_heads_gla=Hg,
                          n_heads_fox=H),
        in_specs=[full((Bd, D)), full((1, D)), full((n_tok_cols, D)), full((n_dm_rows, D)), full((1, H)),
                  full((LANES, gla_kw)), full((1, gla_kw))],
        out_specs=[full(s) for s in s_shapes],
        out_shape=[jax.ShapeDtypeStruct(s, F32) for s in s_shapes],
        compiler_params=pltpu.CompilerParams(vmem_limit_bytes=VMEM_LIMIT_BYTES),
        name="proj_sample",
    )(xs, g1, w_tok, w_dm, fox_b_f.reshape(1, H), w_gate, bgate)

    kc = jnp.transpose(cache_k[0], (0, 2, 3, 1)).reshape(n_phys, fox_w, page)
    vc = jnp.transpose(cache_v[0], (0, 2, 3, 1)).reshape(n_phys, fox_w, page)
    lfc = jnp.transpose(cache_logf[0], (0, 2, 1))
    col = lambda a: a.reshape(Bd, a.shape[1], 1)
    grid_spec = pltpu.PrefetchScalarGridSpec(
        num_scalar_prefetch=1,
        grid=(Bd,),
        in_specs=[
            pl.BlockSpec((1, fox_w, 1), lambda b, pt: (b, 0, 0)),
            pl.BlockSpec((1, fox_w, 1), lambda b, pt: (b, 0, 0)),
            pl.BlockSpec((1, fox_w, 1), lambda b, pt: (b, 0, 0)),
            pl.BlockSpec((1, H, 1), lambda b, pt: (b, 0, 0)),
            pl.BlockSpec(memory_space=pl.ANY), pl.BlockSpec(memory_space=pl.ANY), pl.BlockSpec(memory_space=pl.ANY),
        ],
        out_specs=pl.BlockSpec((1, fox_w, 1), lambda b, pt: (b, 0, 0)),
        scratch_shapes=[
            pltpu.VMEM((DEC_NSLOT, DEC_G, fox_w, page), F32),
            pltpu.VMEM((2, n_pages, H, page), F32),
            pltpu.VMEM((n_pages, H, page), F32),
            pltpu.VMEM((H, n_pages * page), F32),
            pltpu.VMEM((fox_w, page), F32),
            pltpu.VMEM((page, page), BF16),
            pltpu.SemaphoreType.DMA((DEC_NSLOT,)),
            pltpu.SemaphoreType.DMA((2,)),
        ],
    )
    fo_s = pl.pallas_call(
        functools.partial(_decode_kernel, n_pages=n_pages, n_b=Bd, n_heads=H, dh=dh),
        grid_spec=grid_spec,
        out_shape=jax.ShapeDtypeStruct((Bd, fox_w, 1), F32),
        compiler_params=cparams(("arbitrary",)),
        name="fox_decode",
    )(page_table, col(q_s), col(k_s), col(v_s), col(lf_s), kc, vc, lfc)
    foT_s = fo_s.reshape(Bd, fox_w).T.astype(BF16)

    s_new, go_s = pl.pallas_call(
        functools.partial(_gla_sample_kernel, n_heads=Hg, dk=dk, dv=dv),
        grid=(Bd,),
        in_specs=[
            pl.BlockSpec((1, Hg, dk, dv), lambda b: (b, 0, 0, 0)),
            pl.BlockSpec((1, gla_kw, 1), lambda b: (b, 0, 0)),
            pl.BlockSpec((1, gla_kw, 1), lambda b: (b, 0, 0)),
            pl.BlockSpec((1, gla_kw, 1), lambda b: (b, 0, 0)),
            pl.BlockSpec((1, 1, gla_vw), lambda b: (b, 0, 0)),
            pl.BlockSpec((1, 1, gla_vw), lambda b: (b, 0, 0)),
            pl.BlockSpec((1, dv), lambda b: (0, 0)),
        ],
        out_specs=[
            pl.BlockSpec((1, Hg, dk, dv), lambda b: (b, 0, 0, 0)),
            pl.BlockSpec((1, 1, gla_vw), lambda b: (b, 0, 0)),
        ],
        out_shape=[jax.ShapeDtypeStruct((Bd, Hg, dk, dv), F32), jax.ShapeDtypeStruct((Bd, 1, gla_vw), F32)],
        compiler_params=cparams(("arbitrary",)),
        name="gla_sample",
    )(state_gla[0], col(la_s), col(gk_s), col(gq_s), gv_s.reshape(Bd, 1, gla_vw), gg_s.reshape(Bd, 1, gla_vw), gnorm)
    go_s = go_s.reshape(Bd, gla_vw).astype(BF16)

    y_s = _ffn_call(xs[None], foT_s[None], go_s[None], wo_bf, wup_bf, wdn_bf, g2, gf, Bd).reshape(Bd, 1, D)

    new_k_p = jnp.transpose(kT_p.reshape(1, B, H, dh, S), (0, 1, 4, 2, 3))
    new_v_p = jnp.transpose(vT_p.reshape(1, B, H, dh, S), (0, 1, 4, 2, 3))
    new_lf_p = jnp.transpose(lfT_p, (0, 2, 1)).reshape(1, B, S, H)
    return (y_p, y_s, new_k_p, new_v_p, new_lf_p, sfin_p.reshape(1, B, Hg, dk, dv),
            k_s.reshape(1, Bd, 1, H, dh), v_s.reshape(1, Bd, 1, H, dh), lf_s.reshape(1, Bd, 1, H),
            s_new.reshape(1, Bd, Hg, dk, dv))
```

```python
import functools

import jax
import jax.numpy as jnp
from jax import lax
from jax.experimental import pallas as pl
from jax.experimental.pallas import tpu as pltpu

F32 = jnp.float32
BF16 = jnp.bfloat16

LANES = 128
SUBLANES = 8
VMEM_LIMIT_BYTES = 56 * 1024 * 1024

EPS = 1e-6
LOG2E = 1.4426950408889634
N_AUG = 3
FF_LANE0 = 16
GLA_GATE_NORM = 16.0
GLA_CHUNK = 128
GLA_SUB = 32
PROJ_TM = 512
ATT_T = 512
ATT_KS = 256
FFN_TM = 512
FFN_CHUNK = 1024
DEC_G = 8
DEC_NSLOT = 5


def _dot(a, b):
    return jnp.dot(a, b, preferred_element_type=F32)


def _dot_nt(a, b):
    return lax.dot_general(a, b, (((1,), (1,)), ((), ())), preferred_element_type=F32)


def _split3(x):
    hi = x.astype(BF16).astype(F32)
    r = x - hi
    mid = r.astype(BF16).astype(F32)
    lo = r - mid
    return hi, mid, lo


def _log_sigmoid(x):
    return jnp.minimum(x, 0.0) - jnp.log1p(jnp.exp(-jnp.abs(x)))


def _silu(x):
    return x / (1.0 + jnp.exp(-x))


def _rms(x, g):
    return x * lax.rsqrt(jnp.mean(x * x, axis=-1, keepdims=True) + EPS) * g


def _proj_prompt_kernel(x_ref, g1_ref, wtok_ref, wdm_ref, bf_ref, bfrow_ref, wgate_ref, bgate_ref, gnorm_ref,
                        qT_out, kT_out, vT_out, lfT_out, cT_out, kaug_out, go_out, sfin_out,
                        carry_s, bdt_s, uincl_s, ltm_s, lincl_s, gq_s, gk_s, gv_s, gg_s, la_s,
                        *, tm, fox_w, fox_dh, gla_kw, gla_vw, n_heads_gla):
    t = pl.program_id(1)
    nt = pl.num_programs(1)
    dk = gla_kw // n_heads_gla
    dv = gla_vw // n_heads_gla

    @pl.when(jnp.logical_and(pl.program_id(0) == 0, t == 0))
    def _():
        r = lax.broadcasted_iota(jnp.int32, (tm, tm), 0)
        c = lax.broadcasted_iota(jnp.int32, (tm, tm), 1)
        uincl_s[...] = jnp.where(r <= c, 1.0, 0.0).astype(BF16)
        ltm_s[...] = jnp.where(c <= r, 1.0, 0.0).astype(BF16)
        r = lax.broadcasted_iota(jnp.int32, (GLA_CHUNK, GLA_CHUNK), 0)
        c = lax.broadcasted_iota(jnp.int32, (GLA_CHUNK, GLA_CHUNK), 1)
        lincl_s[...] = jnp.where(c <= r, 1.0, 0.0).astype(BF16)

    @pl.when(t == 0)
    def _():
        carry_s[...] = jnp.zeros_like(carry_s)
        bdt_s[...] = jnp.zeros_like(bdt_s)

    x = x_ref[0]
    xn = _rms(x, g1_ref[...]).astype(BF16)

    z = _dot_nt(xn, wtok_ref[...])
    kz = z[:, 0:fox_w]
    o0 = fox_w
    gq_s[...] = z[:, o0:o0 + gla_kw] * (dk ** -0.5)
    o0 += gla_kw
    gk_s[...] = z[:, o0:o0 + gla_kw]
    o0 += gla_kw
    gv_s[...] = z[:, o0:o0 + gla_vw]
    o0 += gla_vw
    gg_s[...] = z[:, o0:o0 + gla_vw]
    o0 += gla_vw
    misc = z[:, o0:o0 + LANES]
    pre = _dot(misc.astype(BF16), wgate_ref[...]) + bgate_ref[...]
    la_s[...] = _log_sigmoid(pre) * (1.0 / GLA_GATE_NORM)

    lf_tok = _log_sigmoid(misc + bfrow_ref[...])
    st3 = jnp.concatenate(_split3(lf_tok), axis=1).astype(BF16)
    cc = _dot(ltm_s[...], st3)
    cs_tok = cc[:, 0:LANES] + cc[:, LANES:2 * LANES] + cc[:, 2 * LANES:3 * LANES]
    d_tok = (cs_tok - cs_tok[0:1, :]) * LOG2E
    lane_k = lax.broadcasted_iota(jnp.int32, (tm, LANES), 1)
    for h in range(fox_w // fox_dh):
        own = (h % 2) * fox_dh
        spare = (1 - h % 2) * fox_dh
        parts = _split3(jnp.broadcast_to(d_tok[:, FF_LANE0 + h:FF_LANE0 + h + 1], (tm, LANES)))
        aug = jnp.zeros((tm, LANES), F32)
        for n, part in enumerate(parts):
            aug = jnp.where(lane_k == spare + n, -part, aug)
        own_l = jnp.logical_and(lane_k >= own, lane_k < own + fox_dh)
        kaug_out[0, h] = jnp.where(own_l, kz[:, (h // 2) * LANES:(h // 2 + 1) * LANES], aug).astype(BF16)

    zt = _dot_nt(wdm_ref[...], xn)
    qT_out[0] = (zt[0:fox_w] * (fox_dh ** -0.5 * LOG2E)).astype(BF16)
    kT_out[0] = zt[fox_w:2 * fox_w]
    vT_out[0] = zt[2 * fox_w:3 * fox_w]
    lf = _log_sigmoid(zt[3 * fox_w:3 * fox_w + SUBLANES] + bf_ref[...])
    lfT_out[0] = lf
    hi, mid, lo = _split3(lf)
    stack = jnp.concatenate([hi, mid, lo, jnp.zeros_like(hi)], axis=0).astype(BF16)
    cs = _dot(stack, uincl_s[...])
    cs = cs[0:8] + cs[8:16] + cs[16:24]
    carry = carry_s[...]
    cT_out[0] = cs + carry[:, 0:1]
    tot = _dot(stack, jnp.ones((tm, LANES), BF16))
    carry_s[...] = carry + tot[0:8] + tot[8:16] + tot[16:24]

    nsub = GLA_CHUNK // GLA_SUB
    rowi = lax.broadcasted_iota(jnp.int32, (GLA_CHUNK, gla_kw), 0)
    lanei = lax.broadcasted_iota(jnp.int32, (GLA_CHUNK, gla_kw), 1)
    ar = lax.broadcasted_iota(jnp.int32, (GLA_CHUNK, GLA_CHUNK), 0)
    ac = lax.broadcasted_iota(jnp.int32, (GLA_CHUNK, GLA_CHUNK), 1)
    tri_blk = jnp.logical_and(ar // GLA_SUB == ac // GLA_SUB, ar >= ac)
    br = lax.broadcasted_iota(jnp.int32, (gla_vw, gla_kw), 0)
    bc = lax.broadcasted_iota(jnp.int32, (gla_vw, gla_kw), 1)
    bd_mask = (br // dv) == (bc // dk)

    def chunk_body(ci, _):
        r0 = ci * GLA_CHUNK
        la_c = la_s[pl.ds(r0, GLA_CHUNK), :]
        gq_c = gq_s[pl.ds(r0, GLA_CHUNK), :]
        gk_c = gk_s[pl.ds(r0, GLA_CHUNK), :]
        gv_c = gv_s[pl.ds(r0, GLA_CHUNK), :]
        gg_c = gg_s[pl.ds(r0, GLA_CHUNK), :]
        h3, m3, l3 = _split3(la_c)
        st = jnp.concatenate([h3, m3, l3], axis=1).astype(BF16)
        bb = _dot(lincl_s[...], st)
        b = bb[:, 0:gla_kw] + bb[:, gla_kw:2 * gla_kw] + bb[:, 2 * gla_kw:3 * gla_kw]
        bmid_l, bend_l, b0_l = [], [], []
        for i in range(nsub):
            s0 = i * GLA_SUB
            bmid_l.append(b[s0 + GLA_SUB // 2:s0 + GLA_SUB // 2 + 1])
            bend_l.append(b[s0 + GLA_SUB - 1:s0 + GLA_SUB])
            b0_l.append(jnp.zeros((1, gla_kw), F32) if i == 0 else b[s0 - 1:s0])
        bc_rows = lambda rows: jnp.concatenate(
            [jnp.broadcast_to(r, (GLA_SUB, gla_kw)) for r in rows], axis=0)
        bmid, bend, b0 = bc_rows(bmid_l), bc_rows(bend_l), bc_rows(b0_l)
        qt = (gq_c * jnp.exp(b - bmid)).astype(BF16)
        kt = (gk_c * jnp.exp(bmid - b)).astype(BF16)
        qp = gq_c * jnp.exp(b - b0)
        kd = gk_c * jnp.exp(bend - b)
        gv_bf = gv_c.astype(BF16)
        gvT_bf = gv_c.T.astype(BF16)

        rms = [jnp.logical_and(rowi >= i * GLA_SUB, rowi < (i + 1) * GLA_SUB) for i in range(nsub)]
        uts = [_dot(gvT_bf, jnp.where(rms[i], kd, 0.0).astype(BF16)) for i in range(nsub)]
        states = [bdt_s[...]]
        for i in range(nsub):
            decay = jnp.exp(bend_l[i] - b0_l[i])
            states.append(states[i] * decay + jnp.where(bd_mask, uts[i], 0.0))
        bdt_s[...] = states[nsub]
        o_inter = None
        for i in range(nsub):
            d = _dot_nt(jnp.where(rms[i], qp, 0.0).astype(BF16), states[i].astype(BF16))
            o_inter = d if o_inter is None else o_inter + d

        for h in range(n_heads_gla):
            hm = jnp.logical_and(lanei >= h * dk, lanei < (h + 1) * dk)
            a = _dot_nt(jnp.where(hm, qt, jnp.zeros_like(qt)), kt)
            a = jnp.where(tri_blk, a, 0.0).astype(BF16)
            o_h = _dot(a, gv_bf[:, h * dv:(h + 1) * dv]) + o_inter[:, h * dv:(h + 1) * dv]
            o_n = _rms(o_h, gnorm_ref[...])
            go = o_n * _silu(gg_c[:, h * dv:(h + 1) * dv])
            go_out[0, pl.ds(r0, GLA_CHUNK), h * dv:(h + 1) * dv] = go.astype(BF16)
        return 0

    for ci in range(tm // GLA_CHUNK):
        chunk_body(ci, 0)

    @pl.when(t == nt - 1)
    def _():
        bd = bdt_s[...].T
        for h in range(n_heads_gla):
            sfin_out[0, h] = bd[h * dk:(h + 1) * dk, h * dv:(h + 1) * dv]


def _attn_prompt_kernel(qT_ref, kaug_ref, vT_ref, c_ref, oT_ref, vaug_s, qa_s, m_s, acc_s, *, t_blk, dh):
    i = pl.program_id(2)
    hw2 = 2 * dh

    @pl.when(i == 0)
    def _():
        for h in range(2):
            vaug_s[h, 0:dh, :] = vT_ref[0, h * dh:(h + 1) * dh, :].astype(BF16)
            vaug_s[h, dh:, :] = jnp.ones((vaug_s.shape[1] - dh, vaug_s.shape[2]), BF16)

    qT = qT_ref[0]
    rowq = lax.broadcasted_iota(jnp.int32, (hw2, t_blk), 0)
    keyi = lax.broadcasted_iota(jnp.int32, (ATT_KS, t_blk), 0)
    qryi = lax.broadcasted_iota(jnp.int32, (ATT_KS, t_blk), 1)
    t0 = pl.multiple_of(i * t_blk, t_blk)
    c_q = []
    for h in range(2):
        spare = (1 - h) * dh
        own_r = jnp.logical_and(rowq >= h * dh, rowq < (h + 1) * dh)
        ones_r = jnp.logical_and(rowq >= spare, rowq < spare + N_AUG)
        qa_s[h] = jnp.where(own_r, qT, jnp.where(ones_r, 1.0, 0.0).astype(BF16))
        m_s[h] = jnp.full(m_s.shape[1:], -jnp.inf, F32)
        acc_s[h] = jnp.zeros(acc_s.shape[1:], F32)
        c_q.append(c_ref[0, 0, h:h + 1, pl.ds(t0, LANES)][:, 0:1])

    n_sub = t_blk // ATT_KS

    def steps(tiles):
        work = []
        for j, masked in tiles:
            k0 = pl.multiple_of(j * t_blk, t_blk)
            kks = [pl.multiple_of(k0 + ks * ATT_KS, ATT_KS) for ks in range(n_sub)]
            s_all = [[_dot(kaug_ref[0, 0, h, pl.ds(kks[ks], ATT_KS), :], qa_s[h]) for ks in range(n_sub)]
                     for h in range(2)]
            work.append((k0, kks, s_all, masked))
        for h in range(2):
            m_run = m_s[h]
            acc = acc_s[h]
            for k0, kks, s_all, masked in work:
                off = (c_ref[0, 0, h:h + 1, pl.ds(k0, LANES)][:, 0:1] - c_q[h]) * LOG2E
                for ks in range(n_sub):
                    s = s_all[h][ks]
                    if masked:
                        s = jnp.where(keyi + ks * ATT_KS <= qryi, s, -jnp.inf)
                    m_new = jnp.maximum(m_run, jnp.max(s, axis=0, keepdims=True) - off)
                    p = jnp.exp2(s - (m_new + off))
                    alpha = jnp.exp2(m_run - m_new)
                    pv = _dot(vaug_s[h, :, pl.ds(kks[ks], ATT_KS)], p.astype(BF16))
                    acc = alpha * acc + pv
                    m_run = m_new
            acc_s[h] = acc
            m_s[h] = m_run

    def pair_body(jj, carry):
        steps([(2 * jj, False), (2 * jj + 1, False)])
        return carry

    lax.fori_loop(0, lax.shift_right_logical(i, 1), pair_body, 0)
    odd = jnp.bitwise_and(i, 1) == 1

    @pl.when(odd)
    def _():
        steps([(i - 1, False), (i, True)])

    @pl.when(jnp.logical_not(odd))
    def _():
        steps([(i, True)])

    outs = [acc_s[h][0:dh, :] / acc_s[h][dh:dh + 1, :] for h in range(2)]
    oT_ref[0] = jnp.concatenate(outs, axis=0).astype(BF16)


def _ffn_kernel(x_ref, foT_ref, go_ref, wo_ref, wup_ref, wdn_ref, g2_ref, gf_ref, y_ref, u_s, *, fox_w, d_ff):
    x = x_ref[0]
    h = x + (_dot(foT_ref[0].T, wo_ref[0:fox_w, :]) + _dot(go_ref[0], wo_ref[fox_w:, :]))
    hn = _rms(h, g2_ref[...]).astype(BF16)
    for c in range(d_ff // FFN_CHUNK):
        u = _dot(hn, wup_ref[:, c * FFN_CHUNK:(c + 1) * FFN_CHUNK])
        u_s[:, c * FFN_CHUNK:(c + 1) * FFN_CHUNK] = jnp.square(jnp.maximum(u, 0.0)).astype(BF16)
    y_ref[0] = _rms(h + _dot(u_s[...], wdn_ref[...]), gf_ref[...])


def _ffn_call(x3, foT, go, wo, wup, wdn, g2, gf, tm):
    nb, n, d = x3.shape
    fox_w = foT.shape[1]
    d_ff = wup.shape[1]
    const = lambda shape: pl.BlockSpec(shape, lambda b, i: (0, 0), pipeline_mode=pl.Buffered(1))
    return pl.pallas_call(
        functools.partial(_ffn_kernel, fox_w=fox_w, d_ff=d_ff),
        grid=(nb, n // tm),
        in_specs=[
            pl.BlockSpec((1, tm, d), lambda b, i: (b, i, 0)),
            pl.BlockSpec((1, fox_w, tm), lambda b, i: (b, 0, i)),
            pl.BlockSpec((1, tm, go.shape[2]), lambda b, i: (b, i, 0)),
            const(wo.shape), const(wup.shape), const(wdn.shape), const(g2.shape), const(gf.shape),
        ],
        out_specs=pl.BlockSpec((1, tm, d), lambda b, i: (b, i, 0)),
        out_shape=jax.ShapeDtypeStruct((nb, n, d), F32),
        scratch_shapes=[pltpu.VMEM((tm, d_ff), BF16)],
        compiler_params=pltpu.CompilerParams(dimension_semantics=("arbitrary", "arbitrary"),
                                             vmem_limit_bytes=VMEM_LIMIT_BYTES),
        name="merge_ffn",
    )(x3, foT, go, wo, wup, wdn, g2, gf)


def _proj_sample_kernel(x_ref, g1_ref, wtok_ref, wdm_ref, bf_ref, wgate_ref, bgate_ref,
                        q_out, k_out, v_out, lf_out, gq_out, gk_out, gv_out, gg_out, la_out,
                        *, fox_w, fox_dh, gla_kw, gla_vw, n_heads_gla, n_heads_fox):
    dk = gla_kw // n_heads_gla
    xn = _rms(x_ref[...], g1_ref[...]).astype(BF16)
    z = _dot_nt(xn, wtok_ref[...])
    o0 = fox_w
    gq_out[...] = z[:, o0:o0 + gla_kw] * (dk ** -0.5)
    o0 += gla_kw
    gk_out[...] = z[:, o0:o0 + gla_kw]
    o0 += gla_kw
    gv_out[...] = z[:, o0:o0 + gla_vw]
    o0 += gla_vw
    gg_out[...] = z[:, o0:o0 + gla_vw]
    o0 += gla_vw
    glr = z[:, o0:o0 + LANES].astype(BF16)
    la_out[...] = _log_sigmoid(_dot(glr, wgate_ref[...]) + bgate_ref[...]) * (1.0 / GLA_GATE_NORM)
    z2 = _dot_nt(xn, wdm_ref[...])
    q_out[...] = z2[:, 0:fox_w] * (fox_dh ** -0.5)
    k_out[...] = z2[:, fox_w:2 * fox_w]
    v_out[...] = z2[:, 2 * fox_w:3 * fox_w]
    lf_out[...] = _log_sigmoid(z2[:, 3 * fox_w:3 * fox_w + n_heads_fox] + bf_ref[...])


def _decode_kernel(pt_ref, q_ref, knew_ref, vnew_ref, lfnew_ref, kc_hbm, vc_hbm, lfc_hbm, o_ref,
                   ring, lfbuf, rev_s, zbuf, acc_s, ustrict_s, sem_ring, sem_lf,
                   *, n_pages, n_b, n_heads, dh):
    b = pl.program_id(0)
    page = LANES
    hw = n_heads * dh
    nch = n_pages // DEC_G
    per_b = 2 * nch
    total = n_b * per_b

    def start_chunk(g):
        bg = g // per_b
        c = g - bg * per_b
        slot = lax.rem(g, DEC_NSLOT)

        @pl.when(c < nch)
        def _():
            for j in range(DEC_G):
                p = n_pages - 1 - (c * DEC_G + j)
                pltpu.make_async_copy(kc_hbm.at[pt_ref[bg, p]], ring.at[slot, j], sem_ring.at[slot]).start()

        @pl.when(c >= nch)
        def _():
            for j in range(DEC_G):
                p = (c - nch) * DEC_G + j
                pltpu.make_async_copy(vc_hbm.at[pt_ref[bg, p]], ring.at[slot, j], sem_ring.at[slot]).start()

    def wait_chunk(g):
        slot = lax.rem(g, DEC_NSLOT)
        for j in range(DEC_G):
            pltpu.make_async_copy(kc_hbm.at[0], ring.at[slot, j], sem_ring.at[slot]).wait()

    def start_lf(bb):
        sl = lax.rem(bb, 2)

        def body(p, _):
            pltpu.make_async_copy(lfc_hbm.at[pt_ref[bb, p]], lfbuf.at[sl, p], sem_lf.at[sl]).start()
            return 0

        lax.fori_loop(0, n_pages, body, 0)

    def wait_lf(bb):
        sl = lax.rem(bb, 2)

        def body(p, _):
            pltpu.make_async_copy(lfc_hbm.at[0], lfbuf.at[sl, p], sem_lf.at[sl]).wait()
            return 0

        lax.fori_loop(0, n_pages, body, 0)

    g0 = b * per_b

    @pl.when(b == 0)
    def _():
        r = lax.broadcasted_iota(jnp.int32, (page, page), 0)
        c = lax.broadcasted_iota(jnp.int32, (page, page), 1)
        ustrict_s[...] = jnp.where(r > c, 1.0, 0.0).astype(BF16)
        start_lf(b)
        for g in range(DEC_NSLOT - 1):
            start_chunk(g0 + g)

    wait_lf(b)

    @pl.when(b + 1 < n_b)
    def _():
        start_lf(b + 1)

    sl = lax.rem(b, 2)
    lf2d = lfbuf[sl].reshape(n_pages * n_heads, page)
    hi, mid, lo = _split3(lf2d)
    u = ustrict_s[...]
    rev = _dot(hi.astype(BF16), u) + _dot(mid.astype(BF16), u) + _dot(lo.astype(BF16), u)
    rev_s[...] = rev.reshape(n_pages, n_heads, page)

    qb = jnp.broadcast_to(q_ref[0], (hw, page))

    def head_sum(x):
        return jnp.sum(x.reshape(n_heads, dh, page), axis=1)

    def head_bcast(x):
        return jnp.broadcast_to(x[:, None, :], (n_heads, dh, page)).reshape(hw, page)

    def k_body(c, carry):
        g = g0 + c

        @pl.when(g + (DEC_NSLOT - 1) < total)
        def _():
            start_chunk(g + (DEC_NSLOT - 1))

        wait_chunk(g)
        slot = lax.rem(g, DEC_NSLOT)
        for j in range(DEC_G):
            p = n_pages - 1 - (c * DEC_G + j)
            s = head_sum(ring[slot, j] * qb)
            revp = rev_s[p]
            zbuf[:, pl.ds(pl.multiple_of(p * page, page), page)] = s + revp + carry
            carry = carry + jnp.broadcast_to(revp[:, 0:1] + lfbuf[sl, p][:, 0:1], (n_heads, page))
        return carry

    carry0 = jnp.broadcast_to(lfnew_ref[0], (n_heads, page))
    lax.fori_loop(0, nch, k_body, carry0)

    z_all = zbuf[...]
    z_new = head_sum(jnp.broadcast_to(knew_ref[0], (hw, page)) * qb)
    m = jnp.maximum(jnp.max(z_all, axis=1, keepdims=True), z_new[:, 0:1])
    p_all = jnp.exp(z_all - m)
    zbuf[...] = p_all
    p_new = jnp.exp(z_new - m)
    l = jnp.sum(p_all, axis=1, keepdims=True) + p_new

    acc_s[...] = jnp.zeros_like(acc_s)

    def v_body(c, carry):
        g = g0 + nch + c

        @pl.when(g + (DEC_NSLOT - 1) < total)
        def _():
            start_chunk(g + (DEC_NSLOT - 1))

        wait_chunk(g)
        slot = lax.rem(g, DEC_NSLOT)
        for j in range(DEC_G):
            p = c * DEC_G + j
            pp = zbuf[:, pl.ds(pl.multiple_of(p * page, page), page)]
            acc_s[...] += ring[slot, j] * head_bcast(pp)
        return carry

    lax.fori_loop(0, nch, v_body, 0)

    num = jnp.sum(acc_s[...], axis=1, keepdims=True) + head_bcast(p_new) * jnp.broadcast_to(vnew_ref[0], (hw, page))
    o_ref[0] = (num / head_bcast(l))[:, 0:1]


def _gla_sample_kernel(s_ref, la_ref, k_ref, q_ref, v_ref, gg_ref, gnorm_ref, s_out, go_out, *, n_heads, dk, dv):
    for h in range(n_heads):
        la = la_ref[0, h * dk:(h + 1) * dk, :]
        kk = k_ref[0, h * dk:(h + 1) * dk, :]
        qq = q_ref[0, h * dk:(h + 1) * dk, :]
        vv = v_ref[0, :, h * dv:(h + 1) * dv]
        s_new = s_ref[0, h] * jnp.exp(la) + kk * vv
        s_out[0, h] = s_new
        o = jnp.sum(qq * s_new, axis=0, keepdims=True)
        o_n = _rms(o, gnorm_ref[...])
        go_out[0, :, h * dv:(h + 1) * dv] = o_n * _silu(gg_ref[0, :, h * dv:(h + 1) * dv])


def kernel(x_prompt, x_sample, cache_k, cache_v, cache_logf, state_gla, page_table, norm1_g, w_in, fox_b_f,
           gla_w_gate_up, gla_b_gate, gla_norm_g, w_o, norm2_g, w_up, w_down, final_g):
    B, S, D = x_prompt.shape
    Bd = x_sample.shape[0]
    depth, n_phys, page, H, dh = cache_k.shape
    _, _, Hg, dk, dv = state_gla.shape
    assert depth == 1 and x_sample.shape[1] == 1 and page == LANES
    fox_w = H * dh
    gla_kw = Hg * dk
    gla_vw = Hg * dv
    rank = gla_w_gate_up.shape[1]
    n_pages = page_table.shape[1]

    wt = jnp.transpose(w_in[0])
    o_fq, o_fk = 0, fox_w
    o_ff = 3 * fox_w
    o_gq = o_ff + H
    o_gk = o_gq + gla_kw
    o_gv = o_gk + gla_kw
    o_glr = o_gv + gla_vw
    o_gg = o_glr + rank
    misc_pad = LANES - FF_LANE0 - H
    w_tok = jnp.concatenate([
        wt[o_fk:o_fk + fox_w], wt[o_gq:o_gq + gla_kw], wt[o_gk:o_gk + gla_kw], wt[o_gv:o_gv + gla_vw],
        wt[o_gg:o_gg + gla_vw], wt[o_glr:o_glr + rank], jnp.zeros((FF_LANE0 - rank, D), F32),
        wt[o_ff:o_ff + H], jnp.zeros((misc_pad, D), F32)], axis=0).astype(BF16)
    w_dm = jnp.concatenate([
        wt[o_fq:o_fq + 3 * fox_w], wt[o_ff:o_ff + H], jnp.zeros((2 * SUBLANES - H, D), F32)], axis=0).astype(BF16)
    bf_row = jnp.concatenate([jnp.zeros((1, FF_LANE0), F32), fox_b_f.reshape(1, H), jnp.zeros((1, misc_pad), F32)],
                             axis=1)
    w_gate = jnp.concatenate([gla_w_gate_up[0], jnp.zeros((LANES - rank, gla_kw), F32)], axis=0).astype(BF16)
    wo_bf = w_o[0].astype(BF16)
    wup_bf = w_up[0].astype(BF16)
    wdn_bf = w_down[0].astype(BF16)
    g1 = norm1_g.reshape(1, D)
    g2 = norm2_g.reshape(1, D)
    gf = final_g.reshape(1, D)
    bgate = gla_b_gate.reshape(1, gla_kw)
    gnorm = gla_norm_g.reshape(1, dv)

    cparams = lambda sem: pltpu.CompilerParams(dimension_semantics=sem, vmem_limit_bytes=VMEM_LIMIT_BYTES)
    single = lambda shape: pl.BlockSpec(shape, lambda *_: (0,) * len(shape), pipeline_mode=pl.Buffered(1))

    tm = PROJ_TM
    nt = S // tm
    n_tok_cols = w_tok.shape[0]
    n_dm_rows = w_dm.shape[0]
    qT_p, kT_p, vT_p, lfT_p, cT_p, kaug_p, go_p, sfin_p = pl.pallas_call(
        functools.partial(_proj_prompt_kernel, tm=tm, fox_w=fox_w, fox_dh=dh, gla_kw=gla_kw, gla_vw=gla_vw,
                          n_heads_gla=Hg),
        grid=(B, nt),
        in_specs=[
            pl.BlockSpec((1, tm, D), lambda b, t: (b, t, 0)),
            single((1, D)), single((n_tok_cols, D)), single((n_dm_rows, D)), single((H, 1)), single((1, LANES)),
            single((LANES, gla_kw)), single((1, gla_kw)), single((1, dv)),
        ],
        out_specs=[
            pl.BlockSpec((1, fox_w, tm), lambda b, t: (b, 0, t)),
            pl.BlockSpec((1, fox_w, tm), lambda b, t: (b, 0, t)),
            pl.BlockSpec((1, fox_w, tm), lambda b, t: (b, 0, t)),
            pl.BlockSpec((1, H, tm), lambda b, t: (b, 0, t)),
            pl.BlockSpec((1, H, tm), lambda b, t: (b, 0, t)),
            pl.BlockSpec((1, H, tm, LANES), lambda b, t: (b, 0, t, 0)),
            pl.BlockSpec((1, tm, gla_vw), lambda b, t: (b, t, 0)),
            pl.BlockSpec((1, Hg, dk, dv), lambda b, t: (b, 0, 0, 0)),
        ],
        out_shape=[
            jax.ShapeDtypeStruct((B, fox_w, S), BF16),
            jax.ShapeDtypeStruct((B, fox_w, S), F32),
            jax.ShapeDtypeStruct((B, fox_w, S), F32),
            jax.ShapeDtypeStruct((B, H, S), F32),
            jax.ShapeDtypeStruct((B, H, S), F32),
            jax.ShapeDtypeStruct((B, H, S, LANES), BF16),
            jax.ShapeDtypeStruct((B, S, gla_vw), BF16),
            jax.ShapeDtypeStruct((B, Hg, dk, dv), F32),
        ],
        scratch_shapes=[
            pltpu.VMEM((H, LANES), F32),
            pltpu.VMEM((gla_vw, gla_kw), F32),
            pltpu.VMEM((tm, tm), BF16),
            pltpu.VMEM((tm, tm), BF16),
            pltpu.VMEM((GLA_CHUNK, GLA_CHUNK), BF16),
            pltpu.VMEM((tm, gla_kw), F32), pltpu.VMEM((tm, gla_kw), F32),
            pltpu.VMEM((tm, gla_vw), F32), pltpu.VMEM((tm, gla_vw), F32), pltpu.VMEM((tm, gla_kw), F32),
        ],
        compiler_params=cparams(("arbitrary", "arbitrary")),
        name="proj_gla_prompt",
    )(x_prompt, g1, w_tok, w_dm, fox_b_f.reshape(H, 1), bf_row, w_gate, bgate, gnorm)

    t_blk = ATT_T
    assert t_blk == tm
    nq = S // t_blk
    n_pairs = fox_w // LANES
    c4 = cT_p.reshape(B, n_pairs, 2, S)
    k5 = kaug_p.reshape(B, n_pairs, 2, S, LANES)
    n_vrows = dh + 2 * SUBLANES
    foT_p = pl.pallas_call(
        functools.partial(_attn_prompt_kernel, t_blk=t_blk, dh=dh),
        grid=(B, n_pairs, nq),
        in_specs=[
            pl.BlockSpec((1, LANES, t_blk), lambda b, p, i: (b, p, i)),
            pl.BlockSpec((1, 1, 2, S, LANES), lambda b, p, i: (b, p, 0, 0, 0)),
            pl.BlockSpec((1, LANES, S), lambda b, p, i: (b, p, 0)),
            pl.BlockSpec((1, 1, 2, S), lambda b, p, i: (b, p, 0, 0)),
        ],
        out_specs=pl.BlockSpec((1, LANES, t_blk), lambda b, p, i: (b, p, i)),
        out_shape=jax.ShapeDtypeStruct((B, fox_w, S), BF16),
        scratch_shapes=[
            pltpu.VMEM((2, n_vrows, S), BF16),
            pltpu.VMEM((2, LANES, t_blk), BF16),
            pltpu.VMEM((2, 1, t_blk), F32),
            pltpu.VMEM((2, n_vrows, t_blk), F32),
        ],
        compiler_params=cparams(("arbitrary", "arbitrary", "arbitrary")),
        name="fox_attn_prompt",
    )(qT_p, k5, vT_p, c4)

    y_p = _ffn_call(x_prompt, foT_p, go_p, wo_bf, wup_bf, wdn_bf, g2, gf, FFN_TM)

    xs = x_sample.reshape(Bd, D)
    full = lambda shape: pl.BlockSpec(shape, lambda: (0,) * len(shape))
    s_shapes = [(Bd, fox_w), (Bd, fox_w), (Bd, fox_w), (Bd, H), (Bd, gla_kw), (Bd, gla_kw), (Bd, gla_vw),
                (Bd, gla_vw), (Bd, gla_kw)]
    q_s, k_s, v_s, lf_s, gq_s, gk_s, gv_s, gg_s, la_s = pl.pallas_call(
        functools.partial(_proj_sample_kernel, fox_w=fox_w, fox_dh=dh, gla_kw=gla_kw, gla_vw=gla_vw, n_heads_gla=Hg,
                          n_heads_fox=H),
        in_specs=[full((Bd, D)), full((1, D)), full((n_tok_cols, D)), full((n_dm_rows, D)), full((1, H)),
                  full((LANES, gla_kw)), full((1, gla_kw))],
        out_specs=[full(s) for s in s_shapes],
        out_shape=[jax.ShapeDtypeStruct(s, F32) for s in s_shapes],
        compiler_params=pltpu.CompilerParams(vmem_limit_bytes=VMEM_LIMIT_BYTES),
        name="proj_sample",
    )(xs, g1, w_tok, w_dm, fox_b_f.reshape(1, H), w_gate, bgate)

    kc = jnp.transpose(cache_k[0], (0, 2, 3, 1)).reshape(n_phys, fox_w, page)
    vc = jnp.transpose(cache_v[0], (0, 2, 3, 1)).reshape(n_phys, fox_w, page)
    lfc = jnp.transpose(cache_logf[0], (0, 2, 1))
    col = lambda a: a.reshape(Bd, a.shape[1], 1)
    grid_spec = pltpu.PrefetchScalarGridSpec(
        num_scalar_prefetch=1,
        grid=(Bd,),
        in_specs=[
            pl.BlockSpec((1, fox_w, 1), lambda b, pt: (b, 0, 0)),
            pl.BlockSpec((1, fox_w, 1), lambda b, pt: (b, 0, 0)),
            pl.BlockSpec((1, fox_w, 1), lambda b, pt: (b, 0, 0)),
            pl.BlockSpec((1, H, 1), lambda b, pt: (b, 0, 0)),
            pl.BlockSpec(memory_space=pl.ANY), pl.BlockSpec(memory_space=pl.ANY), pl.BlockSpec(memory_space=pl.ANY),
        ],
        out_specs=pl.BlockSpec((1, fox_w, 1), lambda b, pt: (b, 0, 0)),
        scratch_shapes=[
            pltpu.VMEM((DEC_NSLOT, DEC_G, fox_w, page), F32),
            pltpu.VMEM((2, n_pages, H, page), F32),
            pltpu.VMEM((n_pages, H, page), F32),
            pltpu.VMEM((H, n_pages * page), F32),
            pltpu.VMEM((fox_w, page), F32),
            pltpu.VMEM((page, page), BF16),
            pltpu.SemaphoreType.DMA((DEC_NSLOT,)),
            pltpu.SemaphoreType.DMA((2,)),
        ],
    )
    fo_s = pl.pallas_call(
        functools.partial(_decode_kernel, n_pages=n_pages, n_b=Bd, n_heads=H, dh=dh),
        grid_spec=grid_spec,
        out_shape=jax.ShapeDtypeStruct((Bd, fox_w, 1), F32),
        compiler_params=cparams(("arbitrary",)),
        name="fox_decode",
    )(page_table, col(q_s), col(k_s), col(v_s), col(lf_s), kc, vc, lfc)
    foT_s = fo_s.reshape(Bd, fox_w).T.astype(BF16)

    s_new, go_s = pl.pallas_call(
        functools.partial(_gla_sample_kernel, n_heads=Hg, dk=dk, dv=dv),
        grid=(Bd,),
        in_specs=[
            pl.BlockSpec((1, Hg, dk, dv), lambda b: (b, 0, 0, 0)),
            pl.BlockSpec((1, gla_kw, 1), lambda b: (b, 0, 0)),
            pl.BlockSpec((1, gla_kw, 1), lambda b: (b, 0, 0)),
            pl.BlockSpec((1, gla_kw, 1), lambda b: (b, 0, 0)),
            pl.BlockSpec((1, 1, gla_vw), lambda b: (b, 0, 0)),
            pl.BlockSpec((1, 1, gla_vw), lambda b: (b, 0, 0)),
            pl.BlockSpec((1, dv), lambda b: (0, 0)),
        ],
        out_specs=[
            pl.BlockSpec((1, Hg, dk, dv), lambda b: (b, 0, 0, 0)),
            pl.BlockSpec((1, 1, gla_vw), lambda b: (b, 0, 0)),
        ],
        out_shape=[jax.ShapeDtypeStruct((Bd, Hg, dk, dv), F32), jax.ShapeDtypeStruct((Bd, 1, gla_vw), F32)],
        compiler_params=cparams(("arbitrary",)),
        name="gla_sample",
    )(state_gla[0], col(la_s), col(gk_s), col(gq_s), gv_s.reshape(Bd, 1, gla_vw), gg_s.reshape(Bd, 1, gla_vw), gnorm)
    go_s = go_s.reshape(Bd, gla_vw).astype(BF16)

    y_s = _ffn_call(xs[None], foT_s[None], go_s[None], wo_bf, wup_bf, wdn_bf, g2, gf, Bd).reshape(Bd, 1, D)

    new_k_p = jnp.transpose(kT_p.reshape(1, B, H, dh, S), (0, 1, 4, 2, 3))
    new_v_p = jnp.transpose(vT_p.reshape(1, B, H, dh, S), (0, 1, 4, 2, 3))
    new_lf_p = jnp.transpose(lfT_p, (0, 2, 1)).reshape(1, B, S, H)
    return (y_p, y_s, new_k_p, new_v_p, new_lf_p, sfin_p.reshape(1, B, Hg, dk, dv),
            k_s.reshape(1, Bd, 1, H, dh), v_s.reshape(1, Bd, 1, H, dh), lf_s.reshape(1, Bd, 1, H),
            s_new.reshape(1, Bd, Hg, dk, dv))
```

```python
import functools

import jax
import jax.numpy as jnp
from jax import lax
from jax.experimental import pallas as pl
from jax.experimental.pallas import tpu as pltpu

F32 = jnp.float32
BF16 = jnp.bfloat16

LANES = 128
SUBLANES = 8
VMEM_LIMIT_BYTES = 56 * 1024 * 1024

EPS = 1e-6
LOG2E = 1.4426950408889634
N_AUG = 3
FF_LANE0 = 16
GLA_GATE_NORM = 16.0
GLA_CHUNK = 128
GLA_SUB = 32
PROJ_TM = 512
ATT_T = 512
ATT_KS = 256
FFN_TM = 512
FFN_CHUNK = 1024
DEC_G = 8
DEC_NSLOT = 5


def _dot(a, b):
    return jnp.dot(a, b, preferred_element_type=F32)


def _dot_nt(a, b):
    return lax.dot_general(a, b, (((1,), (1,)), ((), ())), preferred_element_type=F32)


def _split3(x):
    hi = x.astype(BF16).astype(F32)
    r = x - hi
    mid = r.astype(BF16).astype(F32)
    lo = r - mid
    return hi, mid, lo


def _log_sigmoid(x):
    return jnp.minimum(x, 0.0) - jnp.log1p(jnp.exp(-jnp.abs(x)))


def _silu(x):
    return x / (1.0 + jnp.exp(-x))


def _rms(x, g):
    return x * lax.rsqrt(jnp.mean(x * x, axis=-1, keepdims=True) + EPS) * g


def _proj_prompt_kernel(x_ref, g1_ref, wtok_ref, wdm_ref, bf_ref, bfrow_ref, wgate_ref, bgate_ref, gnorm_ref,
                        qT_out, kT_out, vT_out, lfT_out, cT_out, kaug_out, go_out, sfin_out,
                        carry_s, bdt_s, uincl_s, ltm_s, lincl_s, gq_s, gk_s, gv_s, gg_s, la_s,
                        *, tm, fox_w, fox_dh, gla_kw, gla_vw, n_heads_gla):
    t = pl.program_id(1)
    nt = pl.num_programs(1)
    dk = gla_kw // n_heads_gla
    dv = gla_vw // n_heads_gla

    @pl.when(jnp.logical_and(pl.program_id(0) == 0, t == 0))
    def _():
        r = lax.broadcasted_iota(jnp.int32, (tm, tm), 0)
        c = lax.broadcasted_iota(jnp.int32, (tm, tm), 1)
        uincl_s[...] = jnp.where(r <= c, 1.0, 0.0).astype(BF16)
        ltm_s[...] = jnp.where(c <= r, 1.0, 0.0).astype(BF16)
        r = lax.broadcasted_iota(jnp.int32, (GLA_CHUNK, GLA_CHUNK), 0)
        c = lax.broadcasted_iota(jnp.int32, (GLA_CHUNK, GLA_CHUNK), 1)
        lincl_s[...] = jnp.where(c <= r, 1.0, 0.0).astype(BF16)

    @pl.when(t == 0)
    def _():
        carry_s[...] = jnp.zeros_like(carry_s)
        bdt_s[...] = jnp.zeros_like(bdt_s)

    x = x_ref[0]
    xn = _rms(x, g1_ref[...]).astype(BF16)

    z = _dot_nt(xn, wtok_ref[...])
    kz = z[:, 0:fox_w]
    o0 = fox_w
    gq_s[...] = z[:, o0:o0 + gla_kw] * (dk ** -0.5)
    o0 += gla_kw
    gk_s[...] = z[:, o0:o0 + gla_kw]
    o0 += gla_kw
    gv_s[...] = z[:, o0:o0 + gla_vw]
    o0 += gla_vw
    gg_s[...] = z[:, o0:o0 + gla_vw]
    o0 += gla_vw
    misc = z[:, o0:o0 + LANES]
    pre = _dot(misc.astype(BF16), wgate_ref[...]) + bgate_ref[...]
    la_s[...] = _log_sigmoid(pre) * (1.0 / GLA_GATE_NORM)

    lf_tok = _log_sigmoid(misc + bfrow_ref[...])
    st3 = jnp.concatenate(_split3(lf_tok), axis=1).astype(BF16)
    cc = _dot(ltm_s[...], st3)
    cs_tok = cc[:, 0:LANES] + cc[:, LANES:2 * LANES] + cc[:, 2 * LANES:3 * LANES]
    d_tok = (cs_tok - cs_tok[0:1, :]) * LOG2E
    lane_k = lax.broadcasted_iota(jnp.int32, (tm, LANES), 1)
    for h in range(fox_w // fox_dh):
        own = (h % 2) * fox_dh
        spare = (1 - h % 2) * fox_dh
        parts = _split3(jnp.broadcast_to(d_tok[:, FF_LANE0 + h:FF_LANE0 + h + 1], (tm, LANES)))
        aug = jnp.zeros((tm, LANES), F32)
        for n, part in enumerate(parts):
            aug = jnp.where(lane_k == spare + n, -part, aug)
        own_l = jnp.logical_and(lane_k >= own, lane_k < own + fox_dh)
        kaug_out[0, h] = jnp.where(own_l, kz[:, (h // 2) * LANES:(h // 2 + 1) * LANES], aug).astype(BF16)

    zt = _dot_nt(wdm_ref[...], xn)
    qT_out[0] = (zt[0:fox_w] * (fox_dh ** -0.5 * LOG2E)).astype(BF16)
    kT_out[0] = zt[fox_w:2 * fox_w]
    vT_out[0] = zt[2 * fox_w:3 * fox_w]
    lf = _log_sigmoid(zt[3 * fox_w:3 * fox_w + SUBLANES] + bf_ref[...])
    lfT_out[0] = lf
    hi, mid, lo = _split3(lf)
    stack = jnp.concatenate([hi, mid, lo, jnp.zeros_like(hi)], axis=0).astype(BF16)
    cs = _dot(stack, uincl_s[...])
    cs = cs[0:8] + cs[8:16] + cs[16:24]
    carry = carry_s[...]
    cT_out[0] = cs + carry[:, 0:1]
    tot = _dot(stack, jnp.ones((tm, LANES), BF16))
    carry_s[...] = carry + tot[0:8] + tot[8:16] + tot[16:24]

    nsub = GLA_CHUNK // GLA_SUB
    rowi = lax.broadcasted_iota(jnp.int32, (GLA_CHUNK, gla_kw), 0)
    lanei = lax.broadcasted_iota(jnp.int32, (GLA_CHUNK, gla_kw), 1)
    ar = lax.broadcasted_iota(jnp.int32, (GLA_CHUNK, GLA_CHUNK), 0)
    ac = lax.broadcasted_iota(jnp.int32, (GLA_CHUNK, GLA_CHUNK), 1)
    tri_blk = jnp.logical_and(ar // GLA_SUB == ac // GLA_SUB, ar >= ac)
    br = lax.broadcasted_iota(jnp.int32, (gla_vw, gla_kw), 0)
    bc = lax.broadcasted_iota(jnp.int32, (gla_vw, gla_kw), 1)
    bd_mask = (br // dv) == (bc // dk)

    def chunk_body(ci, _):
        r0 = ci * GLA_CHUNK
        la_c = la_s[pl.ds(r0, GLA_CHUNK), :]
        gq_c = gq_s[pl.ds(r0, GLA_CHUNK), :]
        gk_c = gk_s[pl.ds(r0, GLA_CHUNK), :]
        gv_c = gv_s[pl.ds(r0, GLA_CHUNK), :]
        gg_c = gg_s[pl.ds(r0, GLA_CHUNK), :]
        h3, m3, l3 = _split3(la_c)
        st = jnp.concatenate([h3, m3, l3], axis=1).astype(BF16)
        bb = _dot(lincl_s[...], st)
        b = bb[:, 0:gla_kw] + bb[:, gla_kw:2 * gla_kw] + bb[:, 2 * gla_kw:3 * gla_kw]
        bmid_l, bend_l, b0_l = [], [], []
        for i in range(nsub):
            s0 = i * GLA_SUB
            bmid_l.append(b[s0 + GLA_SUB // 2:s0 + GLA_SUB // 2 + 1])
            bend_l.append(b[s0 + GLA_SUB - 1:s0 + GLA_SUB])
            b0_l.append(jnp.zeros((1, gla_kw), F32) if i == 0 else b[s0 - 1:s0])
        bc_rows = lambda rows: jnp.concatenate(
            [jnp.broadcast_to(r, (GLA_SUB, gla_kw)) for r in rows], axis=0)
        bmid, bend, b0 = bc_rows(bmid_l), bc_rows(bend_l), bc_rows(b0_l)
        qt = (gq_c * jnp.exp(b - bmid)).astype(BF16)
        kt = (gk_c * jnp.exp(bmid - b)).astype(BF16)
        qp = gq_c * jnp.exp(b - b0)
        kd = gk_c * jnp.exp(bend - b)
        gv_bf = gv_c.astype(BF16)
        gvT_bf = gv_c.T.astype(BF16)

        rms = [jnp.logical_and(rowi >= i * GLA_SUB, rowi < (i + 1) * GLA_SUB) for i in range(nsub)]
        uts = [_dot(gvT_bf, jnp.where(rms[i], kd, 0.0).astype(BF16)) for i in range(nsub)]
        states = [bdt_s[...]]
        for i in range(nsub):
            decay = jnp.exp(bend_l[i] - b0_l[i])
            states.append(states[i] * decay + jnp.where(bd_mask, uts[i], 0.0))
        bdt_s[...] = states[nsub]
        o_inter = None
        for i in range(nsub):
            d = _dot_nt(jnp.where(rms[i], qp, 0.0).astype(BF16), states[i].astype(BF16))
            o_inter = d if o_inter is None else o_inter + d

        for h in range(n_heads_gla):
            hm = jnp.logical_and(lanei >= h * dk, lanei < (h + 1) * dk)
            a = _dot_nt(jnp.where(hm, qt, jnp.zeros_like(qt)), kt)
            a = jnp.where(tri_blk, a, 0.0).astype(BF16)
            o_h = _dot(a, gv_bf[:, h * dv:(h + 1) * dv]) + o_inter[:, h * dv:(h + 1) * dv]
            o_n = _rms(o_h, gnorm_ref[...])
            go = o_n * _silu(gg_c[:, h * dv:(h + 1) * dv])
            go_out[0, pl.ds(r0, GLA_CHUNK), h * dv:(h + 1) * dv] = go.astype(BF16)
        return 0

    for ci in range(tm // GLA_CHUNK):
        chunk_body(ci, 0)

    @pl.when(t == nt - 1)
    def _():
        bd = bdt_s[...].T
        for h in range(n_heads_gla):
            sfin_out[0, h] = bd[h * dk:(h + 1) * dk, h * dv:(h + 1) * dv]


def _attn_prompt_kernel(qT_ref, kaug_ref, vT_ref, c_ref, oT_ref, vaug_s, qa_s, m_s, acc_s, *, t_blk, dh):
    i = pl.program_id(2)
    hw2 = 2 * dh

    @pl.when(i == 0)
    def _():
        for h in range(2):
            vaug_s[h, 0:dh, :] = vT_ref[0, h * dh:(h + 1) * dh, :].astype(BF16)
            vaug_s[h, dh:, :] = jnp.ones((vaug_s.shape[1] - dh, vaug_s.shape[2]), BF16)

    qT = qT_ref[0]
    rowq = lax.broadcasted_iota(jnp.int32, (hw2, t_blk), 0)
    keyi = lax.broadcasted_iota(jnp.int32, (ATT_KS, t_blk), 0)
    qryi = lax.broadcasted_iota(jnp.int32, (ATT_KS, t_blk), 1)
    t0 = pl.multiple_of(i * t_blk, t_blk)
    c_q = []
    for h in range(2):
        spare = (1 - h) * dh
        own_r = jnp.logical_and(rowq >= h * dh, rowq < (h + 1) * dh)
        ones_r = jnp.logical_and(rowq >= spare, rowq < spare + N_AUG)
        qa_s[h] = jnp.where(own_r, qT, jnp.where(ones_r, 1.0, 0.0).astype(BF16))
        m_s[h] = jnp.full(m_s.shape[1:], -jnp.inf, F32)
        acc_s[h] = jnp.zeros(acc_s.shape[1:], F32)
        c_q.append(c_ref[0, 0, h:h + 1, pl.ds(t0, LANES)][:, 0:1])

    n_sub = t_blk // ATT_KS

    def steps(tiles):
        work = []
        for j, masked in tiles:
            k0 = pl.multiple_of(j * t_blk, t_blk)
            kks = [pl.multiple_of(k0 + ks * ATT_KS, ATT_KS) for ks in range(n_sub)]
            s_all = [[_dot(kaug_ref[0, 0, h, pl.ds(kks[ks], ATT_KS), :], qa_s[h]) for ks in range(n_sub)]
                     for h in range(2)]
            work.append((k0, kks, s_all, masked))
        for h in range(2):
            m_run = m_s[h]
            acc = acc_s[h]
            for k0, kks, s_all, masked in work:
                off = (c_ref[0, 0, h:h + 1, pl.ds(k0, LANES)][:, 0:1] - c_q[h]) * LOG2E
                for ks in range(n_sub):
                    s = s_all[h][ks]
                    if masked:
                        s = jnp.where(keyi + ks * ATT_KS <= qryi, s, -jnp.inf)
                    m_new = jnp.maximum(m_run, jnp.max(s, axis=0, keepdims=True) - off)
                    p = jnp.exp2(s - (m_new + off))
                    alpha = jnp.exp2(m_run - m_new)
                    pv = _dot(vaug_s[h, :, pl.ds(kks[ks], ATT_KS)], p.astype(BF16))
                    acc = alpha * acc + pv
                    m_run = m_new
            acc_s[h] = acc
            m_s[h] = m_run

    def pair_body(jj, carry):
        steps([(2 * jj, False), (2 * jj + 1, False)])
        return carry

    lax.fori_loop(0, lax.shift_right_logical(i, 1), pair_body, 0)
    odd = jnp.bitwise_and(i, 1) == 1

    @pl.when(odd)
    def _():
        steps([(i - 1, False), (i, True)])

    @pl.when(jnp.logical_not(odd))
    def _():
        steps([(i, True)])

    outs = [acc_s[h][0:dh, :] / acc_s[h][dh:dh + 1, :] for h in range(2)]
    oT_ref[0] = jnp.concatenate(outs, axis=0).astype(BF16)


def _ffn_kernel(x_ref, foT_ref, go_ref, wo_ref, wup_ref, wdn_ref, g2_ref, gf_ref, y_ref, u_s, *, fox_w, d_ff):
    x = x_ref[0]
    h = x + (_dot(foT_ref[0].T, wo_ref[0:fox_w, :]) + _dot(go_ref[0], wo_ref[fox_w:, :]))
    hn = _rms(h, g2_ref[...]).astype(BF16)
    for c in range(d_ff // FFN_CHUNK):
        u = _dot(hn, wup_ref[:, c * FFN_CHUNK:(c + 1) * FFN_CHUNK])
        u_s[:, c * FFN_CHUNK:(c + 1) * FFN_CHUNK] = jnp.square(jnp.maximum(u, 0.0)).astype(BF16)
    y_ref[0] = _rms(h + _dot(u_s[...], wdn_ref[...]), gf_ref[...])


def _ffn_call(x3, foT, go, wo, wup, wdn, g2, gf, tm):
    nb, n, d = x3.shape
    fox_w = foT.shape[1]
    d_ff = wup.shape[1]
    const = lambda shape: pl.BlockSpec(shape, lambda b, i: (0, 0), pipeline_mode=pl.Buffered(1))
    return pl.pallas_call(
        functools.partial(_ffn_kernel, fox_w=fox_w, d_ff=d_ff),
        grid=(nb, n // tm),
        in_specs=[
            pl.BlockSpec((1, tm, d), lambda b, i: (b, i, 0)),
            pl.BlockSpec((1, fox_w, tm), lambda b, i: (b, 0, i)),
            pl.BlockSpec((1, tm, go.shape[2]), lambda b, i: (b, i, 0)),
            const(wo.shape), const(wup.shape), const(wdn.shape), const(g2.shape), const(gf.shape),
        ],
        out_specs=pl.BlockSpec((1, tm, d), lambda b, i: (b, i, 0)),
        out_shape=jax.ShapeDtypeStruct((nb, n, d), F32),
        scratch_shapes=[pltpu.VMEM((tm, d_ff), BF16)],
        compiler_params=pltpu.CompilerParams(dimension_semantics=("arbitrary", "arbitrary"),
                                             vmem_limit_bytes=VMEM_LIMIT_BYTES),
        name="merge_ffn",
    )(x3, foT, go, wo, wup, wdn, g2, gf)


def _proj_sample_kernel(x_ref, g1_ref, wtok_ref, wdm_ref, bf_ref, wgate_ref, bgate_ref,
                        q_out, k_out, v_out, lf_out, gq_out, gk_out, gv_out, gg_out, la_out,
                        *, fox_w, fox_dh, gla_kw, gla_vw, n_heads_gla, n_heads_fox):
    dk = gla_kw // n_heads_gla
    xn = _rms(x_ref[...], g1_ref[...]).astype(BF16)
    z = _dot_nt(xn, wtok_ref[...])
    o0 = fox_w
    gq_out[...] = z[:, o0:o0 + gla_kw] * (dk ** -0.5)
    o0 += gla_kw
    gk_out[...] = z[:, o0:o0 + gla_kw]
    o0 += gla_kw
    gv_out[...] = z[:, o0:o0 + gla_vw]
    o0 += gla_vw
    gg_out[...] = z[:, o0:o0 + gla_vw]
    o0 += gla_vw
    glr = z[:, o0:o0 + LANES].astype(BF16)
    la_out[...] = _log_sigmoid(_dot(glr, wgate_ref[...]) + bgate_ref[...]) * (1.0 / GLA_GATE_NORM)
    z2 = _dot_nt(xn, wdm_ref[...])
    q_out[...] = z2[:, 0:fox_w] * (fox_dh ** -0.5)
    k_out[...] = z2[:, fox_w:2 * fox_w]
    v_out[...] = z2[:, 2 * fox_w:3 * fox_w]
    lf_out[...] = _log_sigmoid(z2[:, 3 * fox_w:3 * fox_w + n_heads_fox] + bf_ref[...])


def _ffn_decode_kernel(pt_ref, x_ref, foT_ref, go_ref, wo_ref, wup_ref, wdn_ref, g2_ref, gf_ref,
                       q_ref, knew_ref, vnew_ref, lfnew_ref, kc_hbm, vc_hbm, lfc_hbm,
                       y_ref, o_ref,
                       u_s, ring, lfbuf, rev_s, zbuf, acc_s, ustrict_s, sem_ring, sem_lf,
                       *, fox_w, d_ff, n_pages, n_b, n_heads, dh):
    b = pl.program_id(0) * pl.num_programs(1) + pl.program_id(1)
    page = LANES
    hw = n_heads * dh
    nch = n_pages // DEC_G
    per_b = 2 * nch
    total = n_b * per_b

    def start_chunk(g):
        bg = g // per_b
        c = g - bg * per_b
        slot = lax.rem(g, DEC_NSLOT)

        @pl.when(c < nch)
        def _():
            for j in range(DEC_G):
                p = n_pages - 1 - (c * DEC_G + j)
                pltpu.make_async_copy(kc_hbm.at[pt_ref[bg, p]], ring.at[slot, j], sem_ring.at[slot]).start()

        @pl.when(c >= nch)
        def _():
            for j in range(DEC_G):
                p = (c - nch) * DEC_G + j
                pltpu.make_async_copy(vc_hbm.at[pt_ref[bg, p]], ring.at[slot, j], sem_ring.at[slot]).start()

    def wait_chunk(g):
        slot = lax.rem(g, DEC_NSLOT)
        for j in range(DEC_G):
            pltpu.make_async_copy(kc_hbm.at[0], ring.at[slot, j], sem_ring.at[slot]).wait()

    def start_lf(bb):
        sl = lax.rem(bb, 2)

        def body(p, _):
            pltpu.make_async_copy(lfc_hbm.at[pt_ref[bb, p]], lfbuf.at[sl, p], sem_lf.at[sl]).start()
            return 0

        lax.fori_loop(0, n_pages, body, 0)

    def wait_lf(bb):
        sl = lax.rem(bb, 2)

        def body(p, _):
            pltpu.make_async_copy(lfc_hbm.at[0], lfbuf.at[sl, p], sem_lf.at[sl]).wait()
            return 0

        lax.fori_loop(0, n_pages, body, 0)

    g0 = b * per_b

    @pl.when(b == 0)
    def _():
        r = lax.broadcasted_iota(jnp.int32, (page, page), 0)
        c = lax.broadcasted_iota(jnp.int32, (page, page), 1)
        ustrict_s[...] = jnp.where(r > c, 1.0, 0.0).astype(BF16)
        start_lf(b)
        for g in range(DEC_NSLOT - 1):
            start_chunk(g0 + g)

    wait_lf(b)

    @pl.when(b + 1 < n_b)
    def _():
        start_lf(b + 1)

    sl = lax.rem(b, 2)
    lf2d = lfbuf[sl].reshape(n_pages * n_heads, page)
    hi, mid, lo = _split3(lf2d)
    u = ustrict_s[...]
    rev = _dot(hi.astype(BF16), u) + _dot(mid.astype(BF16), u) + _dot(lo.astype(BF16), u)
    rev_s[...] = rev.reshape(n_pages, n_heads, page)

    qb = jnp.broadcast_to(q_ref[0], (hw, page))

    def head_sum(x):
        return jnp.sum(x.reshape(n_heads, dh, page), axis=1)

    def head_bcast(x):
        return jnp.broadcast_to(x[:, None, :], (n_heads, dh, page)).reshape(hw, page)

    def k_body(c, carry):
        g = g0 + c

        @pl.when(g + (DEC_NSLOT - 1) < total)
        def _():
            start_chunk(g + (DEC_NSLOT - 1))

        wait_chunk(g)
        slot = lax.rem(g, DEC_NSLOT)
        for j in range(DEC_G):
            p = n_pages - 1 - (c * DEC_G + j)
            s = head_sum(ring[slot, j] * qb)
            revp = rev_s[p]
            zbuf[:, pl.ds(pl.multiple_of(p * page, page), page)] = s + revp + carry
            carry = carry + jnp.broadcast_to(revp[:, 0:1] + lfbuf[sl, p][:, 0:1], (n_heads, page))
        return carry

    x = x_ref[0]
    h = x + (_dot(foT_ref[0].T, wo_ref[0:fox_w, :]) + _dot(go_ref[0], wo_ref[fox_w:, :]))
    hn = _rms(h, g2_ref[...]).astype(BF16)

    n_ffn = d_ff // FFN_CHUNK
    grp = nch // n_ffn
    carry = jnp.broadcast_to(lfnew_ref[0], (n_heads, page))
    for c in range(n_ffn):
        carry = lax.fori_loop(c * grp, (c + 1) * grp, k_body, carry)
        u = _dot(hn, wup_ref[:, c * FFN_CHUNK:(c + 1) * FFN_CHUNK])
        u_s[:, c * FFN_CHUNK:(c + 1) * FFN_CHUNK] = jnp.square(jnp.maximum(u, 0.0)).astype(BF16)

    z_all = zbuf[...]
    z_new = head_sum(jnp.broadcast_to(knew_ref[0], (hw, page)) * qb)
    m = jnp.maximum(jnp.max(z_all, axis=1, keepdims=True), z_new[:, 0:1])
    p_all = jnp.exp(z_all - m)
    zbuf[...] = p_all
    p_new = jnp.exp(z_new - m)
    l = jnp.sum(p_all, axis=1, keepdims=True) + p_new

    acc_s[...] = jnp.zeros_like(acc_s)

    def v_body(c, carry):
        g = g0 + nch + c

        @pl.when(g + (DEC_NSLOT - 1) < total)
        def _():
            start_chunk(g + (DEC_NSLOT - 1))

        wait_chunk(g)
        slot = lax.rem(g, DEC_NSLOT)
        for j in range(DEC_G):
            p = c * DEC_G + j
            pp = zbuf[:, pl.ds(pl.multiple_of(p * page, page), page)]
            acc_s[...] += ring[slot, j] * head_bcast(pp)
        return carry

    down = None
    for c in range(n_ffn):
        lax.fori_loop(c * grp, (c + 1) * grp, v_body, 0)
        d = _dot(u_s[:, c * FFN_CHUNK:(c + 1) * FFN_CHUNK], wdn_ref[c * FFN_CHUNK:(c + 1) * FFN_CHUNK, :])
        down = d if down is None else down + d
    y_ref[0] = _rms(h + down, gf_ref[...])

    num = jnp.sum(acc_s[...], axis=1, keepdims=True) + head_bcast(p_new) * jnp.broadcast_to(vnew_ref[0], (hw, page))
    o_ref[0] = (num / head_bcast(l))[:, 0:1]


def _gla_sample_kernel(s_ref, la_ref, k_ref, q_ref, v_ref, gg_ref, gnorm_ref, s_out, go_out, *, n_heads, dk, dv):
    for h in range(n_heads):
        la = la_ref[0, h * dk:(h + 1) * dk, :]
        kk = k_ref[0, h * dk:(h + 1) * dk, :]
        qq = q_ref[0, h * dk:(h + 1) * dk, :]
        vv = v_ref[0, :, h * dv:(h + 1) * dv]
        s_new = s_ref[0, h] * jnp.exp(la) + kk * vv
        s_out[0, h] = s_new
        o = jnp.sum(qq * s_new, axis=0, keepdims=True)
        o_n = _rms(o, gnorm_ref[...])
        go_out[0, :, h * dv:(h + 1) * dv] = o_n * _silu(gg_ref[0, :, h * dv:(h + 1) * dv])


def kernel(x_prompt, x_sample, cache_k, cache_v, cache_logf, state_gla, page_table, norm1_g, w_in, fox_b_f,
           gla_w_gate_up, gla_b_gate, gla_norm_g, w_o, norm2_g, w_up, w_down, final_g):
    B, S, D = x_prompt.shape
    Bd = x_sample.shape[0]
    depth, n_phys, page, H, dh = cache_k.shape
    _, _, Hg, dk, dv = state_gla.shape
    assert depth == 1 and x_sample.shape[1] == 1 and page == LANES
    fox_w = H * dh
    gla_kw = Hg * dk
    gla_vw = Hg * dv
    rank = gla_w_gate_up.shape[1]
    n_pages = page_table.shape[1]

    wt = jnp.transpose(w_in[0])
    o_fq, o_fk = 0, fox_w
    o_ff = 3 * fox_w
    o_gq = o_ff + H
    o_gk = o_gq + gla_kw
    o_gv = o_gk + gla_kw
    o_glr = o_gv + gla_vw
    o_gg = o_glr + rank
    misc_pad = LANES - FF_LANE0 - H
    w_tok = jnp.concatenate([
        wt[o_fk:o_fk + fox_w], wt[o_gq:o_gq + gla_kw], wt[o_gk:o_gk + gla_kw], wt[o_gv:o_gv + gla_vw],
        wt[o_gg:o_gg + gla_vw], wt[o_glr:o_glr + rank], jnp.zeros((FF_LANE0 - rank, D), F32),
        wt[o_ff:o_ff + H], jnp.zeros((misc_pad, D), F32)], axis=0).astype(BF16)
    w_dm = jnp.concatenate([
        wt[o_fq:o_fq + 3 * fox_w], wt[o_ff:o_ff + H], jnp.zeros((2 * SUBLANES - H, D), F32)], axis=0).astype(BF16)
    bf_row = jnp.concatenate([jnp.zeros((1, FF_LANE0), F32), fox_b_f.reshape(1, H), jnp.zeros((1, misc_pad), F32)],
                             axis=1)
    w_gate = jnp.concatenate([gla_w_gate_up[0], jnp.zeros((LANES - rank, gla_kw), F32)], axis=0).astype(BF16)
    wo_bf = w_o[0].astype(BF16)
    wup_bf = w_up[0].astype(BF16)
    wdn_bf = w_down[0].astype(BF16)
    g1 = norm1_g.reshape(1, D)
    g2 = norm2_g.reshape(1, D)
    gf = final_g.reshape(1, D)
    bgate = gla_b_gate.reshape(1, gla_kw)
    gnorm = gla_norm_g.reshape(1, dv)

    cparams = lambda sem: pltpu.CompilerParams(dimension_semantics=sem, vmem_limit_bytes=VMEM_LIMIT_BYTES)
    single = lambda shape: pl.BlockSpec(shape, lambda *_: (0,) * len(shape), pipeline_mode=pl.Buffered(1))

    tm = PROJ_TM
    nt = S // tm
    n_tok_cols = w_tok.shape[0]
    n_dm_rows = w_dm.shape[0]
    qT_p, kT_p, vT_p, lfT_p, cT_p, kaug_p, go_p, sfin_p = pl.pallas_call(
        functools.partial(_proj_prompt_kernel, tm=tm, fox_w=fox_w, fox_dh=dh, gla_kw=gla_kw, gla_vw=gla_vw,
                          n_heads_gla=Hg),
        grid=(B, nt),
        in_specs=[
            pl.BlockSpec((1, tm, D), lambda b, t: (b, t, 0)),
            single((1, D)), single((n_tok_cols, D)), single((n_dm_rows, D)), single((H, 1)), single((1, LANES)),
            single((LANES, gla_kw)), single((1, gla_kw)), single((1, dv)),
        ],
        out_specs=[
            pl.BlockSpec((1, fox_w, tm), lambda b, t: (b, 0, t)),
            pl.BlockSpec((1, fox_w, tm), lambda b, t: (b, 0, t)),
            pl.BlockSpec((1, fox_w, tm), lambda b, t: (b, 0, t)),
            pl.BlockSpec((1, H, tm), lambda b, t: (b, 0, t)),
            pl.BlockSpec((1, H, tm), lambda b, t: (b, 0, t)),
            pl.BlockSpec((1, H, tm, LANES), lambda b, t: (b, 0, t, 0)),
            pl.BlockSpec((1, tm, gla_vw), lambda b, t: (b, t, 0)),
            pl.BlockSpec((1, Hg, dk, dv), lambda b, t: (b, 0, 0, 0)),
        ],
        out_shape=[
            jax.ShapeDtypeStruct((B, fox_w, S), BF16),
            jax.ShapeDtypeStruct((B, fox_w, S), F32),
            jax.ShapeDtypeStruct((B, fox_w, S), F32),
            jax.ShapeDtypeStruct((B, H, S), F32),
            jax.ShapeDtypeStruct((B, H, S), F32),
            jax.ShapeDtypeStruct((B, H, S, LANES), BF16),
            jax.ShapeDtypeStruct((B, S, gla_vw), BF16),
            jax.ShapeDtypeStruct((B, Hg, dk, dv), F32),
        ],
        scratch_shapes=[
            pltpu.VMEM((H, LANES), F32),
            pltpu.VMEM((gla_vw, gla_kw), F32),
            pltpu.VMEM((tm, tm), BF16),
            pltpu.VMEM((tm, tm), BF16),
            pltpu.VMEM((GLA_CHUNK, GLA_CHUNK), BF16),
            pltpu.VMEM((tm, gla_kw), F32), pltpu.VMEM((tm, gla_kw), F32),
            pltpu.VMEM((tm, gla_vw), F32), pltpu.VMEM((tm, gla_vw), F32), pltpu.VMEM((tm, gla_kw), F32),
        ],
        compiler_params=cparams(("arbitrary", "arbitrary")),
        name="proj_gla_prompt",
    )(x_prompt, g1, w_tok, w_dm, fox_b_f.reshape(H, 1), bf_row, w_gate, bgate, gnorm)

    t_blk = ATT_T
    assert t_blk == tm
    nq = S // t_blk
    n_pairs = fox_w // LANES
    c4 = cT_p.reshape(B, n_pairs, 2, S)
    k5 = kaug_p.reshape(B, n_pairs, 2, S, LANES)
    n_vrows = dh + 2 * SUBLANES
    foT_p = pl.pallas_call(
        functools.partial(_attn_prompt_kernel, t_blk=t_blk, dh=dh),
        grid=(B, n_pairs, nq),
        in_specs=[
            pl.BlockSpec((1, LANES, t_blk), lambda b, p, i: (b, p, i)),
            pl.BlockSpec((1, 1, 2, S, LANES), lambda b, p, i: (b, p, 0, 0, 0)),
            pl.BlockSpec((1, LANES, S), lambda b, p, i: (b, p, 0)),
            pl.BlockSpec((1, 1, 2, S), lambda b, p, i: (b, p, 0, 0)),
        ],
        out_specs=pl.BlockSpec((1, LANES, t_blk), lambda b, p, i: (b, p, i)),
        out_shape=jax.ShapeDtypeStruct((B, fox_w, S), BF16),
        scratch_shapes=[
            pltpu.VMEM((2, n_vrows, S), BF16),
            pltpu.VMEM((2, LANES, t_blk), BF16),
            pltpu.VMEM((2, 1, t_blk), F32),
            pltpu.VMEM((2, n_vrows, t_blk), F32),
        ],
        compiler_params=cparams(("arbitrary", "arbitrary", "arbitrary")),
        name="fox_attn_prompt",
    )(qT_p, k5, vT_p, c4)

    xs = x_sample.reshape(Bd, D)
    full = lambda shape: pl.BlockSpec(shape, lambda: (0,) * len(shape))
    s_shapes = [(Bd, fox_w), (Bd, fox_w), (Bd, fox_w), (Bd, H), (Bd, gla_kw), (Bd, gla_kw), (Bd, gla_vw),
                (Bd, gla_vw), (Bd, gla_kw)]
    q_s, k_s, v_s, lf_s, gq_s, gk_s, gv_s, gg_s, la_s = pl.pallas_call(
        functools.partial(_proj_sample_kernel, fox_w=fox_w, fox_dh=dh, gla_kw=gla_kw, gla_vw=gla_vw, n_heads_gla=Hg,
                          n_heads_fox=H),
        in_specs=[full((Bd, D)), full((1, D)), full((n_tok_cols, D)), full((n_dm_rows, D)), full((1, H)),
                  full((LANES, gla_kw)), full((1, gla_kw))],
        out_specs=[full(s) for s in s_shapes],
        out_shape=[jax.ShapeDtypeStruct(s, F32) for s in s_shapes],
        compiler_params=pltpu.CompilerParams(vmem_limit_bytes=VMEM_LIMIT_BYTES),
        name="proj_sample",
    )(xs, g1, w_tok, w_dm, fox_b_f.reshape(1, H), w_gate, bgate)

    kc = jnp.transpose(cache_k[0], (0, 2, 3, 1)).reshape(n_phys, fox_w, page)
    vc = jnp.transpose(cache_v[0], (0, 2, 3, 1)).reshape(n_phys, fox_w, page)
    lfc = jnp.transpose(cache_logf[0], (0, 2, 1))
    col = lambda a: a.reshape(Bd, a.shape[1], 1)
    ftm = FFN_TM
    n_ft = S // ftm
    assert B * n_ft == Bd
    d_ff = wup_bf.shape[1]
    assert (n_pages // DEC_G) % (d_ff // FFN_CHUNK) == 0
    wconst = lambda shape: pl.BlockSpec(shape, lambda b, i, pt: (0, 0), pipeline_mode=pl.Buffered(1))
    seq = lambda b, i, pt: (b * n_ft + i, 0, 0)
    grid_spec = pltpu.PrefetchScalarGridSpec(
        num_scalar_prefetch=1,
        grid=(B, n_ft),
        in_specs=[
            pl.BlockSpec((1, ftm, D), lambda b, i, pt: (b, i, 0)),
            pl.BlockSpec((1, fox_w, ftm), lambda b, i, pt: (b, 0, i)),
            pl.BlockSpec((1, ftm, gla_vw), lambda b, i, pt: (b, i, 0)),
            wconst(wo_bf.shape), wconst(wup_bf.shape), wconst(wdn_bf.shape), wconst(g2.shape), wconst(gf.shape),
            pl.BlockSpec((1, fox_w, 1), seq), pl.BlockSpec((1, fox_w, 1), seq), pl.BlockSpec((1, fox_w, 1), seq),
            pl.BlockSpec((1, H, 1), seq),
            pl.BlockSpec(memory_space=pl.ANY), pl.BlockSpec(memory_space=pl.ANY), pl.BlockSpec(memory_space=pl.ANY),
        ],
        out_specs=[
            pl.BlockSpec((1, ftm, D), lambda b, i, pt: (b, i, 0)),
            pl.BlockSpec((1, fox_w, 1), seq),
        ],
        scratch_shapes=[
            pltpu.VMEM((ftm, d_ff), BF16),
            pltpu.VMEM((DEC_NSLOT, DEC_G, fox_w, page), F32),
            pltpu.VMEM((2, n_pages, H, page), F32),
            pltpu.VMEM((n_pages, H, page), F32),
            pltpu.VMEM((H, n_pages * page), F32),
            pltpu.VMEM((fox_w, page), F32),
            pltpu.VMEM((page, page), BF16),
            pltpu.SemaphoreType.DMA((DEC_NSLOT,)),
            pltpu.SemaphoreType.DMA((2,)),
        ],
    )
    y_p, fo_s = pl.pallas_call(
        functools.partial(_ffn_decode_kernel, fox_w=fox_w, d_ff=d_ff, n_pages=n_pages, n_b=Bd, n_heads=H, dh=dh),
        grid_spec=grid_spec,
        out_shape=[jax.ShapeDtypeStruct((B, S, D), F32), jax.ShapeDtypeStruct((Bd, fox_w, 1), F32)],
        compiler_params=cparams(("arbitrary", "arbitrary")),
        name="ffn_prompt_fox_decode",
    )(page_table, x_prompt, foT_p, go_p, wo_bf, wup_bf, wdn_bf, g2, gf,
      col(q_s), col(k_s), col(v_s), col(lf_s), kc, vc, lfc)
    foT_s = fo_s.reshape(Bd, fox_w).T.astype(BF16)

    s_new, go_s = pl.pallas_call(
        functools.partial(_gla_sample_kernel, n_heads=Hg, dk=dk, dv=dv),
        grid=(Bd,),
        in_specs=[
            pl.BlockSpec((1, Hg, dk, dv), lambda b: (b, 0, 0, 0)),
            pl.BlockSpec((1, gla_kw, 1), lambda b: (b, 0, 0)),
            pl.BlockSpec((1, gla_kw, 1), lambda b: (b, 0, 0)),
            pl.BlockSpec((1, gla_kw, 1), lambda b: (b, 0, 0)),
            pl.BlockSpec((1, 1, gla_vw), lambda b: (b, 0, 0)),
            pl.BlockSpec((1, 1, gla_vw), lambda b: (b, 0, 0)),
            pl.BlockSpec((1, dv), lambda b: (0, 0)),
        ],
        out_specs=[
            pl.BlockSpec((1, Hg, dk, dv), lambda b: (b, 0, 0, 0)),
            pl.BlockSpec((1, 1, gla_vw), lambda b: (b, 0, 0)),
        ],
        out_shape=[jax.ShapeDtypeStruct((Bd, Hg, dk, dv), F32), jax.ShapeDtypeStruct((Bd, 1, gla_vw), F32)],
        compiler_params=cparams(("arbitrary",)),
        name="gla_sample",
    )(state_gla[0], col(la_s), col(gk_s), col(gq_s), gv_s.reshape(Bd, 1, gla_vw), gg_s.reshape(Bd, 1, gla_vw), gnorm)
    go_s = go_s.reshape(Bd, gla_vw).astype(BF16)

    y_s = _ffn_call(xs[None], foT_s[None], go_s[None], wo_bf, wup_bf, wdn_bf, g2, gf, Bd).reshape(Bd, 1, D)

    new_k_p = jnp.transpose(kT_p.reshape(1, B, H, dh, S), (0, 1, 4, 2, 3))
    new_v_p = jnp.transpose(vT_p.reshape(1, B, H, dh, S), (0, 1, 4, 2, 3))
    new_lf_p = jnp.transpose(lfT_p, (0, 2, 1)).reshape(1, B, S, H)
    return (y_p, y_s, new_k_p, new_v_p, new_lf_p, sfin_p.reshape(1, B, Hg, dk, dv),
            k_s.reshape(1, Bd, 1, H, dh), v_s.reshape(1, Bd, 1, H, dh), lf_s.reshape(1, Bd, 1, H),
            s_new.reshape(1, Bd, Hg, dk, dv))
```

```python
import functools

import jax
import jax.numpy as jnp
from jax import lax
from jax.experimental import pallas as pl
from jax.experimental.pallas import tpu as pltpu

F32 = jnp.float32
BF16 = jnp.bfloat16

LANES = 128
SUBLANES = 8
VMEM_LIMIT_BYTES = 56 * 1024 * 1024

EPS = 1e-6
LOG2E = 1.4426950408889634
N_AUG = 3
FF_LANE0 = 16
GLA_GATE_NORM = 16.0
GLA_CHUNK = 128
GLA_SUB = 32
PROJ_TM = 512
ATT_T = 512
ATT_KS = 256
FFN_TM = 512
FFN_CHUNK = 1024
DEC_G = 8
GLA_SAMPLE_BLOCK = 8
DEC_NSLOT = 5


def _dot(a, b):
    return jnp.dot(a, b, preferred_element_type=F32)


def _dot_nt(a, b):
    return lax.dot_general(a, b, (((1,), (1,)), ((), ())), preferred_element_type=F32)


def _split3(x):
    hi = x.astype(BF16).astype(F32)
    r = x - hi
    mid = r.astype(BF16).astype(F32)
    lo = r - mid
    return hi, mid, lo


def _log_sigmoid(x):
    return jnp.minimum(x, 0.0) - jnp.log1p(jnp.exp(-jnp.abs(x)))


def _silu(x):
    return x / (1.0 + jnp.exp(-x))


def _rms(x, g):
    return x * lax.rsqrt(jnp.mean(x * x, axis=-1, keepdims=True) + EPS) * g


def _proj_prompt_kernel(x_ref, g1_ref, wtok_ref, wdm_ref, bf_ref, bfrow_ref, wgate_ref, bgate_ref, gnorm_ref,
                        qT_out, kT_out, vT_out, lfT_out, cT_out, kaug_out, go_out, sfin_out,
                        carry_s, bdt_s, uincl_s, ltm_s, lincl_s, gq_s, gk_s, gv_s, gg_s, la_s,
                        *, tm, fox_w, fox_dh, gla_kw, gla_vw, n_heads_gla):
    t = pl.program_id(1)
    nt = pl.num_programs(1)
    dk = gla_kw // n_heads_gla
    dv = gla_vw // n_heads_gla

    @pl.when(jnp.logical_and(pl.program_id(0) == 0, t == 0))
    def _():
        r = lax.broadcasted_iota(jnp.int32, (tm, tm), 0)
        c = lax.broadcasted_iota(jnp.int32, (tm, tm), 1)
        uincl_s[...] = jnp.where(r <= c, 1.0, 0.0).astype(BF16)
        ltm_s[...] = jnp.where(c <= r, 1.0, 0.0).astype(BF16)
        r = lax.broadcasted_iota(jnp.int32, (GLA_CHUNK, GLA_CHUNK), 0)
        c = lax.broadcasted_iota(jnp.int32, (GLA_CHUNK, GLA_CHUNK), 1)
        lincl_s[...] = jnp.where(c <= r, 1.0, 0.0).astype(BF16)

    @pl.when(t == 0)
    def _():
        carry_s[...] = jnp.zeros_like(carry_s)
        bdt_s[...] = jnp.zeros_like(bdt_s)

    x = x_ref[0]
    xn = _rms(x, g1_ref[...]).astype(BF16)

    z = _dot_nt(xn, wtok_ref[...])
    kz = z[:, 0:fox_w]
    o0 = fox_w
    gq_s[...] = z[:, o0:o0 + gla_kw] * (dk ** -0.5)
    o0 += gla_kw
    gk_s[...] = z[:, o0:o0 + gla_kw]
    o0 += gla_kw
    gv_s[...] = z[:, o0:o0 + gla_vw]
    o0 += gla_vw
    gg_s[...] = z[:, o0:o0 + gla_vw]
    o0 += gla_vw
    misc = z[:, o0:o0 + LANES]
    pre = _dot(misc.astype(BF16), wgate_ref[...]) + bgate_ref[...]
    la_s[...] = _log_sigmoid(pre) * (1.0 / GLA_GATE_NORM)

    lf_tok = _log_sigmoid(misc + bfrow_ref[...])
    st3 = jnp.concatenate(_split3(lf_tok), axis=1).astype(BF16)
    cc = _dot(ltm_s[...], st3)
    cs_tok = cc[:, 0:LANES] + cc[:, LANES:2 * LANES] + cc[:, 2 * LANES:3 * LANES]
    d_tok = (cs_tok - cs_tok[0:1, :]) * LOG2E
    lane_k = lax.broadcasted_iota(jnp.int32, (tm, LANES), 1)
    for h in range(fox_w // fox_dh):
        own = (h % 2) * fox_dh
        spare = (1 - h % 2) * fox_dh
        parts = _split3(jnp.broadcast_to(d_tok[:, FF_LANE0 + h:FF_LANE0 + h + 1], (tm, LANES)))
        aug = jnp.zeros((tm, LANES), F32)
        for n, part in enumerate(parts):
            aug = jnp.where(lane_k == spare + n, -part, aug)
        own_l = jnp.logical_and(lane_k >= own, lane_k < own + fox_dh)
        kaug_out[0, h] = jnp.where(own_l, kz[:, (h // 2) * LANES:(h // 2 + 1) * LANES], aug).astype(BF16)

    zt = _dot_nt(wdm_ref[...], xn)
    qT_out[0] = (zt[0:fox_w] * (fox_dh ** -0.5 * LOG2E)).astype(BF16)
    kT_out[0] = zt[fox_w:2 * fox_w]
    vT_out[0] = zt[2 * fox_w:3 * fox_w]
    lf = _log_sigmoid(zt[3 * fox_w:3 * fox_w + SUBLANES] + bf_ref[...])
    lfT_out[0] = lf
    hi, mid, lo = _split3(lf)
    stack = jnp.concatenate([hi, mid, lo, jnp.zeros_like(hi)], axis=0).astype(BF16)
    cs = _dot(stack, uincl_s[...])
    cs = cs[0:8] + cs[8:16] + cs[16:24]
    carry = carry_s[...]
    cT_out[0] = cs + carry[:, 0:1]
    tot = _dot(stack, jnp.ones((tm, LANES), BF16))
    carry_s[...] = carry + tot[0:8] + tot[8:16] + tot[16:24]

    nsub = GLA_CHUNK // GLA_SUB
    rowi = lax.broadcasted_iota(jnp.int32, (GLA_CHUNK, gla_kw), 0)
    lanei = lax.broadcasted_iota(jnp.int32, (GLA_CHUNK, gla_kw), 1)
    ar = lax.broadcasted_iota(jnp.int32, (GLA_CHUNK, GLA_CHUNK), 0)
    ac = lax.broadcasted_iota(jnp.int32, (GLA_CHUNK, GLA_CHUNK), 1)
    tri_blk = jnp.logical_and(ar // GLA_SUB == ac // GLA_SUB, ar >= ac)
    br = lax.broadcasted_iota(jnp.int32, (gla_vw, gla_kw), 0)
    bc = lax.broadcasted_iota(jnp.int32, (gla_vw, gla_kw), 1)
    bd_mask = (br // dv) == (bc // dk)

    def chunk_body(ci, _):
        r0 = ci * GLA_CHUNK
        la_c = la_s[pl.ds(r0, GLA_CHUNK), :]
        gq_c = gq_s[pl.ds(r0, GLA_CHUNK), :]
        gk_c = gk_s[pl.ds(r0, GLA_CHUNK), :]
        gv_c = gv_s[pl.ds(r0, GLA_CHUNK), :]
        gg_c = gg_s[pl.ds(r0, GLA_CHUNK), :]
        h3, m3, l3 = _split3(la_c)
        st = jnp.concatenate([h3, m3, l3], axis=1).astype(BF16)
        bb = _dot(lincl_s[...], st)
        b = bb[:, 0:gla_kw] + bb[:, gla_kw:2 * gla_kw] + bb[:, 2 * gla_kw:3 * gla_kw]
        bmid_l, bend_l, b0_l = [], [], []
        for i in range(nsub):
            s0 = i * GLA_SUB
            bmid_l.append(b[s0 + GLA_SUB // 2:s0 + GLA_SUB // 2 + 1])
            bend_l.append(b[s0 + GLA_SUB - 1:s0 + GLA_SUB])
            b0_l.append(jnp.zeros((1, gla_kw), F32) if i == 0 else b[s0 - 1:s0])
        bc_rows = lambda rows: jnp.concatenate(
            [jnp.broadcast_to(r, (GLA_SUB, gla_kw)) for r in rows], axis=0)
        bmid, bend, b0 = bc_rows(bmid_l), bc_rows(bend_l), bc_rows(b0_l)
        qt = (gq_c * jnp.exp(b - bmid)).astype(BF16)
        kt = (gk_c * jnp.exp(bmid - b)).astype(BF16)
        qp = gq_c * jnp.exp(b - b0)
        kd = gk_c * jnp.exp(bend - b)
        gv_bf = gv_c.astype(BF16)
        gvT_bf = gv_c.T.astype(BF16)

        rms = [jnp.logical_and(rowi >= i * GLA_SUB, rowi < (i + 1) * GLA_SUB) for i in range(nsub)]
        uts = [_dot(gvT_bf, jnp.where(rms[i], kd, 0.0).astype(BF16)) for i in range(nsub)]
        states = [bdt_s[...]]
        for i in range(nsub):
            decay = jnp.exp(bend_l[i] - b0_l[i])
            states.append(states[i] * decay + jnp.where(bd_mask, uts[i], 0.0))
        bdt_s[...] = states[nsub]
        o_inter = None
        for i in range(nsub):
            d = _dot_nt(jnp.where(rms[i], qp, 0.0).astype(BF16), states[i].astype(BF16))
            o_inter = d if o_inter is None else o_inter + d

        for h in range(n_heads_gla):
            hm = jnp.logical_and(lanei >= h * dk, lanei < (h + 1) * dk)
            a = _dot_nt(jnp.where(hm, qt, jnp.zeros_like(qt)), kt)
            a = jnp.where(tri_blk, a, 0.0).astype(BF16)
            o_h = _dot(a, gv_bf[:, h * dv:(h + 1) * dv]) + o_inter[:, h * dv:(h + 1) * dv]
            o_n = _rms(o_h, gnorm_ref[...])
            go = o_n * _silu(gg_c[:, h * dv:(h + 1) * dv])
            go_out[0, pl.ds(r0, GLA_CHUNK), h * dv:(h + 1) * dv] = go.astype(BF16)
        return 0

    for ci in range(tm // GLA_CHUNK):
        chunk_body(ci, 0)

    @pl.when(t == nt - 1)
    def _():
        bd = bdt_s[...].T
        for h in range(n_heads_gla):
            sfin_out[0, h] = bd[h * dk:(h + 1) * dk, h * dv:(h + 1) * dv]


def _attn_prompt_kernel(qT_ref, kaug_ref, vT_ref, c_ref, oT_ref, vaug_s, qa_s, m_s, acc_s, *, t_blk, dh):
    i = pl.program_id(2)
    hw2 = 2 * dh

    @pl.when(i == 0)
    def _():
        for h in range(2):
            vaug_s[h, 0:dh, :] = vT_ref[0, h * dh:(h + 1) * dh, :].astype(BF16)
            vaug_s[h, dh:, :] = jnp.ones((vaug_s.shape[1] - dh, vaug_s.shape[2]), BF16)

    qT = qT_ref[0]
    rowq = lax.broadcasted_iota(jnp.int32, (hw2, t_blk), 0)
    keyi = lax.broadcasted_iota(jnp.int32, (ATT_KS, t_blk), 0)
    qryi = lax.broadcasted_iota(jnp.int32, (ATT_KS, t_blk), 1)
    t0 = pl.multiple_of(i * t_blk, t_blk)
    c_q = []
    for h in range(2):
        spare = (1 - h) * dh
        own_r = jnp.logical_and(rowq >= h * dh, rowq < (h + 1) * dh)
        ones_r = jnp.logical_and(rowq >= spare, rowq < spare + N_AUG)
        qa_s[h] = jnp.where(own_r, qT, jnp.where(ones_r, 1.0, 0.0).astype(BF16))
        m_s[h] = jnp.full(m_s.shape[1:], -jnp.inf, F32)
        acc_s[h] = jnp.zeros(acc_s.shape[1:], F32)
        c_q.append(c_ref[0, 0, h:h + 1, pl.ds(t0, LANES)][:, 0:1])

    n_sub = t_blk // ATT_KS

    def steps(tiles):
        work = []
        for j, masked in tiles:
            k0 = pl.multiple_of(j * t_blk, t_blk)
            kks = [pl.multiple_of(k0 + ks * ATT_KS, ATT_KS) for ks in range(n_sub)]
            s_all = [[_dot(kaug_ref[0, 0, h, pl.ds(kks[ks], ATT_KS), :], qa_s[h]) for ks in range(n_sub)]
                     for h in range(2)]
            work.append((k0, kks, s_all, masked))
        for h in range(2):
            m_run = m_s[h]
            acc = acc_s[h]
            for k0, kks, s_all, masked in work:
                off = (c_ref[0, 0, h:h + 1, pl.ds(k0, LANES)][:, 0:1] - c_q[h]) * LOG2E
                for ks in range(n_sub):
                    s = s_all[h][ks]
                    if masked:
                        s = jnp.where(keyi + ks * ATT_KS <= qryi, s, -jnp.inf)
                    m_new = jnp.maximum(m_run, jnp.max(s, axis=0, keepdims=True) - off)
                    p = jnp.exp2(s - (m_new + off))
                    alpha = jnp.exp2(m_run - m_new)
                    pv = _dot(vaug_s[h, :, pl.ds(kks[ks], ATT_KS)], p.astype(BF16))
                    acc = alpha * acc + pv
                    m_run = m_new
            acc_s[h] = acc
            m_s[h] = m_run

    def pair_body(jj, carry):
        steps([(2 * jj, False), (2 * jj + 1, False)])
        return carry

    lax.fori_loop(0, lax.shift_right_logical(i, 1), pair_body, 0)
    odd = jnp.bitwise_and(i, 1) == 1

    @pl.when(odd)
    def _():
        steps([(i - 1, False), (i, True)])

    @pl.when(jnp.logical_not(odd))
    def _():
        steps([(i, True)])

    outs = [acc_s[h][0:dh, :] / acc_s[h][dh:dh + 1, :] for h in range(2)]
    oT_ref[0] = jnp.concatenate(outs, axis=0).astype(BF16)


def _ffn_kernel(x_ref, foT_ref, go_ref, wo_ref, wup_ref, wdn_ref, g2_ref, gf_ref, y_ref, u_s, *, fox_w, d_ff):
    x = x_ref[0]
    h = x + (_dot(foT_ref[0].T, wo_ref[0:fox_w, :]) + _dot(go_ref[0], wo_ref[fox_w:, :]))
    hn = _rms(h, g2_ref[...]).astype(BF16)
    for c in range(d_ff // FFN_CHUNK):
        u = _dot(hn, wup_ref[:, c * FFN_CHUNK:(c + 1) * FFN_CHUNK])
        u_s[:, c * FFN_CHUNK:(c + 1) * FFN_CHUNK] = jnp.square(jnp.maximum(u, 0.0)).astype(BF16)
    y_ref[0] = _rms(h + _dot(u_s[...], wdn_ref[...]), gf_ref[...])


def _ffn_call(x3, foT, go, wo, wup, wdn, g2, gf, tm):
    nb, n, d = x3.shape
    fox_w = foT.shape[1]
    d_ff = wup.shape[1]
    const = lambda shape: pl.BlockSpec(shape, lambda b, i: (0, 0), pipeline_mode=pl.Buffered(1))
    return pl.pallas_call(
        functools.partial(_ffn_kernel, fox_w=fox_w, d_ff=d_ff),
        grid=(nb, n // tm),
        in_specs=[
            pl.BlockSpec((1, tm, d), lambda b, i: (b, i, 0)),
            pl.BlockSpec((1, fox_w, tm), lambda b, i: (b, 0, i)),
            pl.BlockSpec((1, tm, go.shape[2]), lambda b, i: (b, i, 0)),
            const(wo.shape), const(wup.shape), const(wdn.shape), const(g2.shape), const(gf.shape),
        ],
        out_specs=pl.BlockSpec((1, tm, d), lambda b, i: (b, i, 0)),
        out_shape=jax.ShapeDtypeStruct((nb, n, d), F32),
        scratch_shapes=[pltpu.VMEM((tm, d_ff), BF16)],
        compiler_params=pltpu.CompilerParams(dimension_semantics=("arbitrary", "arbitrary"),
                                             vmem_limit_bytes=VMEM_LIMIT_BYTES),
        name="merge_ffn",
    )(x3, foT, go, wo, wup, wdn, g2, gf)


def _proj_sample_kernel(x_ref, g1_ref, wtok_ref, wdm_ref, bf_ref, wgate_ref, bgate_ref,
                        q_out, k_out, v_out, lf_out, gq_out, gk_out, gv_out, gg_out, la_out,
                        *, fox_w, fox_dh, gla_kw, gla_vw, n_heads_gla, n_heads_fox):
    dk = gla_kw // n_heads_gla
    xn = _rms(x_ref[...], g1_ref[...]).astype(BF16)
    z = _dot_nt(xn, wtok_ref[...])
    o0 = fox_w
    gq_out[...] = z[:, o0:o0 + gla_kw] * (dk ** -0.5)
    o0 += gla_kw
    gk_out[...] = z[:, o0:o0 + gla_kw]
    o0 += gla_kw
    gv_out[...] = z[:, o0:o0 + gla_vw]
    o0 += gla_vw
    gg_out[...] = z[:, o0:o0 + gla_vw]
    o0 += gla_vw
    glr = z[:, o0:o0 + LANES].astype(BF16)
    la_out[...] = _log_sigmoid(_dot(glr, wgate_ref[...]) + bgate_ref[...]) * (1.0 / GLA_GATE_NORM)
    z2 = _dot_nt(xn, wdm_ref[...])
    q_out[...] = z2[:, 0:fox_w] * (fox_dh ** -0.5)
    k_out[...] = z2[:, fox_w:2 * fox_w]
    v_out[...] = z2[:, 2 * fox_w:3 * fox_w]
    lf_out[...] = _log_sigmoid(z2[:, 3 * fox_w:3 * fox_w + n_heads_fox] + bf_ref[...])


def _ffn_decode_kernel(pt_ref, x_ref, foT_ref, go_ref, wo_ref, wup_ref, wdn_ref, g2_ref, gf_ref,
                       q_ref, knew_ref, vnew_ref, lfnew_ref, kc_hbm, vc_hbm, lfc_hbm,
                       y_ref, o_ref,
                       u_s, ring, lfbuf, rev_s, zbuf, acc_s, qb_s, ustrict_s, sem_ring, sem_lf,
                       *, fox_w, d_ff, n_pages, n_b, n_heads, dh):
    b = pl.program_id(0) * pl.num_programs(1) + pl.program_id(1)
    page = LANES
    hw = n_heads * dh
    nch = n_pages // DEC_G
    per_b = 2 * nch
    total = n_b * per_b

    def start_chunk(g):
        bg = g // per_b
        c = g - bg * per_b
        slot = lax.rem(g, DEC_NSLOT)

        @pl.when(c < nch)
        def _():
            for j in range(DEC_G):
                p = n_pages - 1 - (c * DEC_G + j)
                pltpu.make_async_copy(kc_hbm.at[pt_ref[bg, p]], ring.at[slot, j], sem_ring.at[slot]).start()

        @pl.when(c >= nch)
        def _():
            for j in range(DEC_G):
                p = (c - nch) * DEC_G + j
                pltpu.make_async_copy(vc_hbm.at[pt_ref[bg, p]], ring.at[slot, j], sem_ring.at[slot]).start()

    def wait_chunk(g):
        slot = lax.rem(g, DEC_NSLOT)
        for j in range(DEC_G):
            pltpu.make_async_copy(kc_hbm.at[0], ring.at[slot, j], sem_ring.at[slot]).wait()

    def start_lf(bb):
        sl = lax.rem(bb, 2)

        def body(p, _):
            pltpu.make_async_copy(lfc_hbm.at[pt_ref[bb, p]], lfbuf.at[sl, p], sem_lf.at[sl]).start()
            return 0

        lax.fori_loop(0, n_pages, body, 0)

    def wait_lf(bb):
        sl = lax.rem(bb, 2)

        def body(p, _):
            pltpu.make_async_copy(lfc_hbm.at[0], lfbuf.at[sl, p], sem_lf.at[sl]).wait()
            return 0

        lax.fori_loop(0, n_pages, body, 0)

    g0 = b * per_b

    @pl.when(b == 0)
    def _():
        r = lax.broadcasted_iota(jnp.int32, (page, page), 0)
        c = lax.broadcasted_iota(jnp.int32, (page, page), 1)
        ustrict_s[...] = jnp.where(r > c, 1.0, 0.0).astype(BF16)
        start_lf(b)
        for g in range(DEC_NSLOT - 1):
            start_chunk(g0 + g)

    wait_lf(b)

    @pl.when(b + 1 < n_b)
    def _():
        start_lf(b + 1)

    sl = lax.rem(b, 2)
    lf2d = lfbuf[sl].reshape(n_pages * n_heads, page)
    hi, mid, lo = _split3(lf2d)
    u = ustrict_s[...]
    rev = _dot(hi.astype(BF16), u) + _dot(mid.astype(BF16), u) + _dot(lo.astype(BF16), u)
    rev_s[...] = rev.reshape(n_pages, n_heads, page)

    qb = jnp.broadcast_to(q_ref[0], (hw, page))
    qb_s[...] = qb

    def head_sum(x):
        return jnp.sum(x.reshape(n_heads, dh, page), axis=1)

    def head_bcast(x):
        return jnp.broadcast_to(x[:, None, :], (n_heads, dh, page)).reshape(hw, page)

    def k_body(c, carry):
        g = g0 + c

        @pl.when(g + (DEC_NSLOT - 1) < total)
        def _():
            start_chunk(g + (DEC_NSLOT - 1))

        wait_chunk(g)
        slot = lax.rem(g, DEC_NSLOT)
        rows = []
        for hd in range(n_heads):
            hs = slice(hd * dh, (hd + 1) * dh)
            qh = qb_s[hs, :]
            ch = carry[hd:hd + 1]
            for j in range(DEC_G):
                p = n_pages - 1 - (c * DEC_G + j)
                s = jnp.sum(ring[slot, j, hs, :] * qh, axis=0, keepdims=True)
                revp = rev_s[p, hd:hd + 1, :]
                zbuf[hd:hd + 1, pl.ds(pl.multiple_of(p * page, page), page)] = s + revp + ch
                ch = ch + jnp.broadcast_to(revp[:, 0:1] + lfbuf[sl, p, hd:hd + 1, 0:1], (1, page))
            rows.append(ch)
        return jnp.concatenate(rows, axis=0)

    x = x_ref[0]
    h = x + (_dot(foT_ref[0].T, wo_ref[0:fox_w, :]) + _dot(go_ref[0], wo_ref[fox_w:, :]))
    hn = _rms(h, g2_ref[...]).astype(BF16)

    n_ffn = d_ff // FFN_CHUNK
    grp = nch // n_ffn
    carry = jnp.broadcast_to(lfnew_ref[0], (n_heads, page))
    for c in range(n_ffn):
        carry = lax.fori_loop(c * grp, (c + 1) * grp, k_body, carry)
        u = _dot(hn, wup_ref[:, c * FFN_CHUNK:(c + 1) * FFN_CHUNK])
        u_s[:, c * FFN_CHUNK:(c + 1) * FFN_CHUNK] = jnp.square(jnp.maximum(u, 0.0)).astype(BF16)

    z_all = zbuf[...]
    z_new = head_sum(jnp.broadcast_to(knew_ref[0], (hw, page)) * qb)
    m = jnp.maximum(jnp.max(z_all, axis=1, keepdims=True), z_new[:, 0:1])
    p_all = jnp.exp(z_all - m)
    zbuf[...] = p_all
    p_new = jnp.exp(z_new - m)
    l = jnp.sum(p_all, axis=1, keepdims=True) + p_new

    acc_s[...] = jnp.zeros_like(acc_s)

    def v_body(c, carry):
        g = g0 + nch + c

        @pl.when(g + (DEC_NSLOT - 1) < total)
        def _():
            start_chunk(g + (DEC_NSLOT - 1))

        wait_chunk(g)
        slot = lax.rem(g, DEC_NSLOT)
        for hd in range(n_heads):
            hs = slice(hd * dh, (hd + 1) * dh)
            acc = acc_s[hs, :]
            for j in range(DEC_G):
                p = c * DEC_G + j
                pp = zbuf[hd:hd + 1, pl.ds(pl.multiple_of(p * page, page), page)]
                acc = acc + ring[slot, j, hs, :] * jnp.broadcast_to(pp, (dh, page))
            acc_s[hs, :] = acc
        return carry

    down = None
    for c in range(n_ffn):
        lax.fori_loop(c * grp, (c + 1) * grp, v_body, 0)
        d = _dot(u_s[:, c * FFN_CHUNK:(c + 1) * FFN_CHUNK], wdn_ref[c * FFN_CHUNK:(c + 1) * FFN_CHUNK, :])
        down = d if down is None else down + d
    y_ref[0] = _rms(h + down, gf_ref[...])

    num = jnp.sum(acc_s[...], axis=1, keepdims=True) + head_bcast(p_new) * jnp.broadcast_to(vnew_ref[0], (hw, page))
    o_ref[0] = (num / head_bcast(l))[:, 0:1]


def _gla_sample_kernel(s_ref, la_ref, k_ref, q_ref, v_ref, gg_ref, gnorm_ref, s_out, go_out, *, n_heads, dk, dv):
    for i in range(s_ref.shape[0]):
        for h in range(n_heads):
            la = la_ref[i, h * dk:(h + 1) * dk, :]
            kk = k_ref[i, h * dk:(h + 1) * dk, :]
            qq = q_ref[i, h * dk:(h + 1) * dk, :]
            vv = v_ref[i, :, h * dv:(h + 1) * dv]
            s_new = s_ref[i, h] * jnp.exp(la) + kk * vv
            s_out[i, h] = s_new
            o = jnp.sum(qq * s_new, axis=0, keepdims=True)
            o_n = _rms(o, gnorm_ref[...])
            go_out[i, :, h * dv:(h + 1) * dv] = o_n * _silu(gg_ref[i, :, h * dv:(h + 1) * dv])


def kernel(x_prompt, x_sample, cache_k, cache_v, cache_logf, state_gla, page_table, norm1_g, w_in, fox_b_f,
           gla_w_gate_up, gla_b_gate, gla_norm_g, w_o, norm2_g, w_up, w_down, final_g):
    B, S, D = x_prompt.shape
    Bd = x_sample.shape[0]
    depth, n_phys, page, H, dh = cache_k.shape
    _, _, Hg, dk, dv = state_gla.shape
    assert depth == 1 and x_sample.shape[1] == 1 and page == LANES
    fox_w = H * dh
    gla_kw = Hg * dk
    gla_vw = Hg * dv
    rank = gla_w_gate_up.shape[1]
    n_pages = page_table.shape[1]

    wt = jnp.transpose(w_in[0])
    o_fq, o_fk = 0, fox_w
    o_ff = 3 * fox_w
    o_gq = o_ff + H
    o_gk = o_gq + gla_kw
    o_gv = o_gk + gla_kw
    o_glr = o_gv + gla_vw
    o_gg = o_glr + rank
    misc_pad = LANES - FF_LANE0 - H
    w_tok = jnp.concatenate([
        wt[o_fk:o_fk + fox_w], wt[o_gq:o_gq + gla_kw], wt[o_gk:o_gk + gla_kw], wt[o_gv:o_gv + gla_vw],
        wt[o_gg:o_gg + gla_vw], wt[o_glr:o_glr + rank], jnp.zeros((FF_LANE0 - rank, D), F32),
        wt[o_ff:o_ff + H], jnp.zeros((misc_pad, D), F32)], axis=0).astype(BF16)
    w_dm = jnp.concatenate([
        wt[o_fq:o_fq + 3 * fox_w], wt[o_ff:o_ff + H], jnp.zeros((2 * SUBLANES - H, D), F32)], axis=0).astype(BF16)
    bf_row = jnp.concatenate([jnp.zeros((1, FF_LANE0), F32), fox_b_f.reshape(1, H), jnp.zeros((1, misc_pad), F32)],
                             axis=1)
    w_gate = jnp.concatenate([gla_w_gate_up[0], jnp.zeros((LANES - rank, gla_kw), F32)], axis=0).astype(BF16)
    wo_bf = w_o[0].astype(BF16)
    wup_bf = w_up[0].astype(BF16)
    wdn_bf = w_down[0].astype(BF16)
    g1 = norm1_g.reshape(1, D)
    g2 = norm2_g.reshape(1, D)
    gf = final_g.reshape(1, D)
    bgate = gla_b_gate.reshape(1, gla_kw)
    gnorm = gla_norm_g.reshape(1, dv)

    cparams = lambda sem: pltpu.CompilerParams(dimension_semantics=sem, vmem_limit_bytes=VMEM_LIMIT_BYTES)
    single = lambda shape: pl.BlockSpec(shape, lambda *_: (0,) * len(shape), pipeline_mode=pl.Buffered(1))

    tm = PROJ_TM
    nt = S // tm
    n_tok_cols = w_tok.shape[0]
    n_dm_rows = w_dm.shape[0]
    qT_p, kT_p, vT_p, lfT_p, cT_p, kaug_p, go_p, sfin_p = pl.pallas_call(
        functools.partial(_proj_prompt_kernel, tm=tm, fox_w=fox_w, fox_dh=dh, gla_kw=gla_kw, gla_vw=gla_vw,
                          n_heads_gla=Hg),
        grid=(B, nt),
        in_specs=[
            pl.BlockSpec((1, tm, D), lambda b, t: (b, t, 0)),
            single((1, D)), single((n_tok_cols, D)), single((n_dm_rows, D)), single((H, 1)), single((1, LANES)),
            single((LANES, gla_kw)), single((1, gla_kw)), single((1, dv)),
        ],
        out_specs=[
            pl.BlockSpec((1, fox_w, tm), lambda b, t: (b, 0, t)),
            pl.BlockSpec((1, fox_w, tm), lambda b, t: (b, 0, t)),
            pl.BlockSpec((1, fox_w, tm), lambda b, t: (b, 0, t)),
            pl.BlockSpec((1, H, tm), lambda b, t: (b, 0, t)),
            pl.BlockSpec((1, H, tm), lambda b, t: (b, 0, t)),
            pl.BlockSpec((1, H, tm, LANES), lambda b, t: (b, 0, t, 0)),
            pl.BlockSpec((1, tm, gla_vw), lambda b, t: (b, t, 0)),
            pl.BlockSpec((1, Hg, dk, dv), lambda b, t: (b, 0, 0, 0)),
        ],
        out_shape=[
            jax.ShapeDtypeStruct((B, fox_w, S), BF16),
            jax.ShapeDtypeStruct((B, fox_w, S), F32),
            jax.ShapeDtypeStruct((B, fox_w, S), F32),
            jax.ShapeDtypeStruct((B, H, S), F32),
            jax.ShapeDtypeStruct((B, H, S), F32),
            jax.ShapeDtypeStruct((B, H, S, LANES), BF16),
            jax.ShapeDtypeStruct((B, S, gla_vw), BF16),
            jax.ShapeDtypeStruct((B, Hg, dk, dv), F32),
        ],
        scratch_shapes=[
            pltpu.VMEM((H, LANES), F32),
            pltpu.VMEM((gla_vw, gla_kw), F32),
            pltpu.VMEM((tm, tm), BF16),
            pltpu.VMEM((tm, tm), BF16),
            pltpu.VMEM((GLA_CHUNK, GLA_CHUNK), BF16),
            pltpu.VMEM((tm, gla_kw), F32), pltpu.VMEM((tm, gla_kw), F32),
            pltpu.VMEM((tm, gla_vw), F32), pltpu.VMEM((tm, gla_vw), F32), pltpu.VMEM((tm, gla_kw), F32),
        ],
        compiler_params=cparams(("arbitrary", "arbitrary")),
        name="proj_gla_prompt",
    )(x_prompt, g1, w_tok, w_dm, fox_b_f.reshape(H, 1), bf_row, w_gate, bgate, gnorm)

    t_blk = ATT_T
    assert t_blk == tm
    nq = S // t_blk
    n_pairs = fox_w // LANES
    c4 = cT_p.reshape(B, n_pairs, 2, S)
    k5 = kaug_p.reshape(B, n_pairs, 2, S, LANES)
    n_vrows = dh + 2 * SUBLANES
    foT_p = pl.pallas_call(
        functools.partial(_attn_prompt_kernel, t_blk=t_blk, dh=dh),
        grid=(B, n_pairs, nq),
        in_specs=[
            pl.BlockSpec((1, LANES, t_blk), lambda b, p, i: (b, p, i)),
            pl.BlockSpec((1, 1, 2, S, LANES), lambda b, p, i: (b, p, 0, 0, 0)),
            pl.BlockSpec((1, LANES, S), lambda b, p, i: (b, p, 0)),
            pl.BlockSpec((1, 1, 2, S), lambda b, p, i: (b, p, 0, 0)),
        ],
        out_specs=pl.BlockSpec((1, LANES, t_blk), lambda b, p, i: (b, p, i)),
        out_shape=jax.ShapeDtypeStruct((B, fox_w, S), BF16),
        scratch_shapes=[
            pltpu.VMEM((2, n_vrows, S), BF16),
            pltpu.VMEM((2, LANES, t_blk), BF16),
            pltpu.VMEM((2, 1, t_blk), F32),
            pltpu.VMEM((2, n_vrows, t_blk), F32),
        ],
        compiler_params=cparams(("arbitrary", "arbitrary", "arbitrary")),
        name="fox_attn_prompt",
    )(qT_p, k5, vT_p, c4)

    xs = x_sample.reshape(Bd, D)
    full = lambda shape: pl.BlockSpec(shape, lambda: (0,) * len(shape))
    s_shapes = [(Bd, fox_w), (Bd, fox_w), (Bd, fox_w), (Bd, H), (Bd, gla_kw), (Bd, gla_kw), (Bd, gla_vw),
                (Bd, gla_vw), (Bd, gla_kw)]
    q_s, k_s, v_s, lf_s, gq_s, gk_s, gv_s, gg_s, la_s = pl.pallas_call(
        functools.partial(_proj_sample_kernel, fox_w=fox_w, fox_dh=dh, gla_kw=gla_kw, gla_vw=gla_vw, n_heads_gla=Hg,
                          n_heads_fox=H),
        in_specs=[full((Bd, D)), full((1, D)), full((n_tok_cols, D)), full((n_dm_rows, D)), full((1, H)),
                  full((LANES, gla_kw)), full((1, gla_kw))],
        out_specs=[full(s) for s in s_shapes],
        out_shape=[jax.ShapeDtypeStruct(s, F32) for s in s_shapes],
        compiler_params=pltpu.CompilerParams(vmem_limit_bytes=VMEM_LIMIT_BYTES),
        name="proj_sample",
    )(xs, g1, w_tok, w_dm, fox_b_f.reshape(1, H), w_gate, bgate)

    kc = jnp.transpose(cache_k[0], (0, 2, 3, 1)).reshape(n_phys, fox_w, page)
    vc = jnp.transpose(cache_v[0], (0, 2, 3, 1)).reshape(n_phys, fox_w, page)
    lfc = jnp.transpose(cache_logf[0], (0, 2, 1))
    col = lambda a: a.reshape(Bd, a.shape[1], 1)
    ftm = FFN_TM
    n_ft = S // ftm
    assert B * n_ft == Bd
    d_ff = wup_bf.shape[1]
    assert (n_pages // DEC_G) % (d_ff // FFN_CHUNK) == 0
    wconst = lambda shape: pl.BlockSpec(shape, lambda b, i, pt: (0, 0), pipeline_mode=pl.Buffered(1))
    seq = lambda b, i, pt: (b * n_ft + i, 0, 0)
    grid_spec = pltpu.PrefetchScalarGridSpec(
        num_scalar_prefetch=1,
        grid=(B, n_ft),
        in_specs=[
            pl.BlockSpec((1, ftm, D), lambda b, i, pt: (b, i, 0)),
            pl.BlockSpec((1, fox_w, ftm), lambda b, i, pt: (b, 0, i)),
            pl.BlockSpec((1, ftm, gla_vw), lambda b, i, pt: (b, i, 0)),
            wconst(wo_bf.shape), wconst(wup_bf.shape), wconst(wdn_bf.shape), wconst(g2.shape), wconst(gf.shape),
            pl.BlockSpec((1, fox_w, 1), seq), pl.BlockSpec((1, fox_w, 1), seq), pl.BlockSpec((1, fox_w, 1), seq),
            pl.BlockSpec((1, H, 1), seq),
            pl.BlockSpec(memory_space=pl.ANY), pl.BlockSpec(memory_space=pl.ANY), pl.BlockSpec(memory_space=pl.ANY),
        ],
        out_specs=[
            pl.BlockSpec((1, ftm, D), lambda b, i, pt: (b, i, 0)),
            pl.BlockSpec((1, fox_w, 1), seq),
        ],
        scratch_shapes=[
            pltpu.VMEM((ftm, d_ff), BF16),
            pltpu.VMEM((DEC_NSLOT, DEC_G, fox_w, page), F32),
            pltpu.VMEM((2, n_pages, H, page), F32),
            pltpu.VMEM((n_pages, H, page), F32),
            pltpu.VMEM((H, n_pages * page), F32),
            pltpu.VMEM((fox_w, page), F32),
            pltpu.VMEM((fox_w, page), F32),
            pltpu.VMEM((page, page), BF16),
            pltpu.SemaphoreType.DMA((DEC_NSLOT,)),
            pltpu.SemaphoreType.DMA((2,)),
        ],
    )
    y_p, fo_s = pl.pallas_call(
        functools.partial(_ffn_decode_kernel, fox_w=fox_w, d_ff=d_ff, n_pages=n_pages, n_b=Bd, n_heads=H, dh=dh),
        grid_spec=grid_spec,
        out_shape=[jax.ShapeDtypeStruct((B, S, D), F32), jax.ShapeDtypeStruct((Bd, fox_w, 1), F32)],
        compiler_params=cparams(("arbitrary", "arbitrary")),
        name="ffn_prompt_fox_decode",
    )(page_table, x_prompt, foT_p, go_p, wo_bf, wup_bf, wdn_bf, g2, gf,
      col(q_s), col(k_s), col(v_s), col(lf_s), kc, vc, lfc)
    foT_s = fo_s.reshape(Bd, fox_w).T.astype(BF16)

    gb = GLA_SAMPLE_BLOCK
    s_new, go_s = pl.pallas_call(
        functools.partial(_gla_sample_kernel, n_heads=Hg, dk=dk, dv=dv),
        grid=(Bd // gb,),
        in_specs=[
            pl.BlockSpec((gb, Hg, dk, dv), lambda b: (b, 0, 0, 0)),
            pl.BlockSpec((gb, gla_kw, 1), lambda b: (b, 0, 0)),
            pl.BlockSpec((gb, gla_kw, 1), lambda b: (b, 0, 0)),
            pl.BlockSpec((gb, gla_kw, 1), lambda b: (b, 0, 0)),
            pl.BlockSpec((gb, 1, gla_vw), lambda b: (b, 0, 0)),
            pl.BlockSpec((gb, 1, gla_vw), lambda b: (b, 0, 0)),
            pl.BlockSpec((1, dv), lambda b: (0, 0)),
        ],
        out_specs=[
            pl.BlockSpec((gb, Hg, dk, dv), lambda b: (b, 0, 0, 0)),
            pl.BlockSpec((gb, 1, gla_vw), lambda b: (b, 0, 0)),
        ],
        out_shape=[jax.ShapeDtypeStruct((Bd, Hg, dk, dv), F32), jax.ShapeDtypeStruct((Bd, 1, gla_vw), F32)],
        compiler_params=cparams(("arbitrary",)),
        name="gla_sample",
    )(state_gla[0], col(la_s), col(gk_s), col(gq_s), gv_s.reshape(Bd, 1, gla_vw), gg_s.reshape(Bd, 1, gla_vw), gnorm)
    go_s = go_s.reshape(Bd, gla_vw).astype(BF16)

    y_s = _ffn_call(xs[None], foT_s[None], go_s[None], wo_bf, wup_bf, wdn_bf, g2, gf, Bd).reshape(Bd, 1, D)

    new_k_p = jnp.transpose(kT_p.reshape(1, B, H, dh, S), (0, 1, 4, 2, 3))
    new_v_p = jnp.transpose(vT_p.reshape(1, B, H, dh, S), (0, 1, 4, 2, 3))
    new_lf_p = jnp.transpose(lfT_p, (0, 2, 1)).reshape(1, B, S, H)
    return (y_p, y_s, new_k_p, new_v_p, new_lf_p, sfin_p.reshape(1, B, Hg, dk, dv),
            k_s.reshape(1, Bd, 1, H, dh), v_s.reshape(1, Bd, 1, H, dh), lf_s.reshape(1, Bd, 1, H),
            s_new.reshape(1, Bd, Hg, dk, dv))
```

```python
import functools

import jax
import jax.numpy as jnp
from jax import lax
from jax.experimental import pallas as pl
from jax.experimental.pallas import tpu as pltpu

F32 = jnp.float32
BF16 = jnp.bfloat16

LANES = 128
SUBLANES = 8
VMEM_LIMIT_BYTES = 56 * 1024 * 1024

EPS = 1e-6
LOG2E = 1.4426950408889634
N_AUG = 3
FF_LANE0 = 16
GLA_GATE_NORM = 16.0
GLA_CHUNK = 128
GLA_SUB = 32
PROJ_TM = 512
ATT_T = 512
ATT_KS = 256
FFN_TM = 512
FFN_CHUNK = 1024
DEC_G = 8
GLA_SAMPLE_BLOCK = 8
DEC_NSLOT = 5


def _dot(a, b):
    return jnp.dot(a, b, preferred_element_type=F32)


def _dot_nt(a, b):
    return lax.dot_general(a, b, (((1,), (1,)), ((), ())), preferred_element_type=F32)


def _split3(x):
    hi = x.astype(BF16).astype(F32)
    r = x - hi
    mid = r.astype(BF16).astype(F32)
    lo = r - mid
    return hi, mid, lo


def _log_sigmoid(x):
    return jnp.minimum(x, 0.0) - jnp.log1p(jnp.exp(-jnp.abs(x)))


def _silu(x):
    return x / (1.0 + jnp.exp(-x))


def _rms(x, g):
    return x * lax.rsqrt(jnp.mean(x * x, axis=-1, keepdims=True) + EPS) * g


def _proj_prompt_kernel(x_ref, g1_ref, wtok_ref, wdm_ref, bf_ref, bfrow_ref, wgate_ref, bgate_ref, gnorm_ref,
                        qT_out, kT_out, vT_out, lfT_out, cT_out, kaug_out, go_out, sfin_out,
                        carry_s, bdt_s, uincl_s, ltm_s, lincl_s, gq_s, gk_s, gv_s, gg_s, la_s,
                        *, tm, fox_w, fox_dh, gla_kw, gla_vw, n_heads_gla):
    t = pl.program_id(1)
    nt = pl.num_programs(1)
    dk = gla_kw // n_heads_gla
    dv = gla_vw // n_heads_gla

    @pl.when(jnp.logical_and(pl.program_id(0) == 0, t == 0))
    def _():
        r = lax.broadcasted_iota(jnp.int32, (tm, tm), 0)
        c = lax.broadcasted_iota(jnp.int32, (tm, tm), 1)
        uincl_s[...] = jnp.where(r <= c, 1.0, 0.0).astype(BF16)
        ltm_s[...] = jnp.where(c <= r, 1.0, 0.0).astype(BF16)
        r = lax.broadcasted_iota(jnp.int32, (GLA_CHUNK, GLA_CHUNK), 0)
        c = lax.broadcasted_iota(jnp.int32, (GLA_CHUNK, GLA_CHUNK), 1)
        lincl_s[...] = jnp.where(c <= r, 1.0, 0.0).astype(BF16)

    @pl.when(t == 0)
    def _():
        carry_s[...] = jnp.zeros_like(carry_s)
        bdt_s[...] = jnp.zeros_like(bdt_s)

    x = x_ref[0]
    xn = _rms(x, g1_ref[...]).astype(BF16)

    z = _dot_nt(xn, wtok_ref[...])
    kz = z[:, 0:fox_w]
    o0 = fox_w
    gq_s[...] = z[:, o0:o0 + gla_kw] * (dk ** -0.5)
    o0 += gla_kw
    gk_s[...] = z[:, o0:o0 + gla_kw]
    o0 += gla_kw
    gv_s[...] = z[:, o0:o0 + gla_vw]
    o0 += gla_vw
    gg_s[...] = z[:, o0:o0 + gla_vw]
    o0 += gla_vw
    misc = z[:, o0:o0 + LANES]
    pre = _dot(misc.astype(BF16), wgate_ref[...]) + bgate_ref[...]
    la_s[...] = _log_sigmoid(pre) * (1.0 / GLA_GATE_NORM)

    lf_tok = _log_sigmoid(misc + bfrow_ref[...])
    st3 = jnp.concatenate(_split3(lf_tok), axis=1).astype(BF16)
    cc = _dot(ltm_s[...], st3)
    cs_tok = cc[:, 0:LANES] + cc[:, LANES:2 * LANES] + cc[:, 2 * LANES:3 * LANES]
    d_tok = (cs_tok - cs_tok[0:1, :]) * LOG2E
    lane_k = lax.broadcasted_iota(jnp.int32, (tm, LANES), 1)
    for h in range(fox_w // fox_dh):
        own = (h % 2) * fox_dh
        spare = (1 - h % 2) * fox_dh
        parts = _split3(jnp.broadcast_to(d_tok[:, FF_LANE0 + h:FF_LANE0 + h + 1], (tm, LANES)))
        aug = jnp.zeros((tm, LANES), F32)
        for n, part in enumerate(parts):
            aug = jnp.where(lane_k == spare + n, -part, aug)
        own_l = jnp.logical_and(lane_k >= own, lane_k < own + fox_dh)
        kaug_out[0, h] = jnp.where(own_l, kz[:, (h // 2) * LANES:(h // 2 + 1) * LANES], aug).astype(BF16)

    zt = _dot_nt(wdm_ref[...], xn)
    qT_out[0] = (zt[0:fox_w] * (fox_dh ** -0.5 * LOG2E)).astype(BF16)
    kT_out[0] = zt[fox_w:2 * fox_w]
    vT_out[0] = zt[2 * fox_w:3 * fox_w]
    lf = _log_sigmoid(zt[3 * fox_w:3 * fox_w + SUBLANES] + bf_ref[...])
    lfT_out[0] = lf
    hi, mid, lo = _split3(lf)
    stack = jnp.concatenate([hi, mid, lo, jnp.zeros_like(hi)], axis=0).astype(BF16)
    cs = _dot(stack, uincl_s[...])
    cs = cs[0:8] + cs[8:16] + cs[16:24]
    carry = carry_s[...]
    cT_out[0] = cs + carry[:, 0:1]
    tot = _dot(stack, jnp.ones((tm, LANES), BF16))
    carry_s[...] = carry + tot[0:8] + tot[8:16] + tot[16:24]

    nsub = GLA_CHUNK // GLA_SUB
    rowi = lax.broadcasted_iota(jnp.int32, (GLA_CHUNK, gla_kw), 0)
    lanei = lax.broadcasted_iota(jnp.int32, (GLA_CHUNK, gla_kw), 1)
    ar = lax.broadcasted_iota(jnp.int32, (GLA_CHUNK, GLA_CHUNK), 0)
    ac = lax.broadcasted_iota(jnp.int32, (GLA_CHUNK, GLA_CHUNK), 1)
    tri_blk = jnp.logical_and(ar // GLA_SUB == ac // GLA_SUB, ar >= ac)
    br = lax.broadcasted_iota(jnp.int32, (gla_vw, gla_kw), 0)
    bc = lax.broadcasted_iota(jnp.int32, (gla_vw, gla_kw), 1)
    bd_mask = (br // dv) == (bc // dk)

    def chunk_body(ci, _):
        r0 = ci * GLA_CHUNK
        la_c = la_s[pl.ds(r0, GLA_CHUNK), :]
        gq_c = gq_s[pl.ds(r0, GLA_CHUNK), :]
        gk_c = gk_s[pl.ds(r0, GLA_CHUNK), :]
        gv_c = gv_s[pl.ds(r0, GLA_CHUNK), :]
        gg_c = gg_s[pl.ds(r0, GLA_CHUNK), :]
        h3, m3, l3 = _split3(la_c)
        st = jnp.concatenate([h3, m3, l3], axis=1).astype(BF16)
        bb = _dot(lincl_s[...], st)
        b = bb[:, 0:gla_kw] + bb[:, gla_kw:2 * gla_kw] + bb[:, 2 * gla_kw:3 * gla_kw]
        bmid_l, bend_l, b0_l = [], [], []
        for i in range(nsub):
            s0 = i * GLA_SUB
            bmid_l.append(b[s0 + GLA_SUB // 2:s0 + GLA_SUB // 2 + 1])
            bend_l.append(b[s0 + GLA_SUB - 1:s0 + GLA_SUB])
            b0_l.append(jnp.zeros((1, gla_kw), F32) if i == 0 else b[s0 - 1:s0])
        bc_rows = lambda rows: jnp.concatenate(
            [jnp.broadcast_to(r, (GLA_SUB, gla_kw)) for r in rows], axis=0)
        bmid, bend, b0 = bc_rows(bmid_l), bc_rows(bend_l), bc_rows(b0_l)
        qt = (gq_c * jnp.exp(b - bmid)).astype(BF16)
        kt = (gk_c * jnp.exp(bmid - b)).astype(BF16)
        qp = gq_c * jnp.exp(b - b0)
        kd = gk_c * jnp.exp(bend - b)
        gv_bf = gv_c.astype(BF16)
        gvT_bf = gv_c.T.astype(BF16)

        rms = [jnp.logical_and(rowi >= i * GLA_SUB, rowi < (i + 1) * GLA_SUB) for i in range(nsub)]
        uts = [_dot(gvT_bf, jnp.where(rms[i], kd, 0.0).astype(BF16)) for i in range(nsub)]
        states = [bdt_s[...]]
        for i in range(nsub):
            decay = jnp.exp(bend_l[i] - b0_l[i])
            states.append(states[i] * decay + jnp.where(bd_mask, uts[i], 0.0))
        bdt_s[...] = states[nsub]
        o_inter = None
        for i in range(nsub):
            d = _dot_nt(jnp.where(rms[i], qp, 0.0).astype(BF16), states[i].astype(BF16))
            o_inter = d if o_inter is None else o_inter + d

        for h in range(n_heads_gla):
            hm = jnp.logical_and(lanei >= h * dk, lanei < (h + 1) * dk)
            a = _dot_nt(jnp.where(hm, qt, jnp.zeros_like(qt)), kt)
            a = jnp.where(tri_blk, a, 0.0).astype(BF16)
            o_h = _dot(a, gv_bf[:, h * dv:(h + 1) * dv]) + o_inter[:, h * dv:(h + 1) * dv]
            o_n = _rms(o_h, gnorm_ref[...])
            go = o_n * _silu(gg_c[:, h * dv:(h + 1) * dv])
            go_out[0, pl.ds(r0, GLA_CHUNK), h * dv:(h + 1) * dv] = go.astype(BF16)
        return 0

    for ci in range(tm // GLA_CHUNK):
        chunk_body(ci, 0)

    @pl.when(t == nt - 1)
    def _():
        bd = bdt_s[...].T
        for h in range(n_heads_gla):
            sfin_out[0, h] = bd[h * dk:(h + 1) * dk, h * dv:(h + 1) * dv]


def _attn_prompt_kernel(qT_ref, kaug_ref, vT_ref, c_ref, oT_ref, vaug_s, qa_s, m_s, acc_s, *, t_blk, dh):
    i = pl.program_id(2)
    hw2 = 2 * dh

    @pl.when(i == 0)
    def _():
        for h in range(2):
            vaug_s[h, 0:dh, :] = vT_ref[0, h * dh:(h + 1) * dh, :].astype(BF16)
            vaug_s[h, dh:, :] = jnp.ones((vaug_s.shape[1] - dh, vaug_s.shape[2]), BF16)

    qT = qT_ref[0]
    rowq = lax.broadcasted_iota(jnp.int32, (hw2, t_blk), 0)
    keyi = lax.broadcasted_iota(jnp.int32, (ATT_KS, t_blk), 0)
    qryi = lax.broadcasted_iota(jnp.int32, (ATT_KS, t_blk), 1)
    t0 = pl.multiple_of(i * t_blk, t_blk)
    c_q = []
    for h in range(2):
        spare = (1 - h) * dh
        own_r = jnp.logical_and(rowq >= h * dh, rowq < (h + 1) * dh)
        ones_r = jnp.logical_and(rowq >= spare, rowq < spare + N_AUG)
        qa_s[h] = jnp.where(own_r, qT, jnp.where(ones_r, 1.0, 0.0).astype(BF16))
        m_s[h] = jnp.full(m_s.shape[1:], -jnp.inf, F32)
        acc_s[h] = jnp.zeros(acc_s.shape[1:], F32)
        c_q.append(c_ref[0, 0, h:h + 1, pl.ds(t0, LANES)][:, 0:1])

    n_sub = t_blk // ATT_KS

    def steps(tiles):
        work = []
        for j, masked in tiles:
            k0 = pl.multiple_of(j * t_blk, t_blk)
            kks = [pl.multiple_of(k0 + ks * ATT_KS, ATT_KS) for ks in range(n_sub)]
            s_all = [[_dot(kaug_ref[0, 0, h, pl.ds(kks[ks], ATT_KS), :], qa_s[h]) for ks in range(n_sub)]
                     for h in range(2)]
            work.append((k0, kks, s_all, masked))
        for h in range(2):
            m_run = m_s[h]
            acc = acc_s[h]
            for k0, kks, s_all, masked in work:
                off = (c_ref[0, 0, h:h + 1, pl.ds(k0, LANES)][:, 0:1] - c_q[h]) * LOG2E
                for ks in range(n_sub):
                    s = s_all[h][ks]
                    if masked:
                        s = jnp.where(keyi + ks * ATT_KS <= qryi, s, -jnp.inf)
                    m_new = jnp.maximum(m_run, jnp.max(s, axis=0, keepdims=True) - off)
                    p = jnp.exp2(s - (m_new + off))
                    alpha = jnp.exp2(m_run - m_new)
                    pv = _dot(vaug_s[h, :, pl.ds(kks[ks], ATT_KS)], p.astype(BF16))
                    acc = alpha * acc + pv
                    m_run = m_new
            acc_s[h] = acc
            m_s[h] = m_run

    def pair_body(jj, carry):
        steps([(2 * jj, False), (2 * jj + 1, False)])
        return carry

    lax.fori_loop(0, lax.shift_right_logical(i, 1), pair_body, 0)
    odd = jnp.bitwise_and(i, 1) == 1

    @pl.when(odd)
    def _():
        steps([(i - 1, False), (i, True)])

    @pl.when(jnp.logical_not(odd))
    def _():
        steps([(i, True)])

    outs = [acc_s[h][0:dh, :] / acc_s[h][dh:dh + 1, :] for h in range(2)]
    oT_ref[0] = jnp.concatenate(outs, axis=0).astype(BF16)


def _ffn_kernel(x_ref, foT_ref, go_ref, wo_ref, wup_ref, wdn_ref, g2_ref, gf_ref, y_ref, u_s, *, fox_w, d_ff):
    x = x_ref[0]
    h = x + (_dot(foT_ref[0].T, wo_ref[0:fox_w, :]) + _dot(go_ref[0], wo_ref[fox_w:, :]))
    hn = _rms(h, g2_ref[...]).astype(BF16)
    for c in range(d_ff // FFN_CHUNK):
        u = _dot(hn, wup_ref[:, c * FFN_CHUNK:(c + 1) * FFN_CHUNK])
        u_s[:, c * FFN_CHUNK:(c + 1) * FFN_CHUNK] = jnp.square(jnp.maximum(u, 0.0)).astype(BF16)
    y_ref[0] = _rms(h + _dot(u_s[...], wdn_ref[...]), gf_ref[...])


def _ffn_call(x3, foT, go, wo, wup, wdn, g2, gf, tm):
    nb, n, d = x3.shape
    fox_w = foT.shape[1]
    d_ff = wup.shape[1]
    const = lambda shape: pl.BlockSpec(shape, lambda b, i: (0, 0), pipeline_mode=pl.Buffered(1))
    return pl.pallas_call(
        functools.partial(_ffn_kernel, fox_w=fox_w, d_ff=d_ff),
        grid=(nb, n // tm),
        in_specs=[
            pl.BlockSpec((1, tm, d), lambda b, i: (b, i, 0)),
            pl.BlockSpec((1, fox_w, tm), lambda b, i: (b, 0, i)),
            pl.BlockSpec((1, tm, go.shape[2]), lambda b, i: (b, i, 0)),
            const(wo.shape), const(wup.shape), const(wdn.shape), const(g2.shape), const(gf.shape),
        ],
        out_specs=pl.BlockSpec((1, tm, d), lambda b, i: (b, i, 0)),
        out_shape=jax.ShapeDtypeStruct((nb, n, d), F32),
        scratch_shapes=[pltpu.VMEM((tm, d_ff), BF16)],
        compiler_params=pltpu.CompilerParams(dimension_semantics=("arbitrary", "arbitrary"),
                                             vmem_limit_bytes=VMEM_LIMIT_BYTES),
        name="merge_ffn",
    )(x3, foT, go, wo, wup, wdn, g2, gf)


def _proj_sample_kernel(x_ref, g1_ref, wtok_ref, wdm_ref, bf_ref, wgate_ref, bgate_ref,
                        q_out, k_out, v_out, lf_out, gq_out, gk_out, gv_out, gg_out, la_out,
                        *, fox_w, fox_dh, gla_kw, gla_vw, n_heads_gla, n_heads_fox):
    dk = gla_kw // n_heads_gla
    xn = _rms(x_ref[...], g1_ref[...]).astype(BF16)
    z = _dot_nt(xn, wtok_ref[...])
    o0 = fox_w
    gq_out[...] = z[:, o0:o0 + gla_kw] * (dk ** -0.5)
    o0 += gla_kw
    gk_out[...] = z[:, o0:o0 + gla_kw]
    o0 += gla_kw
    gv_out[...] = z[:, o0:o0 + gla_vw]
    o0 += gla_vw
    gg_out[...] = z[:, o0:o0 + gla_vw]
    o0 += gla_vw
    glr = z[:, o0:o0 + LANES].astype(BF16)
    la_out[...] = _log_sigmoid(_dot(glr, wgate_ref[...]) + bgate_ref[...]) * (1.0 / GLA_GATE_NORM)
    z2 = _dot_nt(xn, wdm_ref[...])
    q_out[...] = z2[:, 0:fox_w] * (fox_dh ** -0.5)
    k_out[...] = z2[:, fox_w:2 * fox_w]
    v_out[...] = z2[:, 2 * fox_w:3 * fox_w]
    lf_out[...] = _log_sigmoid(z2[:, 3 * fox_w:3 * fox_w + n_heads_fox] + bf_ref[...])


def _ffn_decode_kernel(pt_ref, x_ref, foT_ref, go_ref, wo_ref, wup_ref, wdn_ref, g2_ref, gf_ref,
                       q_ref, knew_ref, vnew_ref, lfnew_ref, kc_hbm, vc_hbm, lfc_hbm,
                       y_ref, o_ref,
                       u_s, ring, lfbuf, rev_s, zbuf, acc_s, qb_s, ustrict_s, live_s, sem_ring, sem_lf,
                       *, fox_w, d_ff, n_pages, n_b, n_heads, dh):
    b = pl.program_id(0) * pl.num_programs(1) + pl.program_id(1)
    page = LANES
    hw = n_heads * dh
    nch = n_pages // DEC_G
    per_b = 2 * nch
    total = n_b * per_b

    def start_chunk(g):
        bg = g // per_b
        c = g - bg * per_b
        slot = lax.rem(g, DEC_NSLOT)

        @pl.when(c < nch)
        def _():
            for j in range(DEC_G):
                p = n_pages - 1 - (c * DEC_G + j)
                pltpu.make_async_copy(kc_hbm.at[pt_ref[bg, p]], ring.at[slot, j], sem_ring.at[slot]).start()

        @pl.when(c >= nch)
        def _():
            for j in range(DEC_G):
                p = n_pages - 1 - ((c - nch) * DEC_G + j)
                pltpu.make_async_copy(vc_hbm.at[pt_ref[bg, p]], ring.at[slot, j], sem_ring.at[slot]).start()

    def wait_chunk(g):
        slot = lax.rem(g, DEC_NSLOT)
        for j in range(DEC_G):
            pltpu.make_async_copy(kc_hbm.at[0], ring.at[slot, j], sem_ring.at[slot]).wait()

    def start_lf(bb):
        sl = lax.rem(bb, 2)

        def body(p, _):
            pltpu.make_async_copy(lfc_hbm.at[pt_ref[bb, p]], lfbuf.at[sl, p], sem_lf.at[sl]).start()
            return 0

        lax.fori_loop(0, n_pages, body, 0)

    def wait_lf(bb):
        sl = lax.rem(bb, 2)

        def body(p, _):
            pltpu.make_async_copy(lfc_hbm.at[0], lfbuf.at[sl, p], sem_lf.at[sl]).wait()
            return 0

        lax.fori_loop(0, n_pages, body, 0)

    g0 = b * per_b

    @pl.when(b == 0)
    def _():
        r = lax.broadcasted_iota(jnp.int32, (page, page), 0)
        c = lax.broadcasted_iota(jnp.int32, (page, page), 1)
        ustrict_s[...] = jnp.where(r > c, 1.0, 0.0).astype(BF16)
        start_lf(b)
        for g in range(DEC_NSLOT - 1):
            start_chunk(g0 + g)

    wait_lf(b)

    @pl.when(b + 1 < n_b)
    def _():
        start_lf(b + 1)

    sl = lax.rem(b, 2)
    lf2d = lfbuf[sl].reshape(n_pages * n_heads, page)
    hi, mid, lo = _split3(lf2d)
    u = ustrict_s[...]
    rev = _dot(hi.astype(BF16), u) + _dot(mid.astype(BF16), u) + _dot(lo.astype(BF16), u)
    rev_s[...] = rev.reshape(n_pages, n_heads, page)

    qb = jnp.broadcast_to(q_ref[0], (hw, page))
    qb_s[...] = qb

    def head_sum(x):
        return jnp.sum(x.reshape(n_heads, dh, page), axis=1)

    def head_bcast(x):
        return jnp.broadcast_to(x[:, None, :], (n_heads, dh, page)).reshape(hw, page)

    def k_body(c, carry):
        g = g0 + c

        @pl.when(g + (DEC_NSLOT - 1) < total)
        def _():
            start_chunk(g + (DEC_NSLOT - 1))

        wait_chunk(g)
        slot = lax.rem(g, DEC_NSLOT)
        rows = []
        for hd in range(n_heads):
            hs = slice(hd * dh, (hd + 1) * dh)
            qh = qb_s[hs, :]
            ch = carry[hd:hd + 1]
            for j in range(DEC_G):
                p = n_pages - 1 - (c * DEC_G + j)
                s = jnp.sum(ring[slot, j, hs, :] * qh, axis=0, keepdims=True)
                revp = rev_s[p, hd:hd + 1, :]
                zbuf[hd:hd + 1, pl.ds(pl.multiple_of(p * page, page), page)] = s + revp + ch
                ch = ch + jnp.broadcast_to(revp[:, 0:1] + lfbuf[sl, p, hd:hd + 1, 0:1], (1, page))
            rows.append(ch)
        return jnp.concatenate(rows, axis=0)

    x = x_ref[0]
    h = x + (_dot(foT_ref[0].T, wo_ref[0:fox_w, :]) + _dot(go_ref[0], wo_ref[fox_w:, :]))
    hn = _rms(h, g2_ref[...]).astype(BF16)

    n_ffn = d_ff // FFN_CHUNK
    grp = nch // n_ffn
    carry = jnp.broadcast_to(lfnew_ref[0], (n_heads, page))
    for c in range(n_ffn):
        carry = lax.fori_loop(c * grp, (c + 1) * grp, k_body, carry)
        u = _dot(hn, wup_ref[:, c * FFN_CHUNK:(c + 1) * FFN_CHUNK])
        u_s[:, c * FFN_CHUNK:(c + 1) * FFN_CHUNK] = jnp.square(jnp.maximum(u, 0.0)).astype(BF16)

    z_all = zbuf[...]
    z_new = head_sum(jnp.broadcast_to(knew_ref[0], (hw, page)) * qb)
    m = jnp.maximum(jnp.max(z_all, axis=1, keepdims=True), z_new[:, 0:1])
    p_all = jnp.exp(z_all - m)
    zbuf[...] = p_all
    p_new = jnp.exp(z_new - m)
    l = jnp.sum(p_all, axis=1, keepdims=True) + p_new

    acc_s[...] = jnp.zeros_like(acc_s)

    span = DEC_G * page
    for vc in range(nch):
        lo = (n_pages - (vc + 1) * DEC_G) * page
        live_s[vc] = jnp.where(jnp.max(p_all[:, lo:lo + span]) > 0.0, 1, 0).astype(jnp.int32)

    def v_body(c, carry):
        g = g0 + nch + c
        nxt = c + (DEC_NSLOT - 1)
        nxt_live = jnp.logical_or(nxt >= nch, live_s[jnp.minimum(nxt, nch - 1)] != 0)

        @pl.when(jnp.logical_and(g + (DEC_NSLOT - 1) < total, nxt_live))
        def _():
            start_chunk(g + (DEC_NSLOT - 1))

        @pl.when(jnp.logical_or(c < DEC_NSLOT - 1, live_s[c] != 0))
        def _():
            wait_chunk(g)
            slot = lax.rem(g, DEC_NSLOT)
            for hd in range(n_heads):
                hs = slice(hd * dh, (hd + 1) * dh)
                acc = acc_s[hs, :]
                for j in range(DEC_G):
                    p = n_pages - 1 - (c * DEC_G + j)
                    pp = zbuf[hd:hd + 1, pl.ds(pl.multiple_of(p * page, page), page)]
                    acc = acc + ring[slot, j, hs, :] * jnp.broadcast_to(pp, (dh, page))
                acc_s[hs, :] = acc

        return carry

    down = None
    for c in range(n_ffn):
        lax.fori_loop(c * grp, (c + 1) * grp, v_body, 0)
        d = _dot(u_s[:, c * FFN_CHUNK:(c + 1) * FFN_CHUNK], wdn_ref[c * FFN_CHUNK:(c + 1) * FFN_CHUNK, :])
        down = d if down is None else down + d
    y_ref[0] = _rms(h + down, gf_ref[...])

    num = jnp.sum(acc_s[...], axis=1, keepdims=True) + head_bcast(p_new) * jnp.broadcast_to(vnew_ref[0], (hw, page))
    o_ref[0] = (num / head_bcast(l))[:, 0:1]


def _gla_sample_kernel(s_ref, la_ref, k_ref, q_ref, v_ref, gg_ref, gnorm_ref, s_out, go_out, *, n_heads, dk, dv):
    for i in range(s_ref.shape[0]):
        for h in range(n_heads):
            la = la_ref[i, h * dk:(h + 1) * dk, :]
            kk = k_ref[i, h * dk:(h + 1) * dk, :]
            qq = q_ref[i, h * dk:(h + 1) * dk, :]
            vv = v_ref[i, :, h * dv:(h + 1) * dv]
            s_new = s_ref[i, h] * jnp.exp(la) + kk * vv
            s_out[i, h] = s_new
            o = jnp.sum(qq * s_new, axis=0, keepdims=True)
            o_n = _rms(o, gnorm_ref[...])
            go_out[i, :, h * dv:(h + 1) * dv] = o_n * _silu(gg_ref[i, :, h * dv:(h + 1) * dv])


def kernel(x_prompt, x_sample, cache_k, cache_v, cache_logf, state_gla, page_table, norm1_g, w_in, fox_b_f,
           gla_w_gate_up, gla_b_gate, gla_norm_g, w_o, norm2_g, w_up, w_down, final_g):
    B, S, D = x_prompt.shape
    Bd = x_sample.shape[0]
    depth, n_phys, page, H, dh = cache_k.shape
    _, _, Hg, dk, dv = state_gla.shape
    assert depth == 1 and x_sample.shape[1] == 1 and page == LANES
    fox_w = H * dh
    gla_kw = Hg * dk
    gla_vw = Hg * dv
    rank = gla_w_gate_up.shape[1]
    n_pages = page_table.shape[1]

    wt = jnp.transpose(w_in[0])
    o_fq, o_fk = 0, fox_w
    o_ff = 3 * fox_w
    o_gq = o_ff + H
    o_gk = o_gq + gla_kw
    o_gv = o_gk + gla_kw
    o_glr = o_gv + gla_vw
    o_gg = o_glr + rank
    misc_pad = LANES - FF_LANE0 - H
    w_tok = jnp.concatenate([
        wt[o_fk:o_fk + fox_w], wt[o_gq:o_gq + gla_kw], wt[o_gk:o_gk + gla_kw], wt[o_gv:o_gv + gla_vw],
        wt[o_gg:o_gg + gla_vw], wt[o_glr:o_glr + rank], jnp.zeros((FF_LANE0 - rank, D), F32),
        wt[o_ff:o_ff + H], jnp.zeros((misc_pad, D), F32)], axis=0).astype(BF16)
    w_dm = jnp.concatenate([
        wt[o_fq:o_fq + 3 * fox_w], wt[o_ff:o_ff + H], jnp.zeros((2 * SUBLANES - H, D), F32)], axis=0).astype(BF16)
    bf_row = jnp.concatenate([jnp.zeros((1, FF_LANE0), F32), fox_b_f.reshape(1, H), jnp.zeros((1, misc_pad), F32)],
                             axis=1)
    w_gate = jnp.concatenate([gla_w_gate_up[0], jnp.zeros((LANES - rank, gla_kw), F32)], axis=0).astype(BF16)
    wo_bf = w_o[0].astype(BF16)
    wup_bf = w_up[0].astype(BF16)
    wdn_bf = w_down[0].astype(BF16)
    g1 = norm1_g.reshape(1, D)
    g2 = norm2_g.reshape(1, D)
    gf = final_g.reshape(1, D)
    bgate = gla_b_gate.reshape(1, gla_kw)
    gnorm = gla_norm_g.reshape(1, dv)

    cparams = lambda sem: pltpu.CompilerParams(dimension_semantics=sem, vmem_limit_bytes=VMEM_LIMIT_BYTES)
    single = lambda shape: pl.BlockSpec(shape, lambda *_: (0,) * len(shape), pipeline_mode=pl.Buffered(1))

    tm = PROJ_TM
    nt = S // tm
    n_tok_cols = w_tok.shape[0]
    n_dm_rows = w_dm.shape[0]
    qT_p, kT_p, vT_p, lfT_p, cT_p, kaug_p, go_p, sfin_p = pl.pallas_call(
        functools.partial(_proj_prompt_kernel, tm=tm, fox_w=fox_w, fox_dh=dh, gla_kw=gla_kw, gla_vw=gla_vw,
                          n_heads_gla=Hg),
        grid=(B, nt),
        in_specs=[
            pl.BlockSpec((1, tm, D), lambda b, t: (b, t, 0)),
            single((1, D)), single((n_tok_cols, D)), single((n_dm_rows, D)), single((H, 1)), single((1, LANES)),
            single((LANES, gla_kw)), single((1, gla_kw)), single((1, dv)),
        ],
        out_specs=[
            pl.BlockSpec((1, fox_w, tm), lambda b, t: (b, 0, t)),
            pl.BlockSpec((1, fox_w, tm), lambda b, t: (b, 0, t)),
            pl.BlockSpec((1, fox_w, tm), lambda b, t: (b, 0, t)),
            pl.BlockSpec((1, H, tm), lambda b, t: (b, 0, t)),
            pl.BlockSpec((1, H, tm), lambda b, t: (b, 0, t)),
            pl.BlockSpec((1, H, tm, LANES), lambda b, t: (b, 0, t, 0)),
            pl.BlockSpec((1, tm, gla_vw), lambda b, t: (b, t, 0)),
            pl.BlockSpec((1, Hg, dk, dv), lambda b, t: (b, 0, 0, 0)),
        ],
        out_shape=[
            jax.ShapeDtypeStruct((B, fox_w, S), BF16),
            jax.ShapeDtypeStruct((B, fox_w, S), F32),
            jax.ShapeDtypeStruct((B, fox_w, S), F32),
            jax.ShapeDtypeStruct((B, H, S), F32),
            jax.ShapeDtypeStruct((B, H, S), F32),
            jax.ShapeDtypeStruct((B, H, S, LANES), BF16),
            jax.ShapeDtypeStruct((B, S, gla_vw), BF16),
            jax.ShapeDtypeStruct((B, Hg, dk, dv), F32),
        ],
        scratch_shapes=[
            pltpu.VMEM((H, LANES), F32),
            pltpu.VMEM((gla_vw, gla_kw), F32),
            pltpu.VMEM((tm, tm), BF16),
            pltpu.VMEM((tm, tm), BF16),
            pltpu.VMEM((GLA_CHUNK, GLA_CHUNK), BF16),
            pltpu.VMEM((tm, gla_kw), F32), pltpu.VMEM((tm, gla_kw), F32),
            pltpu.VMEM((tm, gla_vw), F32), pltpu.VMEM((tm, gla_vw), F32), pltpu.VMEM((tm, gla_kw), F32),
        ],
        compiler_params=cparams(("arbitrary", "arbitrary")),
        name="proj_gla_prompt",
    )(x_prompt, g1, w_tok, w_dm, fox_b_f.reshape(H, 1), bf_row, w_gate, bgate, gnorm)

    t_blk = ATT_T
    assert t_blk == tm
    nq = S // t_blk
    n_pairs = fox_w // LANES
    c4 = cT_p.reshape(B, n_pairs, 2, S)
    k5 = kaug_p.reshape(B, n_pairs, 2, S, LANES)
    n_vrows = dh + 2 * SUBLANES
    foT_p = pl.pallas_call(
        functools.partial(_attn_prompt_kernel, t_blk=t_blk, dh=dh),
        grid=(B, n_pairs, nq),
        in_specs=[
            pl.BlockSpec((1, LANES, t_blk), lambda b, p, i: (b, p, i)),
            pl.BlockSpec((1, 1, 2, S, LANES), lambda b, p, i: (b, p, 0, 0, 0)),
            pl.BlockSpec((1, LANES, S), lambda b, p, i: (b, p, 0)),
            pl.BlockSpec((1, 1, 2, S), lambda b, p, i: (b, p, 0, 0)),
        ],
        out_specs=pl.BlockSpec((1, LANES, t_blk), lambda b, p, i: (b, p, i)),
        out_shape=jax.ShapeDtypeStruct((B, fox_w, S), BF16),
        scratch_shapes=[
            pltpu.VMEM((2, n_vrows, S), BF16),
            pltpu.VMEM((2, LANES, t_blk), BF16),
            pltpu.VMEM((2, 1, t_blk), F32),
            pltpu.VMEM((2, n_vrows, t_blk), F32),
        ],
        compiler_params=cparams(("arbitrary", "arbitrary", "arbitrary")),
        name="fox_attn_prompt",
    )(qT_p, k5, vT_p, c4)

    xs = x_sample.reshape(Bd, D)
    full = lambda shape: pl.BlockSpec(shape, lambda: (0,) * len(shape))
    s_shapes = [(Bd, fox_w), (Bd, fox_w), (Bd, fox_w), (Bd, H), (Bd, gla_kw), (Bd, gla_kw), (Bd, gla_vw),
                (Bd, gla_vw), (Bd, gla_kw)]
    q_s, k_s, v_s, lf_s, gq_s, gk_s, gv_s, gg_s, la_s = pl.pallas_call(
        functools.partial(_proj_sample_kernel, fox_w=fox_w, fox_dh=dh, gla_kw=gla_kw, gla_vw=gla_vw, n_heads_gla=Hg,
                          n_heads_fox=H),
        in_specs=[full((Bd, D)), full((1, D)), full((n_tok_cols, D)), full((n_dm_rows, D)), full((1, H)),
                  full((LANES, gla_kw)), full((1, gla_kw))],
        out_specs=[full(s) for s in s_shapes],
        out_shape=[jax.ShapeDtypeStruct(s, F32) for s in s_shapes],
        compiler_params=pltpu.CompilerParams(vmem_limit_bytes=VMEM_LIMIT_BYTES),
        name="proj_sample",
    )(xs, g1, w_tok, w_dm, fox_b_f.reshape(1, H), w_gate, bgate)

    kc = jnp.transpose(cache_k[0], (0, 2, 3, 1)).reshape(n_phys, fox_w, page)
    vc = jnp.transpose(cache_v[0], (0, 2, 3, 1)).reshape(n_phys, fox_w, page)
    lfc = jnp.transpose(cache_logf[0], (0, 2, 1))
    col = lambda a: a.reshape(Bd, a.shape[1], 1)
    ftm = FFN_TM
    n_ft = S // ftm
    assert B * n_ft == Bd
    d_ff = wup_bf.shape[1]
    assert (n_pages // DEC_G) % (d_ff // FFN_CHUNK) == 0
    wconst = lambda shape: pl.BlockSpec(shape, lambda b, i, pt: (0, 0), pipeline_mode=pl.Buffered(1))
    seq = lambda b, i, pt: (b * n_ft + i, 0, 0)
    grid_spec = pltpu.PrefetchScalarGridSpec(
        num_scalar_prefetch=1,
        grid=(B, n_ft),
        in_specs=[
            pl.BlockSpec((1, ftm, D), lambda b, i, pt: (b, i, 0)),
            pl.BlockSpec((1, fox_w, ftm), lambda b, i, pt: (b, 0, i)),
            pl.BlockSpec((1, ftm, gla_vw), lambda b, i, pt: (b, i, 0)),
            wconst(wo_bf.shape), wconst(wup_bf.shape), wconst(wdn_bf.shape), wconst(g2.shape), wconst(gf.shape),
            pl.BlockSpec((1, fox_w, 1), seq), pl.BlockSpec((1, fox_w, 1), seq), pl.BlockSpec((1, fox_w, 1), seq),
            pl.BlockSpec((1, H, 1), seq),
            pl.BlockSpec(memory_space=pl.ANY), pl.BlockSpec(memory_space=pl.ANY), pl.BlockSpec(memory_space=pl.ANY),
        ],
        out_specs=[
            pl.BlockSpec((1, ftm, D), lambda b, i, pt: (b, i, 0)),
            pl.BlockSpec((1, fox_w, 1), seq),
        ],
        scratch_shapes=[
            pltpu.VMEM((ftm, d_ff), BF16),
            pltpu.VMEM((DEC_NSLOT, DEC_G, fox_w, page), F32),
            pltpu.VMEM((2, n_pages, H, page), F32),
            pltpu.VMEM((n_pages, H, page), F32),
            pltpu.VMEM((H, n_pages * page), F32),
            pltpu.VMEM((fox_w, page), F32),
            pltpu.VMEM((fox_w, page), F32),
            pltpu.VMEM((page, page), BF16),
            pltpu.SMEM((n_pages // DEC_G,), jnp.int32),
            pltpu.SemaphoreType.DMA((DEC_NSLOT,)),
            pltpu.SemaphoreType.DMA((2,)),
        ],
    )
    y_p, fo_s = pl.pallas_call(
        functools.partial(_ffn_decode_kernel, fox_w=fox_w, d_ff=d_ff, n_pages=n_pages, n_b=Bd, n_heads=H, dh=dh),
        grid_spec=grid_spec,
        out_shape=[jax.ShapeDtypeStruct((B, S, D), F32), jax.ShapeDtypeStruct((Bd, fox_w, 1), F32)],
        compiler_params=cparams(("arbitrary", "arbitrary")),
        name="ffn_prompt_fox_decode",
    )(page_table, x_prompt, foT_p, go_p, wo_bf, wup_bf, wdn_bf, g2, gf,
      col(q_s), col(k_s), col(v_s), col(lf_s), kc, vc, lfc)
    foT_s = fo_s.reshape(Bd, fox_w).T.astype(BF16)

    gb = GLA_SAMPLE_BLOCK
    s_new, go_s = pl.pallas_call(
        functools.partial(_gla_sample_kernel, n_heads=Hg, dk=dk, dv=dv),
        grid=(Bd // gb,),
        in_specs=[
            pl.BlockSpec((gb, Hg, dk, dv), lambda b: (b, 0, 0, 0)),
            pl.BlockSpec((gb, gla_kw, 1), lambda b: (b, 0, 0)),
            pl.BlockSpec((gb, gla_kw, 1), lambda b: (b, 0, 0)),
            pl.BlockSpec((gb, gla_kw, 1), lambda b: (b, 0, 0)),
            pl.BlockSpec((gb, 1, gla_vw), lambda b: (b, 0, 0)),
            pl.BlockSpec((gb, 1, gla_vw), lambda b: (b, 0, 0)),
            pl.BlockSpec((1, dv), lambda b: (0, 0)),
        ],
        out_specs=[
            pl.BlockSpec((gb, Hg, dk, dv), lambda b: (b, 0, 0, 0)),
            pl.BlockSpec((gb, 1, gla_vw), lambda b: (b, 0, 0)),
        ],
        out_shape=[jax.ShapeDtypeStruct((Bd, Hg, dk, dv), F32), jax.ShapeDtypeStruct((Bd, 1, gla_vw), F32)],
        compiler_params=cparams(("arbitrary",)),
        name="gla_sample",
    )(state_gla[0], col(la_s), col(gk_s), col(gq_s), gv_s.reshape(Bd, 1, gla_vw), gg_s.reshape(Bd, 1, gla_vw), gnorm)
    go_s = go_s.reshape(Bd, gla_vw).astype(BF16)

    y_s = _ffn_call(xs[None], foT_s[None], go_s[None], wo_bf, wup_bf, wdn_bf, g2, gf, Bd).reshape(Bd, 1, D)

    new_k_p = jnp.transpose(kT_p.reshape(1, B, H, dh, S), (0, 1, 4, 2, 3))
    new_v_p = jnp.transpose(vT_p.reshape(1, B, H, dh, S), (0, 1, 4, 2, 3))
    new_lf_p = jnp.transpose(lfT_p, (0, 2, 1)).reshape(1, B, S, H)
    return (y_p, y_s, new_k_p, new_v_p, new_lf_p, sfin_p.reshape(1, B, Hg, dk, dv),
            k_s.reshape(1, Bd, 1, H, dh), v_s.reshape(1, Bd, 1, H, dh), lf_s.reshape(1, Bd, 1, H),
            s_new.reshape(1, Bd, Hg, dk, dv))
```

```python
import functools

import jax
import jax.numpy as jnp
from jax import lax
from jax.experimental import pallas as pl
from jax.experimental.pallas import tpu as pltpu

F32 = jnp.float32
BF16 = jnp.bfloat16

LANES = 128
SUBLANES = 8
VMEM_LIMIT_BYTES = 56 * 1024 * 1024

EPS = 1e-6
LOG2E = 1.4426950408889634
N_AUG = 3
FF_LANE0 = 16
GLA_GATE_NORM = 16.0
GLA_CHUNK = 128
GLA_SUB = 32
PROJ_TM = 512
ATT_T = 512
ATT_KS = 256
ATT_SKIP_BITS = 160.0
ATT_NORM_SLACK = 1.001
FFN_TM = 512
FFN_CHUNK = 1024
DEC_G = 8
GLA_SAMPLE_BLOCK = 8
DEC_NSLOT = 5


def _dot(a, b):
    return jnp.dot(a, b, preferred_element_type=F32)


def _dot_nt(a, b):
    return lax.dot_general(a, b, (((1,), (1,)), ((), ())), preferred_element_type=F32)


def _split3(x):
    hi = x.astype(BF16).astype(F32)
    r = x - hi
    mid = r.astype(BF16).astype(F32)
    lo = r - mid
    return hi, mid, lo


def _log_sigmoid(x):
    return jnp.minimum(x, 0.0) - jnp.log1p(jnp.exp(-jnp.abs(x)))


def _silu(x):
    return x / (1.0 + jnp.exp(-x))


def _rms(x, g):
    return x * lax.rsqrt(jnp.mean(x * x, axis=-1, keepdims=True) + EPS) * g


def _proj_prompt_kernel(x_ref, g1_ref, wtok_ref, wdm_ref, bf_ref, bfrow_ref, wgate_ref, bgate_ref, gnorm_ref,
                        qT_out, kT_out, vT_out, lfT_out, cT_out, kaug_out, go_out, sfin_out,
                        carry_s, bdt_s, uincl_s, ltm_s, lincl_s, gq_s, gk_s, gv_s, gg_s, la_s,
                        *, tm, fox_w, fox_dh, gla_kw, gla_vw, n_heads_gla):
    t = pl.program_id(1)
    nt = pl.num_programs(1)
    dk = gla_kw // n_heads_gla
    dv = gla_vw // n_heads_gla

    @pl.when(jnp.logical_and(pl.program_id(0) == 0, t == 0))
    def _():
        r = lax.broadcasted_iota(jnp.int32, (tm, tm), 0)
        c = lax.broadcasted_iota(jnp.int32, (tm, tm), 1)
        uincl_s[...] = jnp.where(r <= c, 1.0, 0.0).astype(BF16)
        ltm_s[...] = jnp.where(c <= r, 1.0, 0.0).astype(BF16)
        r = lax.broadcasted_iota(jnp.int32, (GLA_CHUNK, GLA_CHUNK), 0)
        c = lax.broadcasted_iota(jnp.int32, (GLA_CHUNK, GLA_CHUNK), 1)
        lincl_s[...] = jnp.where(c <= r, 1.0, 0.0).astype(BF16)

    @pl.when(t == 0)
    def _():
        carry_s[...] = jnp.zeros_like(carry_s)
        bdt_s[...] = jnp.zeros_like(bdt_s)

    x = x_ref[0]
    xn = _rms(x, g1_ref[...]).astype(BF16)

    z = _dot_nt(xn, wtok_ref[...])
    kz = z[:, 0:fox_w]
    o0 = fox_w
    gq_s[...] = z[:, o0:o0 + gla_kw] * (dk ** -0.5)
    o0 += gla_kw
    gk_s[...] = z[:, o0:o0 + gla_kw]
    o0 += gla_kw
    gv_s[...] = z[:, o0:o0 + gla_vw]
    o0 += gla_vw
    gg_s[...] = z[:, o0:o0 + gla_vw]
    o0 += gla_vw
    misc = z[:, o0:o0 + LANES]
    pre = _dot(misc.astype(BF16), wgate_ref[...]) + bgate_ref[...]
    la_s[...] = _log_sigmoid(pre) * (1.0 / GLA_GATE_NORM)

    lf_tok = _log_sigmoid(misc + bfrow_ref[...])
    st3 = jnp.concatenate(_split3(lf_tok), axis=1).astype(BF16)
    cc = _dot(ltm_s[...], st3)
    cs_tok = cc[:, 0:LANES] + cc[:, LANES:2 * LANES] + cc[:, 2 * LANES:3 * LANES]
    d_tok = (cs_tok - cs_tok[0:1, :]) * LOG2E
    lane_k = lax.broadcasted_iota(jnp.int32, (tm, LANES), 1)
    for h in range(fox_w // fox_dh):
        own = (h % 2) * fox_dh
        spare = (1 - h % 2) * fox_dh
        parts = _split3(jnp.broadcast_to(d_tok[:, FF_LANE0 + h:FF_LANE0 + h + 1], (tm, LANES)))
        aug = jnp.zeros((tm, LANES), F32)
        for n, part in enumerate(parts):
            aug = jnp.where(lane_k == spare + n, -part, aug)
        own_l = jnp.logical_and(lane_k >= own, lane_k < own + fox_dh)
        kaug_out[0, h] = jnp.where(own_l, kz[:, (h // 2) * LANES:(h // 2 + 1) * LANES], aug).astype(BF16)

    zt = _dot_nt(wdm_ref[...], xn)
    qT_out[0] = (zt[0:fox_w] * (fox_dh ** -0.5 * LOG2E)).astype(BF16)
    kT_out[0] = zt[fox_w:2 * fox_w]
    vT_out[0] = zt[2 * fox_w:3 * fox_w]
    lf = _log_sigmoid(zt[3 * fox_w:3 * fox_w + SUBLANES] + bf_ref[...])
    lfT_out[0] = lf
    hi, mid, lo = _split3(lf)
    stack = jnp.concatenate([hi, mid, lo, jnp.zeros_like(hi)], axis=0).astype(BF16)
    cs = _dot(stack, uincl_s[...])
    cs = cs[0:8] + cs[8:16] + cs[16:24]
    carry = carry_s[...]
    cT_out[0] = cs + carry[:, 0:1]
    tot = _dot(stack, jnp.ones((tm, LANES), BF16))
    carry_s[...] = carry + tot[0:8] + tot[8:16] + tot[16:24]

    nsub = GLA_CHUNK // GLA_SUB
    rowi = lax.broadcasted_iota(jnp.int32, (GLA_CHUNK, gla_kw), 0)
    lanei = lax.broadcasted_iota(jnp.int32, (GLA_CHUNK, gla_kw), 1)
    ar = lax.broadcasted_iota(jnp.int32, (GLA_CHUNK, GLA_CHUNK), 0)
    ac = lax.broadcasted_iota(jnp.int32, (GLA_CHUNK, GLA_CHUNK), 1)
    tri_blk = jnp.logical_and(ar // GLA_SUB == ac // GLA_SUB, ar >= ac)
    br = lax.broadcasted_iota(jnp.int32, (gla_vw, gla_kw), 0)
    bc = lax.broadcasted_iota(jnp.int32, (gla_vw, gla_kw), 1)
    bd_mask = (br // dv) == (bc // dk)

    def chunk_body(ci, _):
        r0 = ci * GLA_CHUNK
        la_c = la_s[pl.ds(r0, GLA_CHUNK), :]
        gq_c = gq_s[pl.ds(r0, GLA_CHUNK), :]
        gk_c = gk_s[pl.ds(r0, GLA_CHUNK), :]
        gv_c = gv_s[pl.ds(r0, GLA_CHUNK), :]
        gg_c = gg_s[pl.ds(r0, GLA_CHUNK), :]
        h3, m3, l3 = _split3(la_c)
        st = jnp.concatenate([h3, m3, l3], axis=1).astype(BF16)
        bb = _dot(lincl_s[...], st)
        b = bb[:, 0:gla_kw] + bb[:, gla_kw:2 * gla_kw] + bb[:, 2 * gla_kw:3 * gla_kw]
        bmid_l, bend_l, b0_l = [], [], []
        for i in range(nsub):
            s0 = i * GLA_SUB
            bmid_l.append(b[s0 + GLA_SUB // 2:s0 + GLA_SUB // 2 + 1])
            bend_l.append(b[s0 + GLA_SUB - 1:s0 + GLA_SUB])
            b0_l.append(jnp.zeros((1, gla_kw), F32) if i == 0 else b[s0 - 1:s0])
        bc_rows = lambda rows: jnp.concatenate(
            [jnp.broadcast_to(r, (GLA_SUB, gla_kw)) for r in rows], axis=0)
        bmid, bend, b0 = bc_rows(bmid_l), bc_rows(bend_l), bc_rows(b0_l)
        qt = (gq_c * jnp.exp(b - bmid)).astype(BF16)
        kt = (gk_c * jnp.exp(bmid - b)).astype(BF16)
        qp = gq_c * jnp.exp(b - b0)
        kd = gk_c * jnp.exp(bend - b)
        gv_bf = gv_c.astype(BF16)
        gvT_bf = gv_c.T.astype(BF16)

        rms = [jnp.logical_and(rowi >= i * GLA_SUB, rowi < (i + 1) * GLA_SUB) for i in range(nsub)]
        uts = [_dot(gvT_bf, jnp.where(rms[i], kd, 0.0).astype(BF16)) for i in range(nsub)]
        states = [bdt_s[...]]
        for i in range(nsub):
            decay = jnp.exp(bend_l[i] - b0_l[i])
            states.append(states[i] * decay + jnp.where(bd_mask, uts[i], 0.0))
        bdt_s[...] = states[nsub]
        o_inter = None
        for i in range(nsub):
            d = _dot_nt(jnp.where(rms[i], qp, 0.0).astype(BF16), states[i].astype(BF16))
            o_inter = d if o_inter is None else o_inter + d

        for h in range(n_heads_gla):
            hm = jnp.logical_and(lanei >= h * dk, lanei < (h + 1) * dk)
            a = _dot_nt(jnp.where(hm, qt, jnp.zeros_like(qt)), kt)
            a = jnp.where(tri_blk, a, 0.0).astype(BF16)
            o_h = _dot(a, gv_bf[:, h * dv:(h + 1) * dv]) + o_inter[:, h * dv:(h + 1) * dv]
            o_n = _rms(o_h, gnorm_ref[...])
            go = o_n * _silu(gg_c[:, h * dv:(h + 1) * dv])
            go_out[0, pl.ds(r0, GLA_CHUNK), h * dv:(h + 1) * dv] = go.astype(BF16)
        return 0

    for ci in range(tm // GLA_CHUNK):
        chunk_body(ci, 0)

    @pl.when(t == nt - 1)
    def _():
        bd = bdt_s[...].T
        for h in range(n_heads_gla):
            sfin_out[0, h] = bd[h * dk:(h + 1) * dk, h * dv:(h + 1) * dv]


def _attn_prompt_kernel(qT_ref, kaug_ref, vT_ref, c_ref, oT_ref, vaug_s, qa_s, m_s, acc_s, kn_s, ct_s, *, t_blk, dh):
    i = pl.program_id(2)
    hw2 = 2 * dh
    n_t = kn_s.shape[1]

    @pl.when(i == 0)
    def _():
        lane_k = lax.broadcasted_iota(jnp.int32, (t_blk, hw2), 1)
        for h in range(2):
            vaug_s[h, 0:dh, :] = vT_ref[0, h * dh:(h + 1) * dh, :].astype(BF16)
            vaug_s[h, dh:, :] = jnp.ones((vaug_s.shape[1] - dh, vaug_s.shape[2]), BF16)
            own_l = jnp.logical_and(lane_k >= h * dh, lane_k < (h + 1) * dh)

            def tile_stats(kb, carry):
                k0 = pl.multiple_of(kb * t_blk, t_blk)
                kt = jnp.where(own_l, kaug_ref[0, 0, h, pl.ds(k0, t_blk), :].astype(F32), 0.0)
                kn_s[h, kb] = jnp.sqrt(jnp.max(jnp.sum(kt * kt, axis=1, keepdims=True)))
                ct_s[h, kb] = jnp.max(c_ref[0, 0, h:h + 1, pl.ds(k0, LANES)][:, 0:1])
                return carry

            lax.fori_loop(0, n_t, tile_stats, 0)

    qT = qT_ref[0]
    rowq = lax.broadcasted_iota(jnp.int32, (hw2, t_blk), 0)
    keyi = lax.broadcasted_iota(jnp.int32, (ATT_KS, t_blk), 0)
    qryi = lax.broadcasted_iota(jnp.int32, (ATT_KS, t_blk), 1)
    t0 = pl.multiple_of(i * t_blk, t_blk)
    c_q = []
    q_norm = []
    for h in range(2):
        spare = (1 - h) * dh
        own_r = jnp.logical_and(rowq >= h * dh, rowq < (h + 1) * dh)
        ones_r = jnp.logical_and(rowq >= spare, rowq < spare + N_AUG)
        qa_s[h] = jnp.where(own_r, qT, jnp.where(ones_r, 1.0, 0.0).astype(BF16))
        m_s[h] = jnp.full(m_s.shape[1:], -jnp.inf, F32)
        acc_s[h] = jnp.zeros(acc_s.shape[1:], F32)
        c_q.append(c_ref[0, 0, h:h + 1, pl.ds(t0, LANES)][:, 0:1])
        qf = jnp.where(own_r, qT.astype(F32), 0.0)
        q_norm.append(jnp.sqrt(jnp.max(jnp.sum(qf * qf, axis=0, keepdims=True))))

    n_sub = t_blk // ATT_KS

    def steps(tiles):
        work = []
        for j, masked in tiles:
            k0 = pl.multiple_of(j * t_blk, t_blk)
            kks = [pl.multiple_of(k0 + ks * ATT_KS, ATT_KS) for ks in range(n_sub)]
            s_all = [[_dot(kaug_ref[0, 0, h, pl.ds(kks[ks], ATT_KS), :], qa_s[h]) for ks in range(n_sub)]
                     for h in range(2)]
            work.append((k0, kks, s_all, masked))
        for h in range(2):
            m_run = m_s[h]
            acc = acc_s[h]
            for k0, kks, s_all, masked in work:
                off = (c_ref[0, 0, h:h + 1, pl.ds(k0, LANES)][:, 0:1] - c_q[h]) * LOG2E
                for ks in range(n_sub):
                    s = s_all[h][ks]
                    if masked:
                        s = jnp.where(keyi + ks * ATT_KS <= qryi, s, -jnp.inf)
                    m_new = jnp.maximum(m_run, jnp.max(s, axis=0, keepdims=True) - off)
                    p = jnp.exp2(s - (m_new + off))
                    alpha = jnp.exp2(m_run - m_new)
                    pv = _dot(vaug_s[h, :, pl.ds(kks[ks], ATT_KS)], p.astype(BF16))
                    acc = alpha * acc + pv
                    m_run = m_new
            acc_s[h] = acc
            m_s[h] = m_run

    odd = jnp.bitwise_and(i, 1) == 1

    @pl.when(odd)
    def _():
        steps([(i, True), (i - 1, False)])

    @pl.when(jnp.logical_not(odd))
    def _():
        steps([(i, True)])

    m_low = [jnp.min(m_s[h]) for h in range(2)]
    top = i - 1 - jnp.bitwise_and(i, 1)

    def pair_body(jj, carry):
        hi = top - 2 * jj
        live = None
        for h in range(2):
            for t in (hi, hi - 1):
                bound = q_norm[h] * kn_s[h, t] * ATT_NORM_SLACK - (ct_s[h, t + 1] - ct_s[h, i]) * LOG2E
                alive = bound - m_low[h] > -ATT_SKIP_BITS
                live = alive if live is None else jnp.logical_or(live, alive)

        @pl.when(live)
        def _():
            steps([(hi, False), (hi - 1, False)])

        return carry

    lax.fori_loop(0, lax.shift_right_logical(i, 1), pair_body, 0)

    outs = [acc_s[h][0:dh, :] / acc_s[h][dh:dh + 1, :] for h in range(2)]
    oT_ref[0] = jnp.concatenate(outs, axis=0).astype(BF16)


def _ffn_kernel(x_ref, foT_ref, go_ref, wo_ref, wup_ref, wdn_ref, g2_ref, gf_ref, y_ref, u_s, *, fox_w, d_ff):
    x = x_ref[0]
    h = x + (_dot(foT_ref[0].T, wo_ref[0:fox_w, :]) + _dot(go_ref[0], wo_ref[fox_w:, :]))
    hn = _rms(h, g2_ref[...]).astype(BF16)
    for c in range(d_ff // FFN_CHUNK):
        u = _dot(hn, wup_ref[:, c * FFN_CHUNK:(c + 1) * FFN_CHUNK])
        u_s[:, c * FFN_CHUNK:(c + 1) * FFN_CHUNK] = jnp.square(jnp.maximum(u, 0.0)).astype(BF16)
    y_ref[0] = _rms(h + _dot(u_s[...], wdn_ref[...]), gf_ref[...])


def _ffn_call(x3, foT, go, wo, wup, wdn, g2, gf, tm):
    nb, n, d = x3.shape
    fox_w = foT.shape[1]
    d_ff = wup.shape[1]
    const = lambda shape: pl.BlockSpec(shape, lambda b, i: (0, 0), pipeline_mode=pl.Buffered(1))
    return pl.pallas_call(
        functools.partial(_ffn_kernel, fox_w=fox_w, d_ff=d_ff),
        grid=(nb, n // tm),
        in_specs=[
            pl.BlockSpec((1, tm, d), lambda b, i: (b, i, 0)),
            pl.BlockSpec((1, fox_w, tm), lambda b, i: (b, 0, i)),
            pl.BlockSpec((1, tm, go.shape[2]), lambda b, i: (b, i, 0)),
            const(wo.shape), const(wup.shape), const(wdn.shape), const(g2.shape), const(gf.shape),
        ],
        out_specs=pl.BlockSpec((1, tm, d), lambda b, i: (b, i, 0)),
        out_shape=jax.ShapeDtypeStruct((nb, n, d), F32),
        scratch_shapes=[pltpu.VMEM((tm, d_ff), BF16)],
        compiler_params=pltpu.CompilerParams(dimension_semantics=("arbitrary", "arbitrary"),
                                             vmem_limit_bytes=VMEM_LIMIT_BYTES),
        name="merge_ffn",
    )(x3, foT, go, wo, wup, wdn, g2, gf)


def _proj_sample_kernel(x_ref, g1_ref, wtok_ref, wdm_ref, bf_ref, wgate_ref, bgate_ref,
                        q_out, k_out, v_out, lf_out, gq_out, gk_out, gv_out, gg_out, la_out,
                        *, fox_w, fox_dh, gla_kw, gla_vw, n_heads_gla, n_heads_fox):
    dk = gla_kw // n_heads_gla
    xn = _rms(x_ref[...], g1_ref[...]).astype(BF16)
    z = _dot_nt(xn, wtok_ref[...])
    o0 = fox_w
    gq_out[...] = z[:, o0:o0 + gla_kw] * (dk ** -0.5)
    o0 += gla_kw
    gk_out[...] = z[:, o0:o0 + gla_kw]
    o0 += gla_kw
    gv_out[...] = z[:, o0:o0 + gla_vw]
    o0 += gla_vw
    gg_out[...] = z[:, o0:o0 + gla_vw]
    o0 += gla_vw
    glr = z[:, o0:o0 + LANES].astype(BF16)
    la_out[...] = _log_sigmoid(_dot(glr, wgate_ref[...]) + bgate_ref[...]) * (1.0 / GLA_GATE_NORM)
    z2 = _dot_nt(xn, wdm_ref[...])
    q_out[...] = z2[:, 0:fox_w] * (fox_dh ** -0.5)
    k_out[...] = z2[:, fox_w:2 * fox_w]
    v_out[...] = z2[:, 2 * fox_w:3 * fox_w]
    lf_out[...] = _log_sigmoid(z2[:, 3 * fox_w:3 * fox_w + n_heads_fox] + bf_ref[...])


def _ffn_decode_kernel(pt_ref, x_ref, foT_ref, go_ref, wo_ref, wup_ref, wdn_ref, g2_ref, gf_ref,
                       q_ref, knew_ref, vnew_ref, lfnew_ref, kc_hbm, vc_hbm, lfc_hbm,
                       y_ref, o_ref,
                       u_s, ring, lfbuf, rev_s, zbuf, acc_s, qb_s, ustrict_s, live_s, sem_ring, sem_lf,
                       *, fox_w, d_ff, n_pages, n_b, n_heads, dh):
    b = pl.program_id(0) * pl.num_programs(1) + pl.program_id(1)
    page = LANES
    hw = n_heads * dh
    nch = n_pages // DEC_G
    per_b = 2 * nch
    total = n_b * per_b

    def start_chunk(g):
        bg = g // per_b
        c = g - bg * per_b
        slot = lax.rem(g, DEC_NSLOT)

        @pl.when(c < nch)
        def _():
            for j in range(DEC_G):
                p = n_pages - 1 - (c * DEC_G + j)
                pltpu.make_async_copy(kc_hbm.at[pt_ref[bg, p]], ring.at[slot, j], sem_ring.at[slot]).start()

        @pl.when(c >= nch)
        def _():
            for j in range(DEC_G):
                p = n_pages - 1 - ((c - nch) * DEC_G + j)
                pltpu.make_async_copy(vc_hbm.at[pt_ref[bg, p]], ring.at[slot, j], sem_ring.at[slot]).start()

    def wait_chunk(g):
        slot = lax.rem(g, DEC_NSLOT)
        for j in range(DEC_G):
            pltpu.make_async_copy(kc_hbm.at[0], ring.at[slot, j], sem_ring.at[slot]).wait()

    def start_lf(bb):
        sl = lax.rem(bb, 2)

        def body(p, _):
            pltpu.make_async_copy(lfc_hbm.at[pt_ref[bb, p]], lfbuf.at[sl, p], sem_lf.at[sl]).start()
            return 0

        lax.fori_loop(0, n_pages, body, 0)

    def wait_lf(bb):
        sl = lax.rem(bb, 2)

        def body(p, _):
            pltpu.make_async_copy(lfc_hbm.at[0], lfbuf.at[sl, p], sem_lf.at[sl]).wait()
            return 0

        lax.fori_loop(0, n_pages, body, 0)

    g0 = b * per_b

    @pl.when(b == 0)
    def _():
        r = lax.broadcasted_iota(jnp.int32, (page, page), 0)
        c = lax.broadcasted_iota(jnp.int32, (page, page), 1)
        ustrict_s[...] = jnp.where(r > c, 1.0, 0.0).astype(BF16)
        start_lf(b)
        for g in range(DEC_NSLOT - 1):
            start_chunk(g0 + g)

    wait_lf(b)

    @pl.when(b + 1 < n_b)
    def _():
        start_lf(b + 1)

    sl = lax.rem(b, 2)
    lf2d = lfbuf[sl].reshape(n_pages * n_heads, page)
    hi, mid, lo = _split3(lf2d)
    u = ustrict_s[...]
    rev = _dot(hi.astype(BF16), u) + _dot(mid.astype(BF16), u) + _dot(lo.astype(BF16), u)
    rev_s[...] = rev.reshape(n_pages, n_heads, page)

    qb = jnp.broadcast_to(q_ref[0], (hw, page))
    qb_s[...] = qb

    def head_sum(x):
        return jnp.sum(x.reshape(n_heads, dh, page), axis=1)

    def head_bcast(x):
        return jnp.broadcast_to(x[:, None, :], (n_heads, dh, page)).reshape(hw, page)

    def k_body(c, carry):
        g = g0 + c

        @pl.when(g + (DEC_NSLOT - 1) < total)
        def _():
            start_chunk(g + (DEC_NSLOT - 1))

        wait_chunk(g)
        slot = lax.rem(g, DEC_NSLOT)
        rows = []
        for hd in range(n_heads):
            hs = slice(hd * dh, (hd + 1) * dh)
            qh = qb_s[hs, :]
            ch = carry[hd:hd + 1]
            for j in range(DEC_G):
                p = n_pages - 1 - (c * DEC_G + j)
                s = jnp.sum(ring[slot, j, hs, :] * qh, axis=0, keepdims=True)
                revp = rev_s[p, hd:hd + 1, :]
                zbuf[hd:hd + 1, pl.ds(pl.multiple_of(p * page, page), page)] = s + revp + ch
                ch = ch + jnp.broadcast_to(revp[:, 0:1] + lfbuf[sl, p, hd:hd + 1, 0:1], (1, page))
            rows.append(ch)
        return jnp.concatenate(rows, axis=0)

    x = x_ref[0]
    h = x + (_dot(foT_ref[0].T, wo_ref[0:fox_w, :]) + _dot(go_ref[0], wo_ref[fox_w:, :]))
    hn = _rms(h, g2_ref[...]).astype(BF16)

    n_ffn = d_ff // FFN_CHUNK
    grp = nch // n_ffn
    carry = jnp.broadcast_to(lfnew_ref[0], (n_heads, page))
    for c in range(n_ffn):
        carry = lax.fori_loop(c * grp, (c + 1) * grp, k_body, carry)
        u = _dot(hn, wup_ref[:, c * FFN_CHUNK:(c + 1) * FFN_CHUNK])
        u_s[:, c * FFN_CHUNK:(c + 1) * FFN_CHUNK] = jnp.square(jnp.maximum(u, 0.0)).astype(BF16)

    z_all = zbuf[...]
    z_new = head_sum(jnp.broadcast_to(knew_ref[0], (hw, page)) * qb)
    m = jnp.maximum(jnp.max(z_all, axis=1, keepdims=True), z_new[:, 0:1])
    p_all = jnp.exp(z_all - m)
    zbuf[...] = p_all
    p_new = jnp.exp(z_new - m)
    l = jnp.sum(p_all, axis=1, keepdims=True) + p_new

    acc_s[...] = jnp.zeros_like(acc_s)

    span = DEC_G * page
    for vc in range(nch):
        lo = (n_pages - (vc + 1) * DEC_G) * page
        live_s[vc] = jnp.where(jnp.max(p_all[:, lo:lo + span]) > 0.0, 1, 0).astype(jnp.int32)

    def v_body(c, carry):
        g = g0 + nch + c
        nxt = c + (DEC_NSLOT - 1)
        nxt_live = jnp.logical_or(nxt >= nch, live_s[jnp.minimum(nxt, nch - 1)] != 0)

        @pl.when(jnp.logical_and(g + (DEC_NSLOT - 1) < total, nxt_live))
        def _():
            start_chunk(g + (DEC_NSLOT - 1))

        @pl.when(jnp.logical_or(c < DEC_NSLOT - 1, live_s[c] != 0))
        def _():
            wait_chunk(g)
            slot = lax.rem(g, DEC_NSLOT)
            for hd in range(n_heads):
                hs = slice(hd * dh, (hd + 1) * dh)
                acc = acc_s[hs, :]
                for j in range(DEC_G):
                    p = n_pages - 1 - (c * DEC_G + j)
                    pp = zbuf[hd:hd + 1, pl.ds(pl.multiple_of(p * page, page), page)]
                    acc = acc + ring[slot, j, hs, :] * jnp.broadcast_to(pp, (dh, page))
                acc_s[hs, :] = acc

        return carry

    down = None
    for c in range(n_ffn):
        lax.fori_loop(c * grp, (c + 1) * grp, v_body, 0)
        d = _dot(u_s[:, c * FFN_CHUNK:(c + 1) * FFN_CHUNK], wdn_ref[c * FFN_CHUNK:(c + 1) * FFN_CHUNK, :])
        down = d if down is None else down + d
    y_ref[0] = _rms(h + down, gf_ref[...])

    num = jnp.sum(acc_s[...], axis=1, keepdims=True) + head_bcast(p_new) * jnp.broadcast_to(vnew_ref[0], (hw, page))
    o_ref[0] = (num / head_bcast(l))[:, 0:1]


def _gla_sample_kernel(s_ref, la_ref, k_ref, q_ref, v_ref, gg_ref, gnorm_ref, s_out, go_out, *, n_heads, dk, dv):
    for i in range(s_ref.shape[0]):
        for h in range(n_heads):
            la = la_ref[i, h * dk:(h + 1) * dk, :]
            kk = k_ref[i, h * dk:(h + 1) * dk, :]
            qq = q_ref[i, h * dk:(h + 1) * dk, :]
            vv = v_ref[i, :, h * dv:(h + 1) * dv]
            s_new = s_ref[i, h] * jnp.exp(la) + kk * vv
            s_out[i, h] = s_new
            o = jnp.sum(qq * s_new, axis=0, keepdims=True)
            o_n = _rms(o, gnorm_ref[...])
            go_out[i, :, h * dv:(h + 1) * dv] = o_n * _silu(gg_ref[i, :, h * dv:(h + 1) * dv])


def kernel(x_prompt, x_sample, cache_k, cache_v, cache_logf, state_gla, page_table, norm1_g, w_in, fox_b_f,
           gla_w_gate_up, gla_b_gate, gla_norm_g, w_o, norm2_g, w_up, w_down, final_g):
    B, S, D = x_prompt.shape
    Bd = x_sample.shape[0]
    depth, n_phys, page, H, dh = cache_k.shape
    _, _, Hg, dk, dv = state_gla.shape
    assert depth == 1 and x_sample.shape[1] == 1 and page == LANES
    fox_w = H * dh
    gla_kw = Hg * dk
    gla_vw = Hg * dv
    rank = gla_w_gate_up.shape[1]
    n_pages = page_table.shape[1]

    wt = jnp.transpose(w_in[0])
    o_fq, o_fk = 0, fox_w
    o_ff = 3 * fox_w
    o_gq = o_ff + H
    o_gk = o_gq + gla_kw
    o_gv = o_gk + gla_kw
    o_glr = o_gv + gla_vw
    o_gg = o_glr + rank
    misc_pad = LANES - FF_LANE0 - H
    w_tok = jnp.concatenate([
        wt[o_fk:o_fk + fox_w], wt[o_gq:o_gq + gla_kw], wt[o_gk:o_gk + gla_kw], wt[o_gv:o_gv + gla_vw],
        wt[o_gg:o_gg + gla_vw], wt[o_glr:o_glr + rank], jnp.zeros((FF_LANE0 - rank, D), F32),
        wt[o_ff:o_ff + H], jnp.zeros((misc_pad, D), F32)], axis=0).astype(BF16)
    w_dm = jnp.concatenate([
        wt[o_fq:o_fq + 3 * fox_w], wt[o_ff:o_ff + H], jnp.zeros((2 * SUBLANES - H, D), F32)], axis=0).astype(BF16)
    bf_row = jnp.concatenate([jnp.zeros((1, FF_LANE0), F32), fox_b_f.reshape(1, H), jnp.zeros((1, misc_pad), F32)],
                             axis=1)
    w_gate = jnp.concatenate([gla_w_gate_up[0], jnp.zeros((LANES - rank, gla_kw), F32)], axis=0).astype(BF16)
    wo_bf = w_o[0].astype(BF16)
    wup_bf = w_up[0].astype(BF16)
    wdn_bf = w_down[0].astype(BF16)
    g1 = norm1_g.reshape(1, D)
    g2 = norm2_g.reshape(1, D)
    gf = final_g.reshape(1, D)
    bgate = gla_b_gate.reshape(1, gla_kw)
    gnorm = gla_norm_g.reshape(1, dv)

    cparams = lambda sem: pltpu.CompilerParams(dimension_semantics=sem, vmem_limit_bytes=VMEM_LIMIT_BYTES)
    single = lambda shape: pl.BlockSpec(shape, lambda *_: (0,) * len(shape), pipeline_mode=pl.Buffered(1))

    tm = PROJ_TM
    nt = S // tm
    n_tok_cols = w_tok.shape[0]
    n_dm_rows = w_dm.shape[0]
    qT_p, kT_p, vT_p, lfT_p, cT_p, kaug_p, go_p, sfin_p = pl.pallas_call(
        functools.partial(_proj_prompt_kernel, tm=tm, fox_w=fox_w, fox_dh=dh, gla_kw=gla_kw, gla_vw=gla_vw,
                          n_heads_gla=Hg),
        grid=(B, nt),
        in_specs=[
            pl.BlockSpec((1, tm, D), lambda b, t: (b, t, 0)),
            single((1, D)), single((n_tok_cols, D)), single((n_dm_rows, D)), single((H, 1)), single((1, LANES)),
            single((LANES, gla_kw)), single((1, gla_kw)), single((1, dv)),
        ],
        out_specs=[
            pl.BlockSpec((1, fox_w, tm), lambda b, t: (b, 0, t)),
            pl.BlockSpec((1, fox_w, tm), lambda b, t: (b, 0, t)),
            pl.BlockSpec((1, fox_w, tm), lambda b, t: (b, 0, t)),
            pl.BlockSpec((1, H, tm), lambda b, t: (b, 0, t)),
            pl.BlockSpec((1, H, tm), lambda b, t: (b, 0, t)),
            pl.BlockSpec((1, H, tm, LANES), lambda b, t: (b, 0, t, 0)),
            pl.BlockSpec((1, tm, gla_vw), lambda b, t: (b, t, 0)),
            pl.BlockSpec((1, Hg, dk, dv), lambda b, t: (b, 0, 0, 0)),
        ],
        out_shape=[
            jax.ShapeDtypeStruct((B, fox_w, S), BF16),
            jax.ShapeDtypeStruct((B, fox_w, S), F32),
            jax.ShapeDtypeStruct((B, fox_w, S), F32),
            jax.ShapeDtypeStruct((B, H, S), F32),
            jax.ShapeDtypeStruct((B, H, S), F32),
            jax.ShapeDtypeStruct((B, H, S, LANES), BF16),
            jax.ShapeDtypeStruct((B, S, gla_vw), BF16),
            jax.ShapeDtypeStruct((B, Hg, dk, dv), F32),
        ],
        scratch_shapes=[
            pltpu.VMEM((H, LANES), F32),
            pltpu.VMEM((gla_vw, gla_kw), F32),
            pltpu.VMEM((tm, tm), BF16),
            pltpu.VMEM((tm, tm), BF16),
            pltpu.VMEM((GLA_CHUNK, GLA_CHUNK), BF16),
            pltpu.VMEM((tm, gla_kw), F32), pltpu.VMEM((tm, gla_kw), F32),
            pltpu.VMEM((tm, gla_vw), F32), pltpu.VMEM((tm, gla_vw), F32), pltpu.VMEM((tm, gla_kw), F32),
        ],
        compiler_params=cparams(("arbitrary", "arbitrary")),
        name="proj_gla_prompt",
    )(x_prompt, g1, w_tok, w_dm, fox_b_f.reshape(H, 1), bf_row, w_gate, bgate, gnorm)

    t_blk = ATT_T
    assert t_blk == tm
    nq = S // t_blk
    n_pairs = fox_w // LANES
    c4 = cT_p.reshape(B, n_pairs, 2, S)
    k5 = kaug_p.reshape(B, n_pairs, 2, S, LANES)
    n_vrows = dh + 2 * SUBLANES
    foT_p = pl.pallas_call(
        functools.partial(_attn_prompt_kernel, t_blk=t_blk, dh=dh),
        grid=(B, n_pairs, nq),
        in_specs=[
            pl.BlockSpec((1, LANES, t_blk), lambda b, p, i: (b, p, i)),
            pl.BlockSpec((1, 1, 2, S, LANES), lambda b, p, i: (b, p, 0, 0, 0)),
            pl.BlockSpec((1, LANES, S), lambda b, p, i: (b, p, 0)),
            pl.BlockSpec((1, 1, 2, S), lambda b, p, i: (b, p, 0, 0)),
        ],
        out_specs=pl.BlockSpec((1, LANES, t_blk), lambda b, p, i: (b, p, i)),
        out_shape=jax.ShapeDtypeStruct((B, fox_w, S), BF16),
        scratch_shapes=[
            pltpu.VMEM((2, n_vrows, S), BF16),
            pltpu.VMEM((2, LANES, t_blk), BF16),
            pltpu.VMEM((2, 1, t_blk), F32),
            pltpu.VMEM((2, n_vrows, t_blk), F32),
            pltpu.SMEM((2, nq), F32),
            pltpu.SMEM((2, nq), F32),
        ],
        compiler_params=cparams(("arbitrary", "arbitrary", "arbitrary")),
        name="fox_attn_prompt",
    )(qT_p, k5, vT_p, c4)

    xs = x_sample.reshape(Bd, D)
    full = lambda shape: pl.BlockSpec(shape, lambda: (0,) * len(shape))
    s_shapes = [(Bd, fox_w), (Bd, fox_w), (Bd, fox_w), (Bd, H), (Bd, gla_kw), (Bd, gla_kw), (Bd, gla_vw),
                (Bd, gla_vw), (Bd, gla_kw)]
    q_s, k_s, v_s, lf_s, gq_s, gk_s, gv_s, gg_s, la_s = pl.pallas_call(
        functools.partial(_proj_sample_kernel, fox_w=fox_w, fox_dh=dh, gla_kw=gla_kw, gla_vw=gla_vw, n_heads_gla=Hg,
                          n_heads_fox=H),
        in_specs=[full((Bd, D)), full((1, D)), full((n_tok_cols, D)), full((n_dm_rows, D)), full((1, H)),
                  full((LANES, gla_kw)), full((1, gla_kw))],
        out_specs=[full(s) for s in s_shapes],
        out_shape=[jax.ShapeDtypeStruct(s, F32) for s in s_shapes],
        compiler_params=pltpu.CompilerParams(vmem_limit_bytes=VMEM_LIMIT_BYTES),
        name="proj_sample",
    )(xs, g1, w_tok, w_dm, fox_b_f.reshape(1, H), w_gate, bgate)

    kc = jnp.transpose(cache_k[0], (0, 2, 3, 1)).reshape(n_phys, fox_w, page)
    vc = jnp.transpose(cache_v[0], (0, 2, 3, 1)).reshape(n_phys, fox_w, page)
    lfc = jnp.transpose(cache_logf[0], (0, 2, 1))
    col = lambda a: a.reshape(Bd, a.shape[1], 1)
    ftm = FFN_TM
    n_ft = S // ftm
    assert B * n_ft == Bd
    d_ff = wup_bf.shape[1]
    assert (n_pages // DEC_G) % (d_ff // FFN_CHUNK) == 0
    wconst = lambda shape: pl.BlockSpec(shape, lambda b, i, pt: (0, 0), pipeline_mode=pl.Buffered(1))
    seq = lambda b, i, pt: (b * n_ft + i, 0, 0)
    grid_spec = pltpu.PrefetchScalarGridSpec(
        num_scalar_prefetch=1,
        grid=(B, n_ft),
        in_specs=[
            pl.BlockSpec((1, ftm, D), lambda b, i, pt: (b, i, 0)),
            pl.BlockSpec((1, fox_w, ftm), lambda b, i, pt: (b, 0, i)),
            pl.BlockSpec((1, ftm, gla_vw), lambda b, i, pt: (b, i, 0)),
            wconst(wo_bf.shape), wconst(wup_bf.shape), wconst(wdn_bf.shape), wconst(g2.shape), wconst(gf.shape),
            pl.BlockSpec((1, fox_w, 1), seq), pl.BlockSpec((1, fox_w, 1), seq), pl.BlockSpec((1, fox_w, 1), seq),
            pl.BlockSpec((1, H, 1), seq),
            pl.BlockSpec(memory_space=pl.ANY), pl.BlockSpec(memory_space=pl.ANY), pl.BlockSpec(memory_space=pl.ANY),
        ],
        out_specs=[
            pl.BlockSpec((1, ftm, D), lambda b, i, pt: (b, i, 0)),
            pl.BlockSpec((1, fox_w, 1), seq),
        ],
        scratch_shapes=[
            pltpu.VMEM((ftm, d_ff), BF16),
            pltpu.VMEM((DEC_NSLOT, DEC_G, fox_w, page), F32),
            pltpu.VMEM((2, n_pages, H, page), F32),
            pltpu.VMEM((n_pages, H, page), F32),
            pltpu.VMEM((H, n_pages * page), F32),
            pltpu.VMEM((fox_w, page), F32),
            pltpu.VMEM((fox_w, page), F32),
            pltpu.VMEM((page, page), BF16),
            pltpu.SMEM((n_pages // DEC_G,), jnp.int32),
            pltpu.SemaphoreType.DMA((DEC_NSLOT,)),
            pltpu.SemaphoreType.DMA((2,)),
        ],
    )
    y_p, fo_s = pl.pallas_call(
        functools.partial(_ffn_decode_kernel, fox_w=fox_w, d_ff=d_ff, n_pages=n_pages, n_b=Bd, n_heads=H, dh=dh),
        grid_spec=grid_spec,
        out_shape=[jax.ShapeDtypeStruct((B, S, D), F32), jax.ShapeDtypeStruct((Bd, fox_w, 1), F32)],
        compiler_params=cparams(("arbitrary", "arbitrary")),
        name="ffn_prompt_fox_decode",
    )(page_table, x_prompt, foT_p, go_p, wo_bf, wup_bf, wdn_bf, g2, gf,
      col(q_s), col(k_s), col(v_s), col(lf_s), kc, vc, lfc)
    foT_s = fo_s.reshape(Bd, fox_w).T.astype(BF16)

    gb = GLA_SAMPLE_BLOCK
    s_new, go_s = pl.pallas_call(
        functools.partial(_gla_sample_kernel, n_heads=Hg, dk=dk, dv=dv),
        grid=(Bd // gb,),
        in_specs=[
            pl.BlockSpec((gb, Hg, dk, dv), lambda b: (b, 0, 0, 0)),
            pl.BlockSpec((gb, gla_kw, 1), lambda b: (b, 0, 0)),
            pl.BlockSpec((gb, gla_kw, 1), lambda b: (b, 0, 0)),
            pl.BlockSpec((gb, gla_kw, 1), lambda b: (b, 0, 0)),
            pl.BlockSpec((gb, 1, gla_vw), lambda b: (b, 0, 0)),
            pl.BlockSpec((gb, 1, gla_vw), lambda b: (b, 0, 0)),
            pl.BlockSpec((1, dv), lambda b: (0, 0)),
        ],
        out_specs=[
            pl.BlockSpec((gb, Hg, dk, dv), lambda b: (b, 0, 0, 0)),
            pl.BlockSpec((gb, 1, gla_vw), lambda b: (b, 0, 0)),
        ],
        out_shape=[jax.ShapeDtypeStruct((Bd, Hg, dk, dv), F32), jax.ShapeDtypeStruct((Bd, 1, gla_vw), F32)],
        compiler_params=cparams(("arbitrary",)),
        name="gla_sample",
    )(state_gla[0], col(la_s), col(gk_s), col(gq_s), gv_s.reshape(Bd, 1, gla_vw), gg_s.reshape(Bd, 1, gla_vw), gnorm)
    go_s = go_s.reshape(Bd, gla_vw).astype(BF16)

    y_s = _ffn_call(xs[None], foT_s[None], go_s[None], wo_bf, wup_bf, wdn_bf, g2, gf, Bd).reshape(Bd, 1, D)

    new_k_p = jnp.transpose(kT_p.reshape(1, B, H, dh, S), (0, 1, 4, 2, 3))
    new_v_p = jnp.transpose(vT_p.reshape(1, B, H, dh, S), (0, 1, 4, 2, 3))
    new_lf_p = jnp.transpose(lfT_p, (0, 2, 1)).reshape(1, B, S, H)
    return (y_p, y_s, new_k_p, new_v_p, new_lf_p, sfin_p.reshape(1, B, Hg, dk, dv),
            k_s.reshape(1, Bd, 1, H, dh), v_s.reshape(1, Bd, 1, H, dh), lf_s.reshape(1, Bd, 1, H),
            s_new.reshape(1, Bd, Hg, dk, dv))
```

```python
import functools

import jax
import jax.numpy as jnp
from jax import lax
from jax.experimental import pallas as pl
from jax.experimental.pallas import tpu as pltpu

F32 = jnp.float32
BF16 = jnp.bfloat16

LANES = 128
SUBLANES = 8
VMEM_LIMIT_BYTES = 56 * 1024 * 1024

EPS = 1e-6
LOG2E = 1.4426950408889634
N_AUG = 3
FF_LANE0 = 16
GLA_GATE_NORM = 16.0
GLA_CHUNK = 128
GLA_SUB = 32
PROJ_TM = 512
ATT_T = 512
ATT_KS = 256
FFN_TM = 512
FFN_CHUNK = 1024
DEC_G = 8
GLA_SAMPLE_BLOCK = 8
DEC_NSLOT = 4


def _dot(a, b):
    return jnp.dot(a, b, preferred_element_type=F32)


def _dot_nt(a, b):
    return lax.dot_general(a, b, (((1,), (1,)), ((), ())), preferred_element_type=F32)


def _split3(x):
    hi = x.astype(BF16).astype(F32)
    r = x - hi
    mid = r.astype(BF16).astype(F32)
    lo = r - mid
    return hi, mid, lo


def _log_sigmoid(x):
    return jnp.minimum(x, 0.0) - jnp.log1p(jnp.exp(-jnp.abs(x)))


def _silu(x):
    return x / (1.0 + jnp.exp(-x))


def _rms(x, g):
    return x * lax.rsqrt(jnp.mean(x * x, axis=-1, keepdims=True) + EPS) * g


def _proj_prompt_kernel(x_ref, g1_ref, wtok_ref, wdm_ref, bf_ref, bfrow_ref, wgate_ref, bgate_ref, gnorm_ref,
                        qT_out, kT_out, vT_out, lfT_out, cT_out, kaug_out, go_out, sfin_out,
                        carry_s, bdt_s, uincl_s, ltm_s, lincl_s, gq_s, gk_s, gv_s, gg_s, la_s,
                        *, tm, fox_w, fox_dh, gla_kw, gla_vw, n_heads_gla):
    t = pl.program_id(1)
    nt = pl.num_programs(1)
    dk = gla_kw // n_heads_gla
    dv = gla_vw // n_heads_gla

    @pl.when(jnp.logical_and(pl.program_id(0) == 0, t == 0))
    def _():
        r = lax.broadcasted_iota(jnp.int32, (tm, tm), 0)
        c = lax.broadcasted_iota(jnp.int32, (tm, tm), 1)
        uincl_s[...] = jnp.where(r <= c, 1.0, 0.0).astype(BF16)
        ltm_s[...] = jnp.where(c <= r, 1.0, 0.0).astype(BF16)
        r = lax.broadcasted_iota(jnp.int32, (GLA_CHUNK, GLA_CHUNK), 0)
        c = lax.broadcasted_iota(jnp.int32, (GLA_CHUNK, GLA_CHUNK), 1)
        lincl_s[...] = jnp.where(c <= r, 1.0, 0.0).astype(BF16)

    @pl.when(t == 0)
    def _():
        carry_s[...] = jnp.zeros_like(carry_s)
        bdt_s[...] = jnp.zeros_like(bdt_s)

    x = x_ref[0]
    xn = _rms(x, g1_ref[...]).astype(BF16)

    z = _dot_nt(xn, wtok_ref[...])
    kz = z[:, 0:fox_w]
    o0 = fox_w
    gq_s[...] = z[:, o0:o0 + gla_kw] * (dk ** -0.5)
    o0 += gla_kw
    gk_s[...] = z[:, o0:o0 + gla_kw]
    o0 += gla_kw
    gv_s[...] = z[:, o0:o0 + gla_vw]
    o0 += gla_vw
    gg_s[...] = z[:, o0:o0 + gla_vw]
    o0 += gla_vw
    misc = z[:, o0:o0 + LANES]
    pre = _dot(misc.astype(BF16), wgate_ref[...]) + bgate_ref[...]
    la_s[...] = _log_sigmoid(pre) * (1.0 / GLA_GATE_NORM)

    lf_tok = _log_sigmoid(misc + bfrow_ref[...])
    st3 = jnp.concatenate(_split3(lf_tok), axis=1).astype(BF16)
    cc = _dot(ltm_s[...], st3)
    cs_tok = cc[:, 0:LANES] + cc[:, LANES:2 * LANES] + cc[:, 2 * LANES:3 * LANES]
    d_tok = (cs_tok - cs_tok[0:1, :]) * LOG2E
    lane_k = lax.broadcasted_iota(jnp.int32, (tm, LANES), 1)
    for h in range(fox_w // fox_dh):
        own = (h % 2) * fox_dh
        spare = (1 - h % 2) * fox_dh
        parts = _split3(jnp.broadcast_to(d_tok[:, FF_LANE0 + h:FF_LANE0 + h + 1], (tm, LANES)))
        aug = jnp.zeros((tm, LANES), F32)
        for n, part in enumerate(parts):
            aug = jnp.where(lane_k == spare + n, -part, aug)
        own_l = jnp.logical_and(lane_k >= own, lane_k < own + fox_dh)
        kaug_out[0, h] = jnp.where(own_l, kz[:, (h // 2) * LANES:(h // 2 + 1) * LANES], aug).astype(BF16)

    zt = _dot_nt(wdm_ref[...], xn)
    qT_out[0] = (zt[0:fox_w] * (fox_dh ** -0.5 * LOG2E)).astype(BF16)
    kT_out[0] = zt[fox_w:2 * fox_w]
    vT_out[0] = zt[2 * fox_w:3 * fox_w]
    lf = _log_sigmoid(zt[3 * fox_w:3 * fox_w + SUBLANES] + bf_ref[...])
    lfT_out[0] = lf
    hi, mid, lo = _split3(lf)
    stack = jnp.concatenate([hi, mid, lo, jnp.zeros_like(hi)], axis=0).astype(BF16)
    cs = _dot(stack, uincl_s[...])
    cs = cs[0:8] + cs[8:16] + cs[16:24]
    carry = carry_s[...]
    cT_out[0] = cs + carry[:, 0:1]
    tot = _dot(stack, jnp.ones((tm, LANES), BF16))
    carry_s[...] = carry + tot[0:8] + tot[8:16] + tot[16:24]

    nsub = GLA_CHUNK // GLA_SUB
    rowi = lax.broadcasted_iota(jnp.int32, (GLA_CHUNK, gla_kw), 0)
    lanei = lax.broadcasted_iota(jnp.int32, (GLA_CHUNK, gla_kw), 1)
    ar = lax.broadcasted_iota(jnp.int32, (GLA_CHUNK, GLA_CHUNK), 0)
    ac = lax.broadcasted_iota(jnp.int32, (GLA_CHUNK, GLA_CHUNK), 1)
    tri_blk = jnp.logical_and(ar // GLA_SUB == ac // GLA_SUB, ar >= ac)
    br = lax.broadcasted_iota(jnp.int32, (gla_vw, gla_kw), 0)
    bc = lax.broadcasted_iota(jnp.int32, (gla_vw, gla_kw), 1)
    bd_mask = (br // dv) == (bc // dk)

    def chunk_body(ci, _):
        r0 = ci * GLA_CHUNK
        la_c = la_s[pl.ds(r0, GLA_CHUNK), :]
        gq_c = gq_s[pl.ds(r0, GLA_CHUNK), :]
        gk_c = gk_s[pl.ds(r0, GLA_CHUNK), :]
        gv_c = gv_s[pl.ds(r0, GLA_CHUNK), :]
        gg_c = gg_s[pl.ds(r0, GLA_CHUNK), :]
        h3, m3, l3 = _split3(la_c)
        st = jnp.concatenate([h3, m3, l3], axis=1).astype(BF16)
        bb = _dot(lincl_s[...], st)
        b = bb[:, 0:gla_kw] + bb[:, gla_kw:2 * gla_kw] + bb[:, 2 * gla_kw:3 * gla_kw]
        bmid_l, bend_l, b0_l = [], [], []
        for i in range(nsub):
            s0 = i * GLA_SUB
            bmid_l.append(b[s0 + GLA_SUB // 2:s0 + GLA_SUB // 2 + 1])
            bend_l.append(b[s0 + GLA_SUB - 1:s0 + GLA_SUB])
            b0_l.append(jnp.zeros((1, gla_kw), F32) if i == 0 else b[s0 - 1:s0])
        bc_rows = lambda rows: jnp.concatenate(
            [jnp.broadcast_to(r, (GLA_SUB, gla_kw)) for r in rows], axis=0)
        bmid, bend, b0 = bc_rows(bmid_l), bc_rows(bend_l), bc_rows(b0_l)
        qt = (gq_c * jnp.exp(b - bmid)).astype(BF16)
        kt = (gk_c * jnp.exp(bmid - b)).astype(BF16)
        qp = gq_c * jnp.exp(b - b0)
        kd = gk_c * jnp.exp(bend - b)
        gv_bf = gv_c.astype(BF16)
        gvT_bf = gv_c.T.astype(BF16)

        rms = [jnp.logical_and(rowi >= i * GLA_SUB, rowi < (i + 1) * GLA_SUB) for i in range(nsub)]
        uts = [_dot(gvT_bf, jnp.where(rms[i], kd, 0.0).astype(BF16)) for i in range(nsub)]
        states = [bdt_s[...]]
        for i in range(nsub):
            decay = jnp.exp(bend_l[i] - b0_l[i])
            states.append(states[i] * decay + jnp.where(bd_mask, uts[i], 0.0))
        bdt_s[...] = states[nsub]
        o_inter = None
        for i in range(nsub):
            d = _dot_nt(jnp.where(rms[i], qp, 0.0).astype(BF16), states[i].astype(BF16))
            o_inter = d if o_inter is None else o_inter + d

        for h in range(n_heads_gla):
            hm = jnp.logical_and(lanei >= h * dk, lanei < (h + 1) * dk)
            a = _dot_nt(jnp.where(hm, qt, jnp.zeros_like(qt)), kt)
            a = jnp.where(tri_blk, a, 0.0).astype(BF16)
            o_h = _dot(a, gv_bf[:, h * dv:(h + 1) * dv]) + o_inter[:, h * dv:(h + 1) * dv]
            o_n = _rms(o_h, gnorm_ref[...])
            go = o_n * _silu(gg_c[:, h * dv:(h + 1) * dv])
            go_out[0, pl.ds(r0, GLA_CHUNK), h * dv:(h + 1) * dv] = go.astype(BF16)
        return 0

    for ci in range(tm // GLA_CHUNK):
        chunk_body(ci, 0)

    @pl.when(t == nt - 1)
    def _():
        bd = bdt_s[...].T
        for h in range(n_heads_gla):
            sfin_out[0, h] = bd[h * dk:(h + 1) * dk, h * dv:(h + 1) * dv]


def _attn_prompt_kernel(qT_ref, kaug_ref, vT_ref, c_ref, oT_ref, vaug_s, qa_s, m_s, acc_s, *, t_blk, dh):
    i = pl.program_id(2)
    hw2 = 2 * dh

    @pl.when(i == 0)
    def _():
        for h in range(2):
            vaug_s[h, 0:dh, :] = vT_ref[0, h * dh:(h + 1) * dh, :].astype(BF16)
            vaug_s[h, dh:, :] = jnp.ones((vaug_s.shape[1] - dh, vaug_s.shape[2]), BF16)

    qT = qT_ref[0]
    rowq = lax.broadcasted_iota(jnp.int32, (hw2, t_blk), 0)
    keyi = lax.broadcasted_iota(jnp.int32, (ATT_KS, t_blk), 0)
    qryi = lax.broadcasted_iota(jnp.int32, (ATT_KS, t_blk), 1)
    t0 = pl.multiple_of(i * t_blk, t_blk)
    c_q = []
    for h in range(2):
        spare = (1 - h) * dh
        own_r = jnp.logical_and(rowq >= h * dh, rowq < (h + 1) * dh)
        ones_r = jnp.logical_and(rowq >= spare, rowq < spare + N_AUG)
        qa_s[h] = jnp.where(own_r, qT, jnp.where(ones_r, 1.0, 0.0).astype(BF16))
        m_s[h] = jnp.full(m_s.shape[1:], -jnp.inf, F32)
        acc_s[h] = jnp.zeros(acc_s.shape[1:], F32)
        c_q.append(c_ref[0, 0, h:h + 1, pl.ds(t0, LANES)][:, 0:1])

    n_sub = t_blk // ATT_KS

    def steps(tiles):
        work = []
        for j, masked in tiles:
            k0 = pl.multiple_of(j * t_blk, t_blk)
            kks = [pl.multiple_of(k0 + ks * ATT_KS, ATT_KS) for ks in range(n_sub)]
            s_all = [[_dot(kaug_ref[0, 0, h, pl.ds(kks[ks], ATT_KS), :], qa_s[h]) for ks in range(n_sub)]
                     for h in range(2)]
            work.append((k0, kks, s_all, masked))
        for h in range(2):
            m_run = m_s[h]
            acc = acc_s[h]
            for k0, kks, s_all, masked in work:
                off = (c_ref[0, 0, h:h + 1, pl.ds(k0, LANES)][:, 0:1] - c_q[h]) * LOG2E
                for ks in range(n_sub):
                    s = s_all[h][ks]
                    if masked:
                        s = jnp.where(keyi + ks * ATT_KS <= qryi, s, -jnp.inf)
                    m_new = jnp.maximum(m_run, jnp.max(s, axis=0, keepdims=True) - off)
                    p = jnp.exp2(s - (m_new + off))
                    alpha = jnp.exp2(m_run - m_new)
                    pv = _dot(vaug_s[h, :, pl.ds(kks[ks], ATT_KS)], p.astype(BF16))
                    acc = alpha * acc + pv
                    m_run = m_new
            acc_s[h] = acc
            m_s[h] = m_run

    def pair_body(jj, carry):
        steps([(2 * jj, False), (2 * jj + 1, False)])
        return carry

    lax.fori_loop(0, lax.shift_right_logical(i, 1), pair_body, 0)
    odd = jnp.bitwise_and(i, 1) == 1

    @pl.when(odd)
    def _():
        steps([(i - 1, False), (i, True)])

    @pl.when(jnp.logical_not(odd))
    def _():
        steps([(i, True)])

    outs = [acc_s[h][0:dh, :] / acc_s[h][dh:dh + 1, :] for h in range(2)]
    oT_ref[0] = jnp.concatenate(outs, axis=0).astype(BF16)


def _ffn_kernel(x_ref, foT_ref, go_ref, wo_ref, wup_ref, wdn_ref, g2_ref, gf_ref, y_ref, u_s, *, fox_w, d_ff):
    x = x_ref[0]
    h = x + (_dot(foT_ref[0].T, wo_ref[0:fox_w, :]) + _dot(go_ref[0], wo_ref[fox_w:, :]))
    hn = _rms(h, g2_ref[...]).astype(BF16)
    for c in range(d_ff // FFN_CHUNK):
        u = _dot(hn, wup_ref[:, c * FFN_CHUNK:(c + 1) * FFN_CHUNK])
        u_s[:, c * FFN_CHUNK:(c + 1) * FFN_CHUNK] = jnp.square(jnp.maximum(u, 0.0)).astype(BF16)
    y_ref[0] = _rms(h + _dot(u_s[...], wdn_ref[...]), gf_ref[...])


def _ffn_call(x3, foT, go, wo, wup, wdn, g2, gf, tm):
    nb, n, d = x3.shape
    fox_w = foT.shape[1]
    d_ff = wup.shape[1]
    const = lambda shape: pl.BlockSpec(shape, lambda b, i: (0, 0), pipeline_mode=pl.Buffered(1))
    return pl.pallas_call(
        functools.partial(_ffn_kernel, fox_w=fox_w, d_ff=d_ff),
        grid=(nb, n // tm),
        in_specs=[
            pl.BlockSpec((1, tm, d), lambda b, i: (b, i, 0)),
            pl.BlockSpec((1, fox_w, tm), lambda b, i: (b, 0, i)),
            pl.BlockSpec((1, tm, go.shape[2]), lambda b, i: (b, i, 0)),
            const(wo.shape), const(wup.shape), const(wdn.shape), const(g2.shape), const(gf.shape),
        ],
        out_specs=pl.BlockSpec((1, tm, d), lambda b, i: (b, i, 0)),
        out_shape=jax.ShapeDtypeStruct((nb, n, d), F32),
        scratch_shapes=[pltpu.VMEM((tm, d_ff), BF16)],
        compiler_params=pltpu.CompilerParams(dimension_semantics=("arbitrary", "arbitrary"),
                                             vmem_limit_bytes=VMEM_LIMIT_BYTES),
        name="merge_ffn",
    )(x3, foT, go, wo, wup, wdn, g2, gf)


def _proj_sample_kernel(x_ref, g1_ref, wtok_ref, wdm_ref, bf_ref, wgate_ref, bgate_ref,
                        q_out, k_out, v_out, lf_out, gq_out, gk_out, gv_out, gg_out, la_out,
                        *, fox_w, fox_dh, gla_kw, gla_vw, n_heads_gla, n_heads_fox):
    dk = gla_kw // n_heads_gla
    xn = _rms(x_ref[...], g1_ref[...]).astype(BF16)
    z = _dot_nt(xn, wtok_ref[...])
    o0 = fox_w
    gq_out[...] = z[:, o0:o0 + gla_kw] * (dk ** -0.5)
    o0 += gla_kw
    gk_out[...] = z[:, o0:o0 + gla_kw]
    o0 += gla_kw
    gv_out[...] = z[:, o0:o0 + gla_vw]
    o0 += gla_vw
    gg_out[...] = z[:, o0:o0 + gla_vw]
    o0 += gla_vw
    glr = z[:, o0:o0 + LANES].astype(BF16)
    la_out[...] = _log_sigmoid(_dot(glr, wgate_ref[...]) + bgate_ref[...]) * (1.0 / GLA_GATE_NORM)
    z2 = _dot_nt(xn, wdm_ref[...])
    q_out[...] = z2[:, 0:fox_w] * (fox_dh ** -0.5)
    k_out[...] = z2[:, fox_w:2 * fox_w]
    v_out[...] = z2[:, 2 * fox_w:3 * fox_w]
    lf_out[...] = _log_sigmoid(z2[:, 3 * fox_w:3 * fox_w + n_heads_fox] + bf_ref[...])


def _ffn_decode_kernel(pt_ref, x_ref, foT_ref, go_ref, wo_ref, wup_ref, wdn_ref, g2_ref, gf_ref,
                       q_ref, knew_ref, vnew_ref, lfnew_ref, kc_hbm, vc_hbm, lfc_hbm,
                       y_ref, o_ref,
                       u_s, ring, lfbuf, rev_s, zbuf, acc_s, qb_s, ustrict_s, live_s, sem_ring, sem_lf,
                       *, fox_w, d_ff, n_pages, n_b, n_heads, dh):
    b = pl.program_id(0) * pl.num_programs(1) + pl.program_id(1)
    page = LANES
    hw = n_heads * dh
    nch = n_pages // DEC_G
    per_b = 2 * nch
    total = n_b * per_b

    def start_chunk(g):
        bg = g // per_b
        c = g - bg * per_b
        slot = lax.rem(g, DEC_NSLOT)

        @pl.when(c < nch)
        def _():
            for j in range(DEC_G):
                p = n_pages - 1 - (c * DEC_G + j)
                pltpu.make_async_copy(kc_hbm.at[pt_ref[bg, p]], ring.at[slot, j], sem_ring.at[slot]).start()

        @pl.when(c >= nch)
        def _():
            for j in range(DEC_G):
                p = n_pages - 1 - ((c - nch) * DEC_G + j)
                pltpu.make_async_copy(vc_hbm.at[pt_ref[bg, p]], ring.at[slot, j], sem_ring.at[slot]).start()

    def wait_chunk(g):
        slot = lax.rem(g, DEC_NSLOT)
        for j in range(DEC_G):
            pltpu.make_async_copy(kc_hbm.at[0], ring.at[slot, j], sem_ring.at[slot]).wait()

    def start_lf(bb):
        sl = lax.rem(bb, 2)

        def body(p, _):
            pltpu.make_async_copy(lfc_hbm.at[pt_ref[bb, p]], lfbuf.at[sl, p], sem_lf.at[sl]).start()
            return 0

        lax.fori_loop(0, n_pages, body, 0)

    def wait_lf(bb):
        sl = lax.rem(bb, 2)

        def body(p, _):
            pltpu.make_async_copy(lfc_hbm.at[0], lfbuf.at[sl, p], sem_lf.at[sl]).wait()
            return 0

        lax.fori_loop(0, n_pages, body, 0)

    g0 = b * per_b

    @pl.when(b == 0)
    def _():
        r = lax.broadcasted_iota(jnp.int32, (page, page), 0)
        c = lax.broadcasted_iota(jnp.int32, (page, page), 1)
        ustrict_s[...] = jnp.where(r > c, 1.0, 0.0).astype(BF16)
        start_lf(b)
        for g in range(DEC_NSLOT - 1):
            start_chunk(g0 + g)

    wait_lf(b)

    @pl.when(b + 1 < n_b)
    def _():
        start_lf(b + 1)

    sl = lax.rem(b, 2)
    lf2d = lfbuf[sl].reshape(n_pages * n_heads, page)
    hi, mid, lo = _split3(lf2d)
    u = ustrict_s[...]
    rev = _dot(hi.astype(BF16), u) + _dot(mid.astype(BF16), u) + _dot(lo.astype(BF16), u)
    rev_s[...] = rev.reshape(n_pages, n_heads, page)

    qb = jnp.broadcast_to(q_ref[0], (hw, page))
    qb_s[...] = qb

    def head_sum(x):
        return jnp.sum(x.reshape(n_heads, dh, page), axis=1)

    def head_bcast(x):
        return jnp.broadcast_to(x[:, None, :], (n_heads, dh, page)).reshape(hw, page)

    x = x_ref[0]
    h = x + (_dot(foT_ref[0].T, wo_ref[0:fox_w, :]) + _dot(go_ref[0], wo_ref[fox_w:, :]))
    hn = _rms(h, g2_ref[...]).astype(BF16)
    n_ffn = d_ff // FFN_CHUNK
    grp = nch // n_ffn

    assert per_b % DEC_NSLOT == 0
    up_rows = hn.shape[0] // 2
    up_n = 2 * d_ff // nch
    carry = jnp.broadcast_to(lfnew_ref[0], (n_heads, page))
    for c in range(nch):
        g = g0 + c

        @pl.when(g + (DEC_NSLOT - 1) < total)
        def _():
            start_chunk(g + (DEC_NSLOT - 1))

        wait_chunk(g)
        slot = c % DEC_NSLOT
        r0 = (c % 2) * up_rows
        col0 = (c // 2) * up_n
        u = _dot(hn[r0:r0 + up_rows], wup_ref[:, col0:col0 + up_n])
        rows = []
        for hd in range(n_heads):
            hs = slice(hd * dh, (hd + 1) * dh)
            qh = qb_s[hs, :]
            ch = carry[hd:hd + 1]
            for j in range(DEC_G):
                p = n_pages - 1 - (c * DEC_G + j)
                s = jnp.sum(ring[slot, j, hs, :] * qh, axis=0, keepdims=True)
                revp = rev_s[p, hd:hd + 1, :]
                zbuf[hd:hd + 1, p * page:(p + 1) * page] = s + revp + ch
                ch = ch + jnp.broadcast_to(revp[:, 0:1] + lfbuf[sl, p, hd:hd + 1, 0:1], (1, page))
            rows.append(ch)
        carry = jnp.concatenate(rows, axis=0)
        u_s[r0:r0 + up_rows, col0:col0 + up_n] = jnp.square(jnp.maximum(u, 0.0)).astype(BF16)

    z_all = zbuf[...]
    z_new = head_sum(jnp.broadcast_to(knew_ref[0], (hw, page)) * qb)
    m = jnp.maximum(jnp.max(z_all, axis=1, keepdims=True), z_new[:, 0:1])
    p_all = jnp.exp(z_all - m)
    zbuf[...] = p_all
    p_new = jnp.exp(z_new - m)
    l = jnp.sum(p_all, axis=1, keepdims=True) + p_new

    acc_s[...] = jnp.zeros_like(acc_s)

    span = DEC_G * page
    for vc in range(nch):
        lo = (n_pages - (vc + 1) * DEC_G) * page
        live_s[vc] = jnp.where(jnp.max(p_all[:, lo:lo + span]) > 0.0, 1, 0).astype(jnp.int32)

    def v_body(c, carry):
        g = g0 + nch + c
        nxt = c + (DEC_NSLOT - 1)
        nxt_live = jnp.logical_or(nxt >= nch, live_s[jnp.minimum(nxt, nch - 1)] != 0)

        @pl.when(jnp.logical_and(g + (DEC_NSLOT - 1) < total, nxt_live))
        def _():
            start_chunk(g + (DEC_NSLOT - 1))

        @pl.when(jnp.logical_or(c < DEC_NSLOT - 1, live_s[c] != 0))
        def _():
            wait_chunk(g)
            slot = lax.rem(g, DEC_NSLOT)
            for hd in range(n_heads):
                hs = slice(hd * dh, (hd + 1) * dh)
                acc = acc_s[hs, :]
                for j in range(DEC_G):
                    p = n_pages - 1 - (c * DEC_G + j)
                    pp = zbuf[hd:hd + 1, pl.ds(pl.multiple_of(p * page, page), page)]
                    acc = acc + ring[slot, j, hs, :] * jnp.broadcast_to(pp, (dh, page))
                acc_s[hs, :] = acc

        return carry

    down = None
    for c in range(n_ffn):
        lax.fori_loop(c * grp, (c + 1) * grp, v_body, 0)
        d = _dot(u_s[:, c * FFN_CHUNK:(c + 1) * FFN_CHUNK], wdn_ref[c * FFN_CHUNK:(c + 1) * FFN_CHUNK, :])
        down = d if down is None else down + d
    y_ref[0] = _rms(h + down, gf_ref[...])

    num = jnp.sum(acc_s[...], axis=1, keepdims=True) + head_bcast(p_new) * jnp.broadcast_to(vnew_ref[0], (hw, page))
    o_ref[0] = (num / head_bcast(l))[:, 0:1]


def _gla_sample_kernel(s_ref, la_ref, k_ref, q_ref, v_ref, gg_ref, gnorm_ref, s_out, go_out, *, n_heads, dk, dv):
    for i in range(s_ref.shape[0]):
        for h in range(n_heads):
            la = la_ref[i, h * dk:(h + 1) * dk, :]
            kk = k_ref[i, h * dk:(h + 1) * dk, :]
            qq = q_ref[i, h * dk:(h + 1) * dk, :]
            vv = v_ref[i, :, h * dv:(h + 1) * dv]
            s_new = s_ref[i, h] * jnp.exp(la) + kk * vv
            s_out[i, h] = s_new
            o = jnp.sum(qq * s_new, axis=0, keepdims=True)
            o_n = _rms(o, gnorm_ref[...])
            go_out[i, :, h * dv:(h + 1) * dv] = o_n * _silu(gg_ref[i, :, h * dv:(h + 1) * dv])


def kernel(x_prompt, x_sample, cache_k, cache_v, cache_logf, state_gla, page_table, norm1_g, w_in, fox_b_f,
           gla_w_gate_up, gla_b_gate, gla_norm_g, w_o, norm2_g, w_up, w_down, final_g):
    B, S, D = x_prompt.shape
    Bd = x_sample.shape[0]
    depth, n_phys, page, H, dh = cache_k.shape
    _, _, Hg, dk, dv = state_gla.shape
    assert depth == 1 and x_sample.shape[1] == 1 and page == LANES
    fox_w = H * dh
    gla_kw = Hg * dk
    gla_vw = Hg * dv
    rank = gla_w_gate_up.shape[1]
    n_pages = page_table.shape[1]

    wt = jnp.transpose(w_in[0])
    o_fq, o_fk = 0, fox_w
    o_ff = 3 * fox_w
    o_gq = o_ff + H
    o_gk = o_gq + gla_kw
    o_gv = o_gk + gla_kw
    o_glr = o_gv + gla_vw
    o_gg = o_glr + rank
    misc_pad = LANES - FF_LANE0 - H
    w_tok = jnp.concatenate([
        wt[o_fk:o_fk + fox_w], wt[o_gq:o_gq + gla_kw], wt[o_gk:o_gk + gla_kw], wt[o_gv:o_gv + gla_vw],
        wt[o_gg:o_gg + gla_vw], wt[o_glr:o_glr + rank], jnp.zeros((FF_LANE0 - rank, D), F32),
        wt[o_ff:o_ff + H], jnp.zeros((misc_pad, D), F32)], axis=0).astype(BF16)
    w_dm = jnp.concatenate([
        wt[o_fq:o_fq + 3 * fox_w], wt[o_ff:o_ff + H], jnp.zeros((2 * SUBLANES - H, D), F32)], axis=0).astype(BF16)
    bf_row = jnp.concatenate([jnp.zeros((1, FF_LANE0), F32), fox_b_f.reshape(1, H), jnp.zeros((1, misc_pad), F32)],
                             axis=1)
    w_gate = jnp.concatenate([gla_w_gate_up[0], jnp.zeros((LANES - rank, gla_kw), F32)], axis=0).astype(BF16)
    wo_bf = w_o[0].astype(BF16)
    wup_bf = w_up[0].astype(BF16)
    wdn_bf = w_down[0].astype(BF16)
    g1 = norm1_g.reshape(1, D)
    g2 = norm2_g.reshape(1, D)
    gf = final_g.reshape(1, D)
    bgate = gla_b_gate.reshape(1, gla_kw)
    gnorm = gla_norm_g.reshape(1, dv)

    cparams = lambda sem: pltpu.CompilerParams(dimension_semantics=sem, vmem_limit_bytes=VMEM_LIMIT_BYTES)
    single = lambda shape: pl.BlockSpec(shape, lambda *_: (0,) * len(shape), pipeline_mode=pl.Buffered(1))

    tm = PROJ_TM
    nt = S // tm
    n_tok_cols = w_tok.shape[0]
    n_dm_rows = w_dm.shape[0]
    qT_p, kT_p, vT_p, lfT_p, cT_p, kaug_p, go_p, sfin_p = pl.pallas_call(
        functools.partial(_proj_prompt_kernel, tm=tm, fox_w=fox_w, fox_dh=dh, gla_kw=gla_kw, gla_vw=gla_vw,
                          n_heads_gla=Hg),
        grid=(B, nt),
        in_specs=[
            pl.BlockSpec((1, tm, D), lambda b, t: (b, t, 0)),
            single((1, D)), single((n_tok_cols, D)), single((n_dm_rows, D)), single((H, 1)), single((1, LANES)),
            single((LANES, gla_kw)), single((1, gla_kw)), single((1, dv)),
        ],
        out_specs=[
            pl.BlockSpec((1, fox_w, tm), lambda b, t: (b, 0, t)),
            pl.BlockSpec((1, fox_w, tm), lambda b, t: (b, 0, t)),
            pl.BlockSpec((1, fox_w, tm), lambda b, t: (b, 0, t)),
            pl.BlockSpec((1, H, tm), lambda b, t: (b, 0, t)),
            pl.BlockSpec((1, H, tm), lambda b, t: (b, 0, t)),
            pl.BlockSpec((1, H, tm, LANES), lambda b, t: (b, 0, t, 0)),
            pl.BlockSpec((1, tm, gla_vw), lambda b, t: (b, t, 0)),
            pl.BlockSpec((1, Hg, dk, dv), lambda b, t: (b, 0, 0, 0)),
        ],
        out_shape=[
            jax.ShapeDtypeStruct((B, fox_w, S), BF16),
            jax.ShapeDtypeStruct((B, fox_w, S), F32),
            jax.ShapeDtypeStruct((B, fox_w, S), F32),
            jax.ShapeDtypeStruct((B, H, S), F32),
            jax.ShapeDtypeStruct((B, H, S), F32),
            jax.ShapeDtypeStruct((B, H, S, LANES), BF16),
            jax.ShapeDtypeStruct((B, S, gla_vw), BF16),
            jax.ShapeDtypeStruct((B, Hg, dk, dv), F32),
        ],
        scratch_shapes=[
            pltpu.VMEM((H, LANES), F32),
            pltpu.VMEM((gla_vw, gla_kw), F32),
            pltpu.VMEM((tm, tm), BF16),
            pltpu.VMEM((tm, tm), BF16),
            pltpu.VMEM((GLA_CHUNK, GLA_CHUNK), BF16),
            pltpu.VMEM((tm, gla_kw), F32), pltpu.VMEM((tm, gla_kw), F32),
            pltpu.VMEM((tm, gla_vw), F32), pltpu.VMEM((tm, gla_vw), F32), pltpu.VMEM((tm, gla_kw), F32),
        ],
        compiler_params=cparams(("arbitrary", "arbitrary")),
        name="proj_gla_prompt",
    )(x_prompt, g1, w_tok, w_dm, fox_b_f.reshape(H, 1), bf_row, w_gate, bgate, gnorm)

    t_blk = ATT_T
    assert t_blk == tm
    nq = S // t_blk
    n_pairs = fox_w // LANES
    c4 = cT_p.reshape(B, n_pairs, 2, S)
    k5 = kaug_p.reshape(B, n_pairs, 2, S, LANES)
    n_vrows = dh + 2 * SUBLANES
    foT_p = pl.pallas_call(
        functools.partial(_attn_prompt_kernel, t_blk=t_blk, dh=dh),
        grid=(B, n_pairs, nq),
        in_specs=[
            pl.BlockSpec((1, LANES, t_blk), lambda b, p, i: (b, p, i)),
            pl.BlockSpec((1, 1, 2, S, LANES), lambda b, p, i: (b, p, 0, 0, 0)),
            pl.BlockSpec((1, LANES, S), lambda b, p, i: (b, p, 0)),
            pl.BlockSpec((1, 1, 2, S), lambda b, p, i: (b, p, 0, 0)),
        ],
        out_specs=pl.BlockSpec((1, LANES, t_blk), lambda b, p, i: (b, p, i)),
        out_shape=jax.ShapeDtypeStruct((B, fox_w, S), BF16),
        scratch_shapes=[
            pltpu.VMEM((2, n_vrows, S), BF16),
            pltpu.VMEM((2, LANES, t_blk), BF16),
            pltpu.VMEM((2, 1, t_blk), F32),
            pltpu.VMEM((2, n_vrows, t_blk), F32),
        ],
        compiler_params=cparams(("arbitrary", "arbitrary", "arbitrary")),
        name="fox_attn_prompt",
    )(qT_p, k5, vT_p, c4)

    xs = x_sample.reshape(Bd, D)
    full = lambda shape: pl.BlockSpec(shape, lambda: (0,) * len(shape))
    s_shapes = [(Bd, fox_w), (Bd, fox_w), (Bd, fox_w), (Bd, H), (Bd, gla_kw), (Bd, gla_kw), (Bd, gla_vw),
                (Bd, gla_vw), (Bd, gla_kw)]
    q_s, k_s, v_s, lf_s, gq_s, gk_s, gv_s, gg_s, la_s = pl.pallas_call(
        functools.partial(_proj_sample_kernel, fox_w=fox_w, fox_dh=dh, gla_kw=gla_kw, gla_vw=gla_vw, n_heads_gla=Hg,
                          n_heads_fox=H),
        in_specs=[full((Bd, D)), full((1, D)), full((n_tok_cols, D)), full((n_dm_rows, D)), full((1, H)),
                  full((LANES, gla_kw)), full((1, gla_kw))],
        out_specs=[full(s) for s in s_shapes],
        out_shape=[jax.ShapeDtypeStruct(s, F32) for s in s_shapes],
        compiler_params=pltpu.CompilerParams(vmem_limit_bytes=VMEM_LIMIT_BYTES),
        name="proj_sample",
    )(xs, g1, w_tok, w_dm, fox_b_f.reshape(1, H), w_gate, bgate)

    kc = jnp.transpose(cache_k[0], (0, 2, 3, 1)).reshape(n_phys, fox_w, page)
    vc = jnp.transpose(cache_v[0], (0, 2, 3, 1)).reshape(n_phys, fox_w, page)
    lfc = jnp.transpose(cache_logf[0], (0, 2, 1))
    col = lambda a: a.reshape(Bd, a.shape[1], 1)
    ftm = FFN_TM
    n_ft = S // ftm
    assert B * n_ft == Bd
    d_ff = wup_bf.shape[1]
    assert (n_pages // DEC_G) % (d_ff // FFN_CHUNK) == 0
    wconst = lambda shape: pl.BlockSpec(shape, lambda b, i, pt: (0, 0), pipeline_mode=pl.Buffered(1))
    seq = lambda b, i, pt: (b * n_ft + i, 0, 0)
    grid_spec = pltpu.PrefetchScalarGridSpec(
        num_scalar_prefetch=1,
        grid=(B, n_ft),
        in_specs=[
            pl.BlockSpec((1, ftm, D), lambda b, i, pt: (b, i, 0)),
            pl.BlockSpec((1, fox_w, ftm), lambda b, i, pt: (b, 0, i)),
            pl.BlockSpec((1, ftm, gla_vw), lambda b, i, pt: (b, i, 0)),
            wconst(wo_bf.shape), wconst(wup_bf.shape), wconst(wdn_bf.shape), wconst(g2.shape), wconst(gf.shape),
            pl.BlockSpec((1, fox_w, 1), seq), pl.BlockSpec((1, fox_w, 1), seq), pl.BlockSpec((1, fox_w, 1), seq),
            pl.BlockSpec((1, H, 1), seq),
            pl.BlockSpec(memory_space=pl.ANY), pl.BlockSpec(memory_space=pl.ANY), pl.BlockSpec(memory_space=pl.ANY),
        ],
        out_specs=[
            pl.BlockSpec((1, ftm, D), lambda b, i, pt: (b, i, 0)),
            pl.BlockSpec((1, fox_w, 1), seq),
        ],
        scratch_shapes=[
            pltpu.VMEM((ftm, d_ff), BF16),
            pltpu.VMEM((DEC_NSLOT, DEC_G, fox_w, page), F32),
            pltpu.VMEM((2, n_pages, H, page), F32),
            pltpu.VMEM((n_pages, H, page), F32),
            pltpu.VMEM((H, n_pages * page), F32),
            pltpu.VMEM((fox_w, page), F32),
            pltpu.VMEM((fox_w, page), F32),
            pltpu.VMEM((page, page), BF16),
            pltpu.SMEM((n_pages // DEC_G,), jnp.int32),
            pltpu.SemaphoreType.DMA((DEC_NSLOT,)),
            pltpu.SemaphoreType.DMA((2,)),
        ],
    )
    y_p, fo_s = pl.pallas_call(
        functools.partial(_ffn_decode_kernel, fox_w=fox_w, d_ff=d_ff, n_pages=n_pages, n_b=Bd, n_heads=H, dh=dh),
        grid_spec=grid_spec,
        out_shape=[jax.ShapeDtypeStruct((B, S, D), F32), jax.ShapeDtypeStruct((Bd, fox_w, 1), F32)],
        compiler_params=cparams(("arbitrary", "arbitrary")),
        name="ffn_prompt_fox_decode",
    )(page_table, x_prompt, foT_p, go_p, wo_bf, wup_bf, wdn_bf, g2, gf,
      col(q_s), col(k_s), col(v_s), col(lf_s), kc, vc, lfc)
    foT_s = fo_s.reshape(Bd, fox_w).T.astype(BF16)

    gb = GLA_SAMPLE_BLOCK
    s_new, go_s = pl.pallas_call(
        functools.partial(_gla_sample_kernel, n_heads=Hg, dk=dk, dv=dv),
        grid=(Bd // gb,),
        in_specs=[
            pl.BlockSpec((gb, Hg, dk, dv), lambda b: (b, 0, 0, 0)),
            pl.BlockSpec((gb, gla_kw, 1), lambda b: (b, 0, 0)),
            pl.BlockSpec((gb, gla_kw, 1), lambda b: (b, 0, 0)),
            pl.BlockSpec((gb, gla_kw, 1), lambda b: (b, 0, 0)),
            pl.BlockSpec((gb, 1, gla_vw), lambda b: (b, 0, 0)),
            pl.BlockSpec((gb, 1, gla_vw), lambda b: (b, 0, 0)),
            pl.BlockSpec((1, dv), lambda b: (0, 0)),
        ],
        out_specs=[
            pl.BlockSpec((gb, Hg, dk, dv), lambda b: (b, 0, 0, 0)),
            pl.BlockSpec((gb, 1, gla_vw), lambda b: (b, 0, 0)),
        ],
        out_shape=[jax.ShapeDtypeStruct((Bd, Hg, dk, dv), F32), jax.ShapeDtypeStruct((Bd, 1, gla_vw), F32)],
        compiler_params=cparams(("arbitrary",)),
        name="gla_sample",
    )(state_gla[0], col(la_s), col(gk_s), col(gq_s), gv_s.reshape(Bd, 1, gla_vw), gg_s.reshape(Bd, 1, gla_vw), gnorm)
    go_s = go_s.reshape(Bd, gla_vw).astype(BF16)

    y_s = _ffn_call(xs[None], foT_s[None], go_s[None], wo_bf, wup_bf, wdn_bf, g2, gf, Bd).reshape(Bd, 1, D)

    new_k_p = jnp.transpose(kT_p.reshape(1, B, H, dh, S), (0, 1, 4, 2, 3))
    new_v_p = jnp.transpose(vT_p.reshape(1, B, H, dh, S), (0, 1, 4, 2, 3))
    new_lf_p = jnp.transpose(lfT_p, (0, 2, 1)).reshape(1, B, S, H)
    return (y_p, y_s, new_k_p, new_v_p, new_lf_p, sfin_p.reshape(1, B, Hg, dk, dv),
            k_s.reshape(1, Bd, 1, H, dh), v_s.reshape(1, Bd, 1, H, dh), lf_s.reshape(1, Bd, 1, H),
            s_new.reshape(1, Bd, Hg, dk, dv))
```

```python
import functools

import jax
import jax.numpy as jnp
from jax import lax
from jax.experimental import pallas as pl
from jax.experimental.pallas import tpu as pltpu

F32 = jnp.float32
BF16 = jnp.bfloat16

LANES = 128
SUBLANES = 8
VMEM_LIMIT_BYTES = 56 * 1024 * 1024

EPS = 1e-6
LOG2E = 1.4426950408889634
N_AUG = 3
FF_LANE0 = 16
GLA_GATE_NORM = 16.0
GLA_CHUNK = 128
GLA_SUB = 32
PROJ_TM = 512
ATT_T = 512
ATT_KS = 256
FFN_TM = 512
FFN_CHUNK = 1024
DEC_G = 8
GLA_SAMPLE_BLOCK = 8
DEC_NSLOT = 4


def _dot(a, b):
    return jnp.dot(a, b, preferred_element_type=F32)


def _dot_nt(a, b):
    return lax.dot_general(a, b, (((1,), (1,)), ((), ())), preferred_element_type=F32)


def _split3(x):
    hi = x.astype(BF16).astype(F32)
    r = x - hi
    mid = r.astype(BF16).astype(F32)
    lo = r - mid
    return hi, mid, lo


def _log_sigmoid(x):
    return jnp.minimum(x, 0.0) - jnp.log1p(jnp.exp(-jnp.abs(x)))


def _silu(x):
    return x / (1.0 + jnp.exp(-x))


def _rms(x, g):
    return x * lax.rsqrt(jnp.mean(x * x, axis=-1, keepdims=True) + EPS) * g


def _proj_prompt_kernel(x_ref, g1_ref, wtok_ref, wdm_ref, bf_ref, bfrow_ref, wgate_ref, bgate_ref, gnorm_ref,
                        qT_out, kT_out, vT_out, lfT_out, cT_out, kaug_out, go_out, sfin_out,
                        carry_s, bdt_s, uincl_s, ltm_s, lincl_s, gq_s, gk_s, gv_s, gg_s, la_s,
                        *, tm, fox_w, fox_dh, gla_kw, gla_vw, n_heads_gla):
    t = pl.program_id(1)
    nt = pl.num_programs(1)
    dk = gla_kw // n_heads_gla
    dv = gla_vw // n_heads_gla

    @pl.when(jnp.logical_and(pl.program_id(0) == 0, t == 0))
    def _():
        r = lax.broadcasted_iota(jnp.int32, (tm, tm), 0)
        c = lax.broadcasted_iota(jnp.int32, (tm, tm), 1)
        uincl_s[...] = jnp.where(r <= c, 1.0, 0.0).astype(BF16)
        ltm_s[...] = jnp.where(c <= r, 1.0, 0.0).astype(BF16)
        r = lax.broadcasted_iota(jnp.int32, (GLA_CHUNK, GLA_CHUNK), 0)
        c = lax.broadcasted_iota(jnp.int32, (GLA_CHUNK, GLA_CHUNK), 1)
        lincl_s[...] = jnp.where(c <= r, 1.0, 0.0).astype(BF16)

    @pl.when(t == 0)
    def _():
        carry_s[...] = jnp.zeros_like(carry_s)
        bdt_s[...] = jnp.zeros_like(bdt_s)

    x = x_ref[0]
    xn = _rms(x, g1_ref[...]).astype(BF16)

    zt = _dot_nt(wdm_ref[...], xn)
    kz = zt[fox_w:2 * fox_w].T

    z = _dot_nt(xn, wtok_ref[...])
    o0 = 0
    gq_s[...] = z[:, o0:o0 + gla_kw] * (dk ** -0.5)
    o0 += gla_kw
    gk_s[...] = z[:, o0:o0 + gla_kw]
    o0 += gla_kw
    gv_s[...] = z[:, o0:o0 + gla_vw]
    o0 += gla_vw
    gg_s[...] = z[:, o0:o0 + gla_vw]
    o0 += gla_vw
    misc = z[:, o0:o0 + LANES]
    pre = _dot(misc.astype(BF16), wgate_ref[...]) + bgate_ref[...]
    la_s[...] = _log_sigmoid(pre) * (1.0 / GLA_GATE_NORM)

    lf_tok = _log_sigmoid(misc + bfrow_ref[...])
    st3 = jnp.concatenate(_split3(lf_tok), axis=1).astype(BF16)
    cc = _dot(ltm_s[...], st3)
    cs_tok = cc[:, 0:LANES] + cc[:, LANES:2 * LANES] + cc[:, 2 * LANES:3 * LANES]
    d_tok = (cs_tok - cs_tok[0:1, :]) * LOG2E
    lane_k = lax.broadcasted_iota(jnp.int32, (tm, LANES), 1)
    for h in range(fox_w // fox_dh):
        own = (h % 2) * fox_dh
        spare = (1 - h % 2) * fox_dh
        parts = _split3(jnp.broadcast_to(d_tok[:, FF_LANE0 + h:FF_LANE0 + h + 1], (tm, LANES)))
        aug = jnp.zeros((tm, LANES), F32)
        for n, part in enumerate(parts):
            aug = jnp.where(lane_k == spare + n, -part, aug)
        own_l = jnp.logical_and(lane_k >= own, lane_k < own + fox_dh)
        kaug_out[0, h] = jnp.where(own_l, kz[:, (h // 2) * LANES:(h // 2 + 1) * LANES], aug).astype(BF16)

    qT_out[0] = (zt[0:fox_w] * (fox_dh ** -0.5 * LOG2E)).astype(BF16)
    kT_out[0] = zt[fox_w:2 * fox_w]
    vT_out[0] = zt[2 * fox_w:3 * fox_w]
    lf = _log_sigmoid(zt[3 * fox_w:3 * fox_w + SUBLANES] + bf_ref[...])
    lfT_out[0] = lf
    hi, mid, lo = _split3(lf)
    stack = jnp.concatenate([hi, mid, lo, jnp.zeros_like(hi)], axis=0).astype(BF16)
    cs = _dot(stack, uincl_s[...])
    cs = cs[0:8] + cs[8:16] + cs[16:24]
    carry = carry_s[...]
    cT_out[0] = cs + carry[:, 0:1]
    tot = _dot(stack, jnp.ones((tm, LANES), BF16))
    carry_s[...] = carry + tot[0:8] + tot[8:16] + tot[16:24]

    nsub = GLA_CHUNK // GLA_SUB
    rowi = lax.broadcasted_iota(jnp.int32, (GLA_CHUNK, gla_kw), 0)
    lanei = lax.broadcasted_iota(jnp.int32, (GLA_CHUNK, gla_kw), 1)
    ar = lax.broadcasted_iota(jnp.int32, (GLA_CHUNK, GLA_CHUNK), 0)
    ac = lax.broadcasted_iota(jnp.int32, (GLA_CHUNK, GLA_CHUNK), 1)
    tri_blk = jnp.logical_and(ar // GLA_SUB == ac // GLA_SUB, ar >= ac)
    br = lax.broadcasted_iota(jnp.int32, (gla_vw, gla_kw), 0)
    bc = lax.broadcasted_iota(jnp.int32, (gla_vw, gla_kw), 1)
    bd_mask = (br // dv) == (bc // dk)

    def chunk_body(ci, _):
        r0 = ci * GLA_CHUNK
        la_c = la_s[pl.ds(r0, GLA_CHUNK), :]
        gq_c = gq_s[pl.ds(r0, GLA_CHUNK), :]
        gk_c = gk_s[pl.ds(r0, GLA_CHUNK), :]
        gv_c = gv_s[pl.ds(r0, GLA_CHUNK), :]
        gg_c = gg_s[pl.ds(r0, GLA_CHUNK), :]
        h3, m3, l3 = _split3(la_c)
        st = jnp.concatenate([h3, m3, l3], axis=1).astype(BF16)
        bb = _dot(lincl_s[...], st)
        b = bb[:, 0:gla_kw] + bb[:, gla_kw:2 * gla_kw] + bb[:, 2 * gla_kw:3 * gla_kw]
        bmid_l, bend_l, b0_l = [], [], []
        for i in range(nsub):
            s0 = i * GLA_SUB
            bmid_l.append(b[s0 + GLA_SUB // 2:s0 + GLA_SUB // 2 + 1])
            bend_l.append(b[s0 + GLA_SUB - 1:s0 + GLA_SUB])
            b0_l.append(jnp.zeros((1, gla_kw), F32) if i == 0 else b[s0 - 1:s0])
        bc_rows = lambda rows: jnp.concatenate(
            [jnp.broadcast_to(r, (GLA_SUB, gla_kw)) for r in rows], axis=0)
        bmid, bend, b0 = bc_rows(bmid_l), bc_rows(bend_l), bc_rows(b0_l)
        qt = (gq_c * jnp.exp(b - bmid)).astype(BF16)
        kt = (gk_c * jnp.exp(bmid - b)).astype(BF16)
        qp = gq_c * jnp.exp(b - b0)
        kd = gk_c * jnp.exp(bend - b)
        gv_bf = gv_c.astype(BF16)
        gvT_bf = gv_c.T.astype(BF16)

        rms = [jnp.logical_and(rowi >= i * GLA_SUB, rowi < (i + 1) * GLA_SUB) for i in range(nsub)]
        uts = [_dot(gvT_bf, jnp.where(rms[i], kd, 0.0).astype(BF16)) for i in range(nsub)]
        states = [bdt_s[...]]
        for i in range(nsub):
            decay = jnp.exp(bend_l[i] - b0_l[i])
            states.append(states[i] * decay + jnp.where(bd_mask, uts[i], 0.0))
        bdt_s[...] = states[nsub]
        o_inter = None
        for i in range(nsub):
            d = _dot_nt(jnp.where(rms[i], qp, 0.0).astype(BF16), states[i].astype(BF16))
            o_inter = d if o_inter is None else o_inter + d

        for h in range(n_heads_gla):
            hm = jnp.logical_and(lanei >= h * dk, lanei < (h + 1) * dk)
            a = _dot_nt(jnp.where(hm, qt, jnp.zeros_like(qt)), kt)
            a = jnp.where(tri_blk, a, 0.0).astype(BF16)
            o_h = _dot(a, gv_bf[:, h * dv:(h + 1) * dv]) + o_inter[:, h * dv:(h + 1) * dv]
            o_n = _rms(o_h, gnorm_ref[...])
            go = o_n * _silu(gg_c[:, h * dv:(h + 1) * dv])
            go_out[0, pl.ds(r0, GLA_CHUNK), h * dv:(h + 1) * dv] = go.astype(BF16)
        return 0

    for ci in range(tm // GLA_CHUNK):
        chunk_body(ci, 0)

    @pl.when(t == nt - 1)
    def _():
        bd = bdt_s[...].T
        for h in range(n_heads_gla):
            sfin_out[0, h] = bd[h * dk:(h + 1) * dk, h * dv:(h + 1) * dv]


def _attn_prompt_kernel(qT_ref, kaug_ref, vT_ref, c_ref, oT_ref, vaug_s, qa_s, m_s, acc_s, *, t_blk, dh):
    i = pl.program_id(2)
    hw2 = 2 * dh

    @pl.when(i == 0)
    def _():
        for h in range(2):
            vaug_s[h, 0:dh, :] = vT_ref[0, h * dh:(h + 1) * dh, :].astype(BF16)
            vaug_s[h, dh:, :] = jnp.ones((vaug_s.shape[1] - dh, vaug_s.shape[2]), BF16)

    qT = qT_ref[0]
    rowq = lax.broadcasted_iota(jnp.int32, (hw2, t_blk), 0)
    keyi = lax.broadcasted_iota(jnp.int32, (ATT_KS, t_blk), 0)
    qryi = lax.broadcasted_iota(jnp.int32, (ATT_KS, t_blk), 1)
    t0 = pl.multiple_of(i * t_blk, t_blk)
    c_q = []
    for h in range(2):
        spare = (1 - h) * dh
        own_r = jnp.logical_and(rowq >= h * dh, rowq < (h + 1) * dh)
        ones_r = jnp.logical_and(rowq >= spare, rowq < spare + N_AUG)
        qa_s[h] = jnp.where(own_r, qT, jnp.where(ones_r, 1.0, 0.0).astype(BF16))
        m_s[h] = jnp.full(m_s.shape[1:], -jnp.inf, F32)
        acc_s[h] = jnp.zeros(acc_s.shape[1:], F32)
        c_q.append(c_ref[0, 0, h:h + 1, pl.ds(t0, LANES)][:, 0:1])

    n_sub = t_blk // ATT_KS

    def steps(tiles):
        work = []
        for j, masked in tiles:
            k0 = pl.multiple_of(j * t_blk, t_blk)
            kks = [pl.multiple_of(k0 + ks * ATT_KS, ATT_KS) for ks in range(n_sub)]
            s_all = [[_dot(kaug_ref[0, 0, h, pl.ds(kks[ks], ATT_KS), :], qa_s[h]) for ks in range(n_sub)]
                     for h in range(2)]
            work.append((k0, kks, s_all, masked))
        m_run = [m_s[h] for h in range(2)]
        acc = [acc_s[h] for h in range(2)]
        for k0, kks, s_all, masked in work:
            offs = [(c_ref[0, 0, h:h + 1, pl.ds(k0, LANES)][:, 0:1] - c_q[h]) * LOG2E for h in range(2)]
            for ks in range(n_sub):
                for h in range(2):
                    s = s_all[h][ks]
                    if masked:
                        s = jnp.where(keyi + ks * ATT_KS <= qryi, s, -jnp.inf)
                    m_new = jnp.maximum(m_run[h], jnp.max(s, axis=0, keepdims=True) - offs[h])
                    p = jnp.exp2(s - (m_new + offs[h]))
                    alpha = jnp.exp2(m_run[h] - m_new)
                    pv = _dot(vaug_s[h, :, pl.ds(kks[ks], ATT_KS)], p.astype(BF16))
                    acc[h] = alpha * acc[h] + pv
                    m_run[h] = m_new
        for h in range(2):
            acc_s[h] = acc[h]
            m_s[h] = m_run[h]

    def pair_body(jj, carry):
        steps([(2 * jj, False), (2 * jj + 1, False)])
        return carry

    lax.fori_loop(0, lax.shift_right_logical(i, 1), pair_body, 0)
    odd = jnp.bitwise_and(i, 1) == 1

    @pl.when(odd)
    def _():
        steps([(i - 1, False), (i, True)])

    @pl.when(jnp.logical_not(odd))
    def _():
        steps([(i, True)])

    outs = [acc_s[h][0:dh, :] / acc_s[h][dh:dh + 1, :] for h in range(2)]
    oT_ref[0] = jnp.concatenate(outs, axis=0).astype(BF16)


def _ffn_kernel(x_ref, foT_ref, go_ref, wo_ref, wup_ref, wdn_ref, g2_ref, gf_ref, y_ref, u_s, *, fox_w, d_ff):
    x = x_ref[0]
    h = x + (_dot(foT_ref[0].T, wo_ref[0:fox_w, :]) + _dot(go_ref[0], wo_ref[fox_w:, :]))
    hn = _rms(h, g2_ref[...]).astype(BF16)
    for c in range(d_ff // FFN_CHUNK):
        u = _dot(hn, wup_ref[:, c * FFN_CHUNK:(c + 1) * FFN_CHUNK])
        u_s[:, c * FFN_CHUNK:(c + 1) * FFN_CHUNK] = jnp.square(jnp.maximum(u, 0.0)).astype(BF16)
    y_ref[0] = _rms(h + _dot(u_s[...], wdn_ref[...]), gf_ref[...])


def _ffn_call(x3, foT, go, wo, wup, wdn, g2, gf, tm):
    nb, n, d = x3.shape
    fox_w = foT.shape[1]
    d_ff = wup.shape[1]
    const = lambda shape: pl.BlockSpec(shape, lambda b, i: (0, 0), pipeline_mode=pl.Buffered(1))
    return pl.pallas_call(
        functools.partial(_ffn_kernel, fox_w=fox_w, d_ff=d_ff),
        grid=(nb, n // tm),
        in_specs=[
            pl.BlockSpec((1, tm, d), lambda b, i: (b, i, 0)),
            pl.BlockSpec((1, fox_w, tm), lambda b, i: (b, 0, i)),
            pl.BlockSpec((1, tm, go.shape[2]), lambda b, i: (b, i, 0)),
            const(wo.shape), const(wup.shape), const(wdn.shape), const(g2.shape), const(gf.shape),
        ],
        out_specs=pl.BlockSpec((1, tm, d), lambda b, i: (b, i, 0)),
        out_shape=jax.ShapeDtypeStruct((nb, n, d), F32),
        scratch_shapes=[pltpu.VMEM((tm, d_ff), BF16)],
        compiler_params=pltpu.CompilerParams(dimension_semantics=("arbitrary", "arbitrary"),
                                             vmem_limit_bytes=VMEM_LIMIT_BYTES),
        name="merge_ffn",
    )(x3, foT, go, wo, wup, wdn, g2, gf)


def _proj_sample_kernel(x_ref, g1_ref, wtok_ref, wdm_ref, bf_ref, wgate_ref, bgate_ref,
                        q_out, k_out, v_out, lf_out, gq_out, gk_out, gv_out, gg_out, la_out,
                        *, fox_w, fox_dh, gla_kw, gla_vw, n_heads_gla, n_heads_fox):
    dk = gla_kw // n_heads_gla
    xn = _rms(x_ref[...], g1_ref[...]).astype(BF16)
    z = _dot_nt(xn, wtok_ref[...])
    o0 = 0
    gq_out[...] = z[:, o0:o0 + gla_kw] * (dk ** -0.5)
    o0 += gla_kw
    gk_out[...] = z[:, o0:o0 + gla_kw]
    o0 += gla_kw
    gv_out[...] = z[:, o0:o0 + gla_vw]
    o0 += gla_vw
    gg_out[...] = z[:, o0:o0 + gla_vw]
    o0 += gla_vw
    glr = z[:, o0:o0 + LANES].astype(BF16)
    la_out[...] = _log_sigmoid(_dot(glr, wgate_ref[...]) + bgate_ref[...]) * (1.0 / GLA_GATE_NORM)
    z2 = _dot_nt(xn, wdm_ref[...])
    q_out[...] = z2[:, 0:fox_w] * (fox_dh ** -0.5)
    k_out[...] = z2[:, fox_w:2 * fox_w]
    v_out[...] = z2[:, 2 * fox_w:3 * fox_w]
    lf_out[...] = _log_sigmoid(z2[:, 3 * fox_w:3 * fox_w + n_heads_fox] + bf_ref[...])


def _ffn_decode_kernel(pt_ref, x_ref, foT_ref, go_ref, wo_ref, wup_ref, wdn_ref, g2_ref, gf_ref,
                       q_ref, knew_ref, vnew_ref, lfnew_ref, kc_hbm, vc_hbm, lfc_hbm,
                       y_ref, o_ref,
                       u_s, ring, lfbuf, rev_s, zbuf, acc_s, qb_s, ustrict_s, live_s, sem_ring, sem_lf,
                       *, fox_w, d_ff, n_pages, n_b, n_heads, dh):
    b = pl.program_id(0) * pl.num_programs(1) + pl.program_id(1)
    page = LANES
    hw = n_heads * dh
    nch = n_pages // DEC_G
    per_b = 2 * nch
    total = n_b * per_b

    def start_chunk(g):
        bg = g // per_b
        c = g - bg * per_b
        slot = lax.rem(g, DEC_NSLOT)

        @pl.when(c < nch)
        def _():
            for j in range(DEC_G):
                p = n_pages - 1 - (c * DEC_G + j)
                pltpu.make_async_copy(kc_hbm.at[pt_ref[bg, p]], ring.at[slot, j], sem_ring.at[slot]).start()

        @pl.when(c >= nch)
        def _():
            for j in range(DEC_G):
                p = n_pages - 1 - ((c - nch) * DEC_G + j)
                pltpu.make_async_copy(vc_hbm.at[pt_ref[bg, p]], ring.at[slot, j], sem_ring.at[slot]).start()

    def wait_chunk(g):
        slot = lax.rem(g, DEC_NSLOT)
        for j in range(DEC_G):
            pltpu.make_async_copy(kc_hbm.at[0], ring.at[slot, j], sem_ring.at[slot]).wait()

    def start_lf(bb):
        sl = lax.rem(bb, 2)

        def body(p, _):
            pltpu.make_async_copy(lfc_hbm.at[pt_ref[bb, p]], lfbuf.at[sl, p], sem_lf.at[sl]).start()
            return 0

        lax.fori_loop(0, n_pages, body, 0)

    def wait_lf(bb):
        sl = lax.rem(bb, 2)

        def body(p, _):
            pltpu.make_async_copy(lfc_hbm.at[0], lfbuf.at[sl, p], sem_lf.at[sl]).wait()
            return 0

        lax.fori_loop(0, n_pages, body, 0)

    g0 = b * per_b

    @pl.when(b == 0)
    def _():
        r = lax.broadcasted_iota(jnp.int32, (page, page), 0)
        c = lax.broadcasted_iota(jnp.int32, (page, page), 1)
        ustrict_s[...] = jnp.where(r > c, 1.0, 0.0).astype(BF16)
        start_lf(b)
        for g in range(DEC_NSLOT - 1):
            start_chunk(g0 + g)

    wait_lf(b)

    @pl.when(b + 1 < n_b)
    def _():
        start_lf(b + 1)

    sl = lax.rem(b, 2)
    lf2d = lfbuf[sl].reshape(n_pages * n_heads, page)
    hi, mid, lo = _split3(lf2d)
    u = ustrict_s[...]
    rev = _dot(hi.astype(BF16), u) + _dot(mid.astype(BF16), u) + _dot(lo.astype(BF16), u)
    rev_s[...] = rev.reshape(n_pages, n_heads, page)

    qb = jnp.broadcast_to(q_ref[0], (hw, page))
    qb_s[...] = qb

    def head_sum(x):
        return jnp.sum(x.reshape(n_heads, dh, page), axis=1)

    def head_bcast(x):
        return jnp.broadcast_to(x[:, None, :], (n_heads, dh, page)).reshape(hw, page)

    x = x_ref[0]
    h = x + (_dot(foT_ref[0].T, wo_ref[0:fox_w, :]) + _dot(go_ref[0], wo_ref[fox_w:, :]))
    hn = _rms(h, g2_ref[...]).astype(BF16)
    n_ffn = d_ff // FFN_CHUNK
    grp = nch // n_ffn

    assert per_b % DEC_NSLOT == 0
    up_rows = hn.shape[0] // 2
    up_n = 2 * d_ff // nch
    carry = jnp.broadcast_to(lfnew_ref[0], (n_heads, page))
    for c in range(nch):
        g = g0 + c

        @pl.when(g + (DEC_NSLOT - 1) < total)
        def _():
            start_chunk(g + (DEC_NSLOT - 1))

        wait_chunk(g)
        slot = c % DEC_NSLOT
        r0 = (c % 2) * up_rows
        col0 = (c // 2) * up_n
        u = _dot(hn[r0:r0 + up_rows], wup_ref[:, col0:col0 + up_n])
        rows = []
        for hd in range(n_heads):
            hs = slice(hd * dh, (hd + 1) * dh)
            qh = qb_s[hs, :]
            ch = carry[hd:hd + 1]
            for j in range(DEC_G):
                p = n_pages - 1 - (c * DEC_G + j)
                s = jnp.sum(ring[slot, j, hs, :] * qh, axis=0, keepdims=True)
                revp = rev_s[p, hd:hd + 1, :]
                zbuf[hd:hd + 1, p * page:(p + 1) * page] = s + revp + ch
                ch = ch + jnp.broadcast_to(revp[:, 0:1] + lfbuf[sl, p, hd:hd + 1, 0:1], (1, page))
            rows.append(ch)
        carry = jnp.concatenate(rows, axis=0)
        u_s[r0:r0 + up_rows, col0:col0 + up_n] = jnp.square(jnp.maximum(u, 0.0)).astype(BF16)

    z_all = zbuf[...]
    z_new = head_sum(jnp.broadcast_to(knew_ref[0], (hw, page)) * qb)
    m = jnp.maximum(jnp.max(z_all, axis=1, keepdims=True), z_new[:, 0:1])
    p_all = jnp.exp(z_all - m)
    zbuf[...] = p_all
    p_new = jnp.exp(z_new - m)
    l = jnp.sum(p_all, axis=1, keepdims=True) + p_new

    acc_s[...] = jnp.zeros_like(acc_s)

    span = DEC_G * page
    for vc in range(nch):
        lo = (n_pages - (vc + 1) * DEC_G) * page
        live_s[vc] = jnp.where(jnp.max(p_all[:, lo:lo + span]) > 0.0, 1, 0).astype(jnp.int32)

    def v_body(c, carry):
        g = g0 + nch + c
        nxt = c + (DEC_NSLOT - 1)
        nxt_live = jnp.logical_or(nxt >= nch, live_s[jnp.minimum(nxt, nch - 1)] != 0)

        @pl.when(jnp.logical_and(g + (DEC_NSLOT - 1) < total, nxt_live))
        def _():
            start_chunk(g + (DEC_NSLOT - 1))

        @pl.when(jnp.logical_or(c < DEC_NSLOT - 1, live_s[c] != 0))
        def _():
            wait_chunk(g)
            slot = lax.rem(g, DEC_NSLOT)
            for hd in range(n_heads):
                hs = slice(hd * dh, (hd + 1) * dh)
                acc = acc_s[hs, :]
                for j in range(DEC_G):
                    p = n_pages - 1 - (c * DEC_G + j)
                    pp = zbuf[hd:hd + 1, pl.ds(pl.multiple_of(p * page, page), page)]
                    acc = acc + ring[slot, j, hs, :] * jnp.broadcast_to(pp, (dh, page))
                acc_s[hs, :] = acc

        return carry

    down = None
    for c in range(n_ffn):
        lax.fori_loop(c * grp, (c + 1) * grp, v_body, 0)
        d = _dot(u_s[:, c * FFN_CHUNK:(c + 1) * FFN_CHUNK], wdn_ref[c * FFN_CHUNK:(c + 1) * FFN_CHUNK, :])
        down = d if down is None else down + d
    y_ref[0] = _rms(h + down, gf_ref[...])

    num = jnp.sum(acc_s[...], axis=1, keepdims=True) + head_bcast(p_new) * jnp.broadcast_to(vnew_ref[0], (hw, page))
    o_ref[0] = (num / head_bcast(l))[:, 0:1]


def _gla_sample_kernel(s_ref, la_ref, k_ref, q_ref, v_ref, gg_ref, gnorm_ref, s_out, go_out, *, n_heads, dk, dv):
    for i in range(s_ref.shape[0]):
        for h in range(n_heads):
            la = la_ref[i, h * dk:(h + 1) * dk, :]
            kk = k_ref[i, h * dk:(h + 1) * dk, :]
            qq = q_ref[i, h * dk:(h + 1) * dk, :]
            vv = v_ref[i, :, h * dv:(h + 1) * dv]
            s_new = s_ref[i, h] * jnp.exp(la) + kk * vv
            s_out[i, h] = s_new
            o = jnp.sum(qq * s_new, axis=0, keepdims=True)
            o_n = _rms(o, gnorm_ref[...])
            go_out[i, :, h * dv:(h + 1) * dv] = o_n * _silu(gg_ref[i, :, h * dv:(h + 1) * dv])


def kernel(x_prompt, x_sample, cache_k, cache_v, cache_logf, state_gla, page_table, norm1_g, w_in, fox_b_f,
           gla_w_gate_up, gla_b_gate, gla_norm_g, w_o, norm2_g, w_up, w_down, final_g):
    B, S, D = x_prompt.shape
    Bd = x_sample.shape[0]
    depth, n_phys, page, H, dh = cache_k.shape
    _, _, Hg, dk, dv = state_gla.shape
    assert depth == 1 and x_sample.shape[1] == 1 and page == LANES
    fox_w = H * dh
    gla_kw = Hg * dk
    gla_vw = Hg * dv
    rank = gla_w_gate_up.shape[1]
    n_pages = page_table.shape[1]

    wt = jnp.transpose(w_in[0])
    o_fq = 0
    o_ff = 3 * fox_w
    o_gq = o_ff + H
    o_gk = o_gq + gla_kw
    o_gv = o_gk + gla_kw
    o_glr = o_gv + gla_vw
    o_gg = o_glr + rank
    misc_pad = LANES - FF_LANE0 - H
    w_tok = jnp.concatenate([
        wt[o_gq:o_gq + gla_kw], wt[o_gk:o_gk + gla_kw], wt[o_gv:o_gv + gla_vw],
        wt[o_gg:o_gg + gla_vw], wt[o_glr:o_glr + rank], jnp.zeros((FF_LANE0 - rank, D), F32),
        wt[o_ff:o_ff + H], jnp.zeros((misc_pad, D), F32)], axis=0).astype(BF16)
    w_dm = jnp.concatenate([
        wt[o_fq:o_fq + 3 * fox_w], wt[o_ff:o_ff + H], jnp.zeros((2 * SUBLANES - H, D), F32)], axis=0).astype(BF16)
    bf_row = jnp.concatenate([jnp.zeros((1, FF_LANE0), F32), fox_b_f.reshape(1, H), jnp.zeros((1, misc_pad), F32)],
                             axis=1)
    w_gate = jnp.concatenate([gla_w_gate_up[0], jnp.zeros((LANES - rank, gla_kw), F32)], axis=0).astype(BF16)
    wo_bf = w_o[0].astype(BF16)
    wup_bf = w_up[0].astype(BF16)
    wdn_bf = w_down[0].astype(BF16)
    g1 = norm1_g.reshape(1, D)
    g2 = norm2_g.reshape(1, D)
    gf = final_g.reshape(1, D)
    bgate = gla_b_gate.reshape(1, gla_kw)
    gnorm = gla_norm_g.reshape(1, dv)

    cparams = lambda sem: pltpu.CompilerParams(dimension_semantics=sem, vmem_limit_bytes=VMEM_LIMIT_BYTES)
    single = lambda shape: pl.BlockSpec(shape, lambda *_: (0,) * len(shape), pipeline_mode=pl.Buffered(1))

    tm = PROJ_TM
    nt = S // tm
    n_tok_cols = w_tok.shape[0]
    n_dm_rows = w_dm.shape[0]
    qT_p, kT_p, vT_p, lfT_p, cT_p, kaug_p, go_p, sfin_p = pl.pallas_call(
        functools.partial(_proj_prompt_kernel, tm=tm, fox_w=fox_w, fox_dh=dh, gla_kw=gla_kw, gla_vw=gla_vw,
                          n_heads_gla=Hg),
        grid=(B, nt),
        in_specs=[
            pl.BlockSpec((1, tm, D), lambda b, t: (b, t, 0)),
            single((1, D)), single((n_tok_cols, D)), single((n_dm_rows, D)), single((H, 1)), single((1, LANES)),
            single((LANES, gla_kw)), single((1, gla_kw)), single((1, dv)),
        ],
        out_specs=[
            pl.BlockSpec((1, fox_w, tm), lambda b, t: (b, 0, t)),
            pl.BlockSpec((1, fox_w, tm), lambda b, t: (b, 0, t)),
            pl.BlockSpec((1, fox_w, tm), lambda b, t: (b, 0, t)),
            pl.BlockSpec((1, H, tm), lambda b, t: (b, 0, t)),
            pl.BlockSpec((1, H, tm), lambda b, t: (b, 0, t)),
            pl.BlockSpec((1, H, tm, LANES), lambda b, t: (b, 0, t, 0)),
            pl.BlockSpec((1, tm, gla_vw), lambda b, t: (b, t, 0)),
            pl.BlockSpec((1, Hg, dk, dv), lambda b, t: (b, 0, 0, 0)),
        ],
        out_shape=[
            jax.ShapeDtypeStruct((B, fox_w, S), BF16),
            jax.ShapeDtypeStruct((B, fox_w, S), F32),
            jax.ShapeDtypeStruct((B, fox_w, S), F32),
            jax.ShapeDtypeStruct((B, H, S), F32),
            jax.ShapeDtypeStruct((B, H, S), F32),
            jax.ShapeDtypeStruct((B, H, S, LANES), BF16),
            jax.ShapeDtypeStruct((B, S, gla_vw), BF16),
            jax.ShapeDtypeStruct((B, Hg, dk, dv), F32),
        ],
        scratch_shapes=[
            pltpu.VMEM((H, LANES), F32),
            pltpu.VMEM((gla_vw, gla_kw), F32),
            pltpu.VMEM((tm, tm), BF16),
            pltpu.VMEM((tm, tm), BF16),
            pltpu.VMEM((GLA_CHUNK, GLA_CHUNK), BF16),
            pltpu.VMEM((tm, gla_kw), F32), pltpu.VMEM((tm, gla_kw), F32),
            pltpu.VMEM((tm, gla_vw), F32), pltpu.VMEM((tm, gla_vw), F32), pltpu.VMEM((tm, gla_kw), F32),
        ],
        compiler_params=cparams(("arbitrary", "arbitrary")),
        name="proj_gla_prompt",
    )(x_prompt, g1, w_tok, w_dm, fox_b_f.reshape(H, 1), bf_row, w_gate, bgate, gnorm)

    t_blk = ATT_T
    assert t_blk == tm
    nq = S // t_blk
    n_pairs = fox_w // LANES
    c4 = cT_p.reshape(B, n_pairs, 2, S)
    k5 = kaug_p.reshape(B, n_pairs, 2, S, LANES)
    n_vrows = dh + 2 * SUBLANES
    foT_p = pl.pallas_call(
        functools.partial(_attn_prompt_kernel, t_blk=t_blk, dh=dh),
        grid=(B, n_pairs, nq),
        in_specs=[
            pl.BlockSpec((1, LANES, t_blk), lambda b, p, i: (b, p, i)),
            pl.BlockSpec((1, 1, 2, S, LANES), lambda b, p, i: (b, p, 0, 0, 0)),
            pl.BlockSpec((1, LANES, S), lambda b, p, i: (b, p, 0)),
            pl.BlockSpec((1, 1, 2, S), lambda b, p, i: (b, p, 0, 0)),
        ],
        out_specs=pl.BlockSpec((1, LANES, t_blk), lambda b, p, i: (b, p, i)),
        out_shape=jax.ShapeDtypeStruct((B, fox_w, S), BF16),
        scratch_shapes=[
            pltpu.VMEM((2, n_vrows, S), BF16),
            pltpu.VMEM((2, LANES, t_blk), BF16),
            pltpu.VMEM((2, 1, t_blk), F32),
            pltpu.VMEM((2, n_vrows, t_blk), F32),
        ],
        compiler_params=cparams(("arbitrary", "arbitrary", "arbitrary")),
        name="fox_attn_prompt",
    )(qT_p, k5, vT_p, c4)

    xs = x_sample.reshape(Bd, D)
    full = lambda shape: pl.BlockSpec(shape, lambda: (0,) * len(shape))
    s_shapes = [(Bd, fox_w), (Bd, fox_w), (Bd, fox_w), (Bd, H), (Bd, gla_kw), (Bd, gla_kw), (Bd, gla_vw),
                (Bd, gla_vw), (Bd, gla_kw)]
    q_s, k_s, v_s, lf_s, gq_s, gk_s, gv_s, gg_s, la_s = pl.pallas_call(
        functools.partial(_proj_sample_kernel, fox_w=fox_w, fox_dh=dh, gla_kw=gla_kw, gla_vw=gla_vw, n_heads_gla=Hg,
                          n_heads_fox=H),
        in_specs=[full((Bd, D)), full((1, D)), full((n_tok_cols, D)), full((n_dm_rows, D)), full((1, H)),
                  full((LANES, gla_kw)), full((1, gla_kw))],
        out_specs=[full(s) for s in s_shapes],
        out_shape=[jax.ShapeDtypeStruct(s, F32) for s in s_shapes],
        compiler_params=pltpu.CompilerParams(vmem_limit_bytes=VMEM_LIMIT_BYTES),
        name="proj_sample",
    )(xs, g1, w_tok, w_dm, fox_b_f.reshape(1, H), w_gate, bgate)

    kc = jnp.transpose(cache_k[0], (0, 2, 3, 1)).reshape(n_phys, fox_w, page)
    vc = jnp.transpose(cache_v[0], (0, 2, 3, 1)).reshape(n_phys, fox_w, page)
    lfc = jnp.transpose(cache_logf[0], (0, 2, 1))
    col = lambda a: a.reshape(Bd, a.shape[1], 1)
    ftm = FFN_TM
    n_ft = S // ftm
    assert B * n_ft == Bd
    d_ff = wup_bf.shape[1]
    assert (n_pages // DEC_G) % (d_ff // FFN_CHUNK) == 0
    wconst = lambda shape: pl.BlockSpec(shape, lambda b, i, pt: (0, 0), pipeline_mode=pl.Buffered(1))
    seq = lambda b, i, pt: (b * n_ft + i, 0, 0)
    grid_spec = pltpu.PrefetchScalarGridSpec(
        num_scalar_prefetch=1,
        grid=(B, n_ft),
        in_specs=[
            pl.BlockSpec((1, ftm, D), lambda b, i, pt: (b, i, 0)),
            pl.BlockSpec((1, fox_w, ftm), lambda b, i, pt: (b, 0, i)),
            pl.BlockSpec((1, ftm, gla_vw), lambda b, i, pt: (b, i, 0)),
            wconst(wo_bf.shape), wconst(wup_bf.shape), wconst(wdn_bf.shape), wconst(g2.shape), wconst(gf.shape),
            pl.BlockSpec((1, fox_w, 1), seq), pl.BlockSpec((1, fox_w, 1), seq), pl.BlockSpec((1, fox_w, 1), seq),
            pl.BlockSpec((1, H, 1), seq),
            pl.BlockSpec(memory_space=pl.ANY), pl.BlockSpec(memory_space=pl.ANY), pl.BlockSpec(memory_space=pl.ANY),
        ],
        out_specs=[
            pl.BlockSpec((1, ftm, D), lambda b, i, pt: (b, i, 0)),
            pl.BlockSpec((1, fox_w, 1), seq),
        ],
        scratch_shapes=[
            pltpu.VMEM((ftm, d_ff), BF16),
            pltpu.VMEM((DEC_NSLOT, DEC_G, fox_w, page), F32),
            pltpu.VMEM((2, n_pages, H, page), F32),
            pltpu.VMEM((n_pages, H, page), F32),
            pltpu.VMEM((H, n_pages * page), F32),
            pltpu.VMEM((fox_w, page), F32),
            pltpu.VMEM((fox_w, page), F32),
            pltpu.VMEM((page, page), BF16),
            pltpu.SMEM((n_pages // DEC_G,), jnp.int32),
            pltpu.SemaphoreType.DMA((DEC_NSLOT,)),
            pltpu.SemaphoreType.DMA((2,)),
        ],
    )
    y_p, fo_s = pl.pallas_call(
        functools.partial(_ffn_decode_kernel, fox_w=fox_w, d_ff=d_ff, n_pages=n_pages, n_b=Bd, n_heads=H, dh=dh),
        grid_spec=grid_spec,
        out_shape=[jax.ShapeDtypeStruct((B, S, D), F32), jax.ShapeDtypeStruct((Bd, fox_w, 1), F32)],
        compiler_params=cparams(("arbitrary", "arbitrary")),
        name="ffn_prompt_fox_decode",
    )(page_table, x_prompt, foT_p, go_p, wo_bf, wup_bf, wdn_bf, g2, gf,
      col(q_s), col(k_s), col(v_s), col(lf_s), kc, vc, lfc)
    foT_s = fo_s.reshape(Bd, fox_w).T.astype(BF16)

    gb = GLA_SAMPLE_BLOCK
    s_new, go_s = pl.pallas_call(
        functools.partial(_gla_sample_kernel, n_heads=Hg, dk=dk, dv=dv),
        grid=(Bd // gb,),
        in_specs=[
            pl.BlockSpec((gb, Hg, dk, dv), lambda b: (b, 0, 0, 0)),
            pl.BlockSpec((gb, gla_kw, 1), lambda b: (b, 0, 0)),
            pl.BlockSpec((gb, gla_kw, 1), lambda b: (b, 0, 0)),
            pl.BlockSpec((gb, gla_kw, 1), lambda b: (b, 0, 0)),
            pl.BlockSpec((gb, 1, gla_vw), lambda b: (b, 0, 0)),
            pl.BlockSpec((gb, 1, gla_vw), lambda b: (b, 0, 0)),
            pl.BlockSpec((1, dv), lambda b: (0, 0)),
        ],
        out_specs=[
            pl.BlockSpec((gb, Hg, dk, dv), lambda b: (b, 0, 0, 0)),
            pl.BlockSpec((gb, 1, gla_vw), lambda b: (b, 0, 0)),
        ],
        out_shape=[jax.ShapeDtypeStruct((Bd, Hg, dk, dv), F32), jax.ShapeDtypeStruct((Bd, 1, gla_vw), F32)],
        compiler_params=cparams(("arbitrary",)),
        name="gla_sample",
    )(state_gla[0], col(la_s), col(gk_s), col(gq_s), gv_s.reshape(Bd, 1, gla_vw), gg_s.reshape(Bd, 1, gla_vw), gnorm)
    go_s = go_s.reshape(Bd, gla_vw).astype(BF16)

    y_s = _ffn_call(xs[None], foT_s[None], go_s[None], wo_bf, wup_bf, wdn_bf, g2, gf, Bd).reshape(Bd, 1, D)

    new_k_p = jnp.transpose(kT_p.reshape(1, B, H, dh, S), (0, 1, 4, 2, 3))
    new_v_p = jnp.transpose(vT_p.reshape(1, B, H, dh, S), (0, 1, 4, 2, 3))
    new_lf_p = jnp.transpose(lfT_p, (0, 2, 1)).reshape(1, B, S, H)
    return (y_p, y_s, new_k_p, new_v_p, new_lf_p, sfin_p.reshape(1, B, Hg, dk, dv),
            k_s.reshape(1, Bd, 1, H, dh), v_s.reshape(1, Bd, 1, H, dh), lf_s.reshape(1, Bd, 1, H),
            s_new.reshape(1, Bd, Hg, dk, dv))
```

```python
import functools

import jax
import jax.numpy as jnp
from jax import lax
from jax.experimental import pallas as pl
from jax.experimental.pallas import tpu as pltpu

F32 = jnp.float32
BF16 = jnp.bfloat16

LANES = 128
SUBLANES = 8
VMEM_LIMIT_BYTES = 56 * 1024 * 1024

EPS = 1e-6
LOG2E = 1.4426950408889634
N_AUG = 3
FF_LANE0 = 16
GLA_GATE_NORM = 16.0
GLA_CHUNK = 128
GLA_SUB = 32
PROJ_TM = 512
ATT_T = 512
ATT_KS = 256
FFN_TM = 512
FFN_CHUNK = 1024
DEC_G = 8
GLA_SAMPLE_BLOCK = 8
DEC_NSLOT = 4


def _dot(a, b):
    return jnp.dot(a, b, preferred_element_type=F32)


def _dot_nt(a, b):
    return lax.dot_general(a, b, (((1,), (1,)), ((), ())), preferred_element_type=F32)


def _split3(x):
    hi = x.astype(BF16).astype(F32)
    r = x - hi
    mid = r.astype(BF16).astype(F32)
    lo = r - mid
    return hi, mid, lo


def _log_sigmoid(x):
    return jnp.minimum(x, 0.0) - jnp.log1p(jnp.exp(-jnp.abs(x)))


def _silu(x):
    return x / (1.0 + jnp.exp(-x))


def _rms(x, g):
    return x * lax.rsqrt(jnp.mean(x * x, axis=-1, keepdims=True) + EPS) * g


def _proj_prompt_kernel(x_ref, g1_ref, wtok_ref, wdm_ref, bf_ref, bfrow_ref, wgate_ref, bgate_ref, gnorm_ref,
                        qT_out, kT_out, vT_out, lfT_out, cT_out, kaug_out, go_out, sfin_out,
                        carry_s, bdt_s, uincl_s, ltm_s, lincl_s, gq_s, gk_s, gv_s, gg_s, la_s,
                        *, tm, fox_w, fox_dh, gla_kw, gla_vw, n_heads_gla):
    t = pl.program_id(1)
    nt = pl.num_programs(1)
    dk = gla_kw // n_heads_gla
    dv = gla_vw // n_heads_gla

    @pl.when(jnp.logical_and(pl.program_id(0) == 0, t == 0))
    def _():
        r = lax.broadcasted_iota(jnp.int32, (tm, tm), 0)
        c = lax.broadcasted_iota(jnp.int32, (tm, tm), 1)
        uincl_s[...] = jnp.where(r <= c, 1.0, 0.0).astype(BF16)
        ltm_s[...] = jnp.where(c <= r, 1.0, 0.0).astype(BF16)
        r = lax.broadcasted_iota(jnp.int32, (GLA_CHUNK, GLA_CHUNK), 0)
        c = lax.broadcasted_iota(jnp.int32, (GLA_CHUNK, GLA_CHUNK), 1)
        lincl_s[...] = jnp.where(c <= r, 1.0, 0.0).astype(BF16)

    @pl.when(t == 0)
    def _():
        carry_s[...] = jnp.zeros_like(carry_s)
        bdt_s[...] = jnp.zeros_like(bdt_s)

    x = x_ref[0]
    xn = _rms(x, g1_ref[...]).astype(BF16)

    zt = _dot_nt(wdm_ref[...], xn)
    kz = zt[fox_w:2 * fox_w].T

    z = _dot_nt(xn, wtok_ref[...])
    o0 = 0
    gq_s[...] = z[:, o0:o0 + gla_kw] * (dk ** -0.5)
    o0 += gla_kw
    gk_s[...] = z[:, o0:o0 + gla_kw]
    o0 += gla_kw
    gv_s[...] = z[:, o0:o0 + gla_vw]
    o0 += gla_vw
    gg_s[...] = z[:, o0:o0 + gla_vw]
    o0 += gla_vw
    misc = z[:, o0:o0 + LANES]
    pre = _dot(misc.astype(BF16), wgate_ref[...]) + bgate_ref[...]
    la_s[...] = _log_sigmoid(pre) * (1.0 / GLA_GATE_NORM)

    lf_tok = _log_sigmoid(misc + bfrow_ref[...])
    st3 = jnp.concatenate(_split3(lf_tok), axis=1).astype(BF16)
    cc = _dot(ltm_s[...], st3)
    cs_tok = cc[:, 0:LANES] + cc[:, LANES:2 * LANES] + cc[:, 2 * LANES:3 * LANES]
    d_tok = (cs_tok - cs_tok[0:1, :]) * LOG2E
    lane_k = lax.broadcasted_iota(jnp.int32, (tm, LANES), 1)
    for h in range(fox_w // fox_dh):
        own = (h % 2) * fox_dh
        spare = (1 - h % 2) * fox_dh
        parts = _split3(jnp.broadcast_to(d_tok[:, FF_LANE0 + h:FF_LANE0 + h + 1], (tm, LANES)))
        aug = jnp.zeros((tm, LANES), F32)
        for n, part in enumerate(parts):
            aug = jnp.where(lane_k == spare + n, -part, aug)
        own_l = jnp.logical_and(lane_k >= own, lane_k < own + fox_dh)
        kaug_out[0, h] = jnp.where(own_l, kz[:, (h // 2) * LANES:(h // 2 + 1) * LANES], aug).astype(BF16)

    qT_out[0] = (zt[0:fox_w] * (fox_dh ** -0.5 * LOG2E)).astype(BF16)
    kT_out[0] = zt[fox_w:2 * fox_w]
    vT_out[0] = zt[2 * fox_w:3 * fox_w]
    lf = _log_sigmoid(zt[3 * fox_w:3 * fox_w + SUBLANES] + bf_ref[...])
    lfT_out[0] = lf
    hi, mid, lo = _split3(lf)
    stack = jnp.concatenate([hi, mid, lo, jnp.zeros_like(hi)], axis=0).astype(BF16)
    cs = _dot(stack, uincl_s[...])
    cs = cs[0:8] + cs[8:16] + cs[16:24]
    carry = carry_s[...]
    cT_out[0] = cs + carry[:, 0:1]
    tot = _dot(stack, jnp.ones((tm, LANES), BF16))
    carry_s[...] = carry + tot[0:8] + tot[8:16] + tot[16:24]

    nsub = GLA_CHUNK // GLA_SUB
    rowi = lax.broadcasted_iota(jnp.int32, (GLA_CHUNK, gla_kw), 0)
    lanei = lax.broadcasted_iota(jnp.int32, (GLA_CHUNK, gla_kw), 1)
    ar = lax.broadcasted_iota(jnp.int32, (GLA_CHUNK, GLA_CHUNK), 0)
    ac = lax.broadcasted_iota(jnp.int32, (GLA_CHUNK, GLA_CHUNK), 1)
    tri_blk = jnp.logical_and(ar // GLA_SUB == ac // GLA_SUB, ar >= ac)
    br = lax.broadcasted_iota(jnp.int32, (gla_vw, gla_kw), 0)
    bc = lax.broadcasted_iota(jnp.int32, (gla_vw, gla_kw), 1)
    bd_mask = (br // dv) == (bc // dk)

    def chunk_body(ci, _):
        r0 = ci * GLA_CHUNK
        la_c = la_s[pl.ds(r0, GLA_CHUNK), :]
        gq_c = gq_s[pl.ds(r0, GLA_CHUNK), :]
        gk_c = gk_s[pl.ds(r0, GLA_CHUNK), :]
        gv_c = gv_s[pl.ds(r0, GLA_CHUNK), :]
        gg_c = gg_s[pl.ds(r0, GLA_CHUNK), :]
        h3, m3, l3 = _split3(la_c)
        st = jnp.concatenate([h3, m3, l3], axis=1).astype(BF16)
        bb = _dot(lincl_s[...], st)
        b = bb[:, 0:gla_kw] + bb[:, gla_kw:2 * gla_kw] + bb[:, 2 * gla_kw:3 * gla_kw]
        bmid_l, bend_l, b0_l = [], [], []
        for i in range(nsub):
            s0 = i * GLA_SUB
            bmid_l.append(b[s0 + GLA_SUB // 2:s0 + GLA_SUB // 2 + 1])
            bend_l.append(b[s0 + GLA_SUB - 1:s0 + GLA_SUB])
            b0_l.append(jnp.zeros((1, gla_kw), F32) if i == 0 else b[s0 - 1:s0])
        bc_rows = lambda rows: jnp.concatenate(
            [jnp.broadcast_to(r, (GLA_SUB, gla_kw)) for r in rows], axis=0)
        bmid, bend, b0 = bc_rows(bmid_l), bc_rows(bend_l), bc_rows(b0_l)
        qt = (gq_c * jnp.exp(b - bmid)).astype(BF16)
        kt = (gk_c * jnp.exp(bmid - b)).astype(BF16)
        qp = gq_c * jnp.exp(b - b0)
        kd = gk_c * jnp.exp(bend - b)
        gv_bf = gv_c.astype(BF16)
        gvT_bf = gv_c.T.astype(BF16)

        rms = [jnp.logical_and(rowi >= i * GLA_SUB, rowi < (i + 1) * GLA_SUB) for i in range(nsub)]
        uts = [_dot(gvT_bf, jnp.where(rms[i], kd, 0.0).astype(BF16)) for i in range(nsub)]
        states = [bdt_s[...]]
        for i in range(nsub):
            decay = jnp.exp(bend_l[i] - b0_l[i])
            states.append(states[i] * decay + jnp.where(bd_mask, uts[i], 0.0))
        bdt_s[...] = states[nsub]
        o_inter = None
        for i in range(nsub):
            d = _dot_nt(jnp.where(rms[i], qp, 0.0).astype(BF16), states[i].astype(BF16))
            o_inter = d if o_inter is None else o_inter + d

        for h in range(n_heads_gla):
            hm = jnp.logical_and(lanei >= h * dk, lanei < (h + 1) * dk)
            a = _dot_nt(jnp.where(hm, qt, jnp.zeros_like(qt)), kt)
            a = jnp.where(tri_blk, a, 0.0).astype(BF16)
            o_h = _dot(a, gv_bf[:, h * dv:(h + 1) * dv]) + o_inter[:, h * dv:(h + 1) * dv]
            o_n = _rms(o_h, gnorm_ref[...])
            go = o_n * _silu(gg_c[:, h * dv:(h + 1) * dv])
            go_out[0, pl.ds(r0, GLA_CHUNK), h * dv:(h + 1) * dv] = go.astype(BF16)
        return 0

    for ci in range(tm // GLA_CHUNK):
        chunk_body(ci, 0)

    @pl.when(t == nt - 1)
    def _():
        bd = bdt_s[...].T
        for h in range(n_heads_gla):
            sfin_out[0, h] = bd[h * dk:(h + 1) * dk, h * dv:(h + 1) * dv]


def _attn_prompt_kernel(qT_ref, kaug_ref, vT_ref, c_ref, oT_ref, vaug_s, qa_s, m_s, acc_s, *, t_blk, dh):
    i = pl.program_id(2)
    hw2 = 2 * dh

    @pl.when(i == 0)
    def _():
        for h in range(2):
            vaug_s[h, 0:dh, :] = vT_ref[0, h * dh:(h + 1) * dh, :].astype(BF16)
            vaug_s[h, dh:, :] = jnp.ones((vaug_s.shape[1] - dh, vaug_s.shape[2]), BF16)

    qT = qT_ref[0]
    rowq = lax.broadcasted_iota(jnp.int32, (hw2, t_blk), 0)
    keyi = lax.broadcasted_iota(jnp.int32, (ATT_KS, t_blk), 0)
    qryi = lax.broadcasted_iota(jnp.int32, (ATT_KS, t_blk), 1)
    t0 = pl.multiple_of(i * t_blk, t_blk)
    c_q = []
    for h in range(2):
        spare = (1 - h) * dh
        own_r = jnp.logical_and(rowq >= h * dh, rowq < (h + 1) * dh)
        ones_r = jnp.logical_and(rowq >= spare, rowq < spare + N_AUG)
        qa_s[h] = jnp.where(own_r, qT, jnp.where(ones_r, 1.0, 0.0).astype(BF16))
        m_s[h] = jnp.full(m_s.shape[1:], -jnp.inf, F32)
        acc_s[h] = jnp.zeros(acc_s.shape[1:], F32)
        c_q.append(c_ref[0, 0, h:h + 1, pl.ds(t0, LANES)][:, 0:1])

    n_sub = t_blk // ATT_KS

    def steps(tiles):
        work = []
        for j, masked in tiles:
            k0 = pl.multiple_of(j * t_blk, t_blk)
            kks = [pl.multiple_of(k0 + ks * ATT_KS, ATT_KS) for ks in range(n_sub)]
            s_all = [[_dot(kaug_ref[0, 0, h, pl.ds(kks[ks], ATT_KS), :], qa_s[h]) for ks in range(n_sub)]
                     for h in range(2)]
            work.append((k0, kks, s_all, masked))
        m_run = [m_s[h] for h in range(2)]
        acc = [acc_s[h] for h in range(2)]
        for k0, kks, s_all, masked in work:
            offs = [(c_ref[0, 0, h:h + 1, pl.ds(k0, LANES)][:, 0:1] - c_q[h]) * LOG2E for h in range(2)]
            for ks in range(n_sub):
                for h in range(2):
                    s = s_all[h][ks]
                    if masked:
                        s = jnp.where(keyi + ks * ATT_KS <= qryi, s, -jnp.inf)
                    m_new = jnp.maximum(m_run[h], jnp.max(s, axis=0, keepdims=True) - offs[h])
                    p = jnp.exp2(s - (m_new + offs[h]))
                    alpha = jnp.exp2(m_run[h] - m_new)
                    pv = _dot(vaug_s[h, :, pl.ds(kks[ks], ATT_KS)], p.astype(BF16))
                    acc[h] = alpha * acc[h] + pv
                    m_run[h] = m_new
        for h in range(2):
            acc_s[h] = acc[h]
            m_s[h] = m_run[h]

    def pair_body(jj, carry):
        steps([(2 * jj, False), (2 * jj + 1, False)])
        return carry

    lax.fori_loop(0, lax.shift_right_logical(i, 1), pair_body, 0)
    odd = jnp.bitwise_and(i, 1) == 1

    @pl.when(odd)
    def _():
        steps([(i - 1, False), (i, True)])

    @pl.when(jnp.logical_not(odd))
    def _():
        steps([(i, True)])

    outs = [acc_s[h][0:dh, :] / acc_s[h][dh:dh + 1, :] for h in range(2)]
    oT_ref[0] = jnp.concatenate(outs, axis=0).astype(BF16)


def _ffn_kernel(x_ref, foT_ref, go_ref, wo_ref, wup_ref, wdn_ref, g2_ref, gf_ref, y_ref, u_s, *, fox_w, d_ff):
    x = x_ref[0]
    h = x + (_dot(foT_ref[0].T, wo_ref[0:fox_w, :]) + _dot(go_ref[0], wo_ref[fox_w:, :]))
    hn = _rms(h, g2_ref[...]).astype(BF16)
    for c in range(d_ff // FFN_CHUNK):
        u = _dot(hn, wup_ref[:, c * FFN_CHUNK:(c + 1) * FFN_CHUNK])
        u_s[:, c * FFN_CHUNK:(c + 1) * FFN_CHUNK] = jnp.square(jnp.maximum(u, 0.0)).astype(BF16)
    y_ref[0] = _rms(h + _dot(u_s[...], wdn_ref[...]), gf_ref[...])


def _ffn_call(x3, foT, go, wo, wup, wdn, g2, gf, tm):
    nb, n, d = x3.shape
    fox_w = foT.shape[1]
    d_ff = wup.shape[1]
    const = lambda shape: pl.BlockSpec(shape, lambda b, i: (0, 0), pipeline_mode=pl.Buffered(1))
    return pl.pallas_call(
        functools.partial(_ffn_kernel, fox_w=fox_w, d_ff=d_ff),
        grid=(nb, n // tm),
        in_specs=[
            pl.BlockSpec((1, tm, d), lambda b, i: (b, i, 0)),
            pl.BlockSpec((1, fox_w, tm), lambda b, i: (b, 0, i)),
            pl.BlockSpec((1, tm, go.shape[2]), lambda b, i: (b, i, 0)),
            const(wo.shape), const(wup.shape), const(wdn.shape), const(g2.shape), const(gf.shape),
        ],
        out_specs=pl.BlockSpec((1, tm, d), lambda b, i: (b, i, 0)),
        out_shape=jax.ShapeDtypeStruct((nb, n, d), F32),
        scratch_shapes=[pltpu.VMEM((tm, d_ff), BF16)],
        compiler_params=pltpu.CompilerParams(dimension_semantics=("arbitrary", "arbitrary"),
                                             vmem_limit_bytes=VMEM_LIMIT_BYTES),
        name="merge_ffn",
    )(x3, foT, go, wo, wup, wdn, g2, gf)


def _proj_sample_kernel(x_ref, g1_ref, wtok_ref, wdm_ref, bf_ref, wgate_ref, bgate_ref,
                        q_out, k_out, v_out, lf_out, gq_out, gk_out, gv_out, gg_out, la_out,
                        *, fox_w, fox_dh, gla_kw, gla_vw, n_heads_gla, n_heads_fox):
    dk = gla_kw // n_heads_gla
    xn = _rms(x_ref[...], g1_ref[...]).astype(BF16)
    z = _dot_nt(xn, wtok_ref[...])
    o0 = 0
    gq_out[...] = z[:, o0:o0 + gla_kw] * (dk ** -0.5)
    o0 += gla_kw
    gk_out[...] = z[:, o0:o0 + gla_kw]
    o0 += gla_kw
    gv_out[...] = z[:, o0:o0 + gla_vw]
    o0 += gla_vw
    gg_out[...] = z[:, o0:o0 + gla_vw]
    o0 += gla_vw
    glr = z[:, o0:o0 + LANES].astype(BF16)
    la_out[...] = _log_sigmoid(_dot(glr, wgate_ref[...]) + bgate_ref[...]) * (1.0 / GLA_GATE_NORM)
    z2 = _dot_nt(xn, wdm_ref[...])
    q_out[...] = z2[:, 0:fox_w] * (fox_dh ** -0.5)
    k_out[...] = z2[:, fox_w:2 * fox_w]
    v_out[...] = z2[:, 2 * fox_w:3 * fox_w]
    lf_out[...] = _log_sigmoid(z2[:, 3 * fox_w:3 * fox_w + n_heads_fox] + bf_ref[...])


def _ffn_decode_kernel(pt_ref, x_ref, foT_ref, go_ref, wo_ref, wup_ref, wdn_ref, g2_ref, gf_ref,
                       q_ref, knew_ref, vnew_ref, lfnew_ref, kc_hbm, vc_hbm, lfc_hbm,
                       y_ref, o_ref,
                       u_s, ring, lfbuf, rev_s, zbuf, acc_s, qb_s, ustrict_s, live_s, sem_ring, sem_lf,
                       *, fox_w, d_ff, n_pages, n_b, n_heads, dh):
    b = pl.program_id(0) * pl.num_programs(1) + pl.program_id(1)
    page = LANES
    hw = n_heads * dh
    nch = n_pages // DEC_G
    per_b = 2 * nch
    total = n_b * per_b

    def start_chunk(g):
        bg = g // per_b
        c = g - bg * per_b
        slot = lax.rem(g, DEC_NSLOT)

        @pl.when(c < nch)
        def _():
            for j in range(DEC_G):
                p = n_pages - 1 - (c * DEC_G + j)
                pltpu.make_async_copy(kc_hbm.at[pt_ref[bg, p]], ring.at[slot, j],
                                      sem_ring.at[slot]).start(priority=j % 2)

        @pl.when(c >= nch)
        def _():
            for j in range(DEC_G):
                p = n_pages - 1 - ((c - nch) * DEC_G + j)
                pltpu.make_async_copy(vc_hbm.at[pt_ref[bg, p]], ring.at[slot, j],
                                      sem_ring.at[slot]).start(priority=j % 2)

    def wait_chunk(g):
        slot = lax.rem(g, DEC_NSLOT)
        for j in range(DEC_G):
            pltpu.make_async_copy(kc_hbm.at[0], ring.at[slot, j], sem_ring.at[slot]).wait()

    def start_lf(bb):
        sl = lax.rem(bb, 2)

        def body(p, _):
            pltpu.make_async_copy(lfc_hbm.at[pt_ref[bb, p]], lfbuf.at[sl, p], sem_lf.at[sl]).start()
            return 0

        lax.fori_loop(0, n_pages, body, 0)

    def wait_lf(bb):
        sl = lax.rem(bb, 2)

        def body(p, _):
            pltpu.make_async_copy(lfc_hbm.at[0], lfbuf.at[sl, p], sem_lf.at[sl]).wait()
            return 0

        lax.fori_loop(0, n_pages, body, 0)

    g0 = b * per_b

    @pl.when(b == 0)
    def _():
        r = lax.broadcasted_iota(jnp.int32, (page, page), 0)
        c = lax.broadcasted_iota(jnp.int32, (page, page), 1)
        ustrict_s[...] = jnp.where(r > c, 1.0, 0.0).astype(BF16)
        start_lf(b)
        for g in range(DEC_NSLOT - 1):
            start_chunk(g0 + g)

    wait_lf(b)

    @pl.when(b + 1 < n_b)
    def _():
        start_lf(b + 1)

    sl = lax.rem(b, 2)
    lf2d = lfbuf[sl].reshape(n_pages * n_heads, page)
    hi, mid, lo = _split3(lf2d)
    u = ustrict_s[...]
    rev = _dot(hi.astype(BF16), u) + _dot(mid.astype(BF16), u) + _dot(lo.astype(BF16), u)
    rev_s[...] = rev.reshape(n_pages, n_heads, page)

    qb = jnp.broadcast_to(q_ref[0], (hw, page))
    qb_s[...] = qb

    def head_sum(x):
        return jnp.sum(x.reshape(n_heads, dh, page), axis=1)

    def head_bcast(x):
        return jnp.broadcast_to(x[:, None, :], (n_heads, dh, page)).reshape(hw, page)

    x = x_ref[0]
    h = x + (_dot(foT_ref[0].T, wo_ref[0:fox_w, :]) + _dot(go_ref[0], wo_ref[fox_w:, :]))
    hn = _rms(h, g2_ref[...]).astype(BF16)
    n_ffn = d_ff // FFN_CHUNK
    grp = nch // n_ffn

    assert per_b % DEC_NSLOT == 0
    up_rows = hn.shape[0] // 2
    up_n = 2 * d_ff // nch
    carry = jnp.broadcast_to(lfnew_ref[0], (n_heads, page))
    for c in range(nch):
        g = g0 + c

        @pl.when(g + (DEC_NSLOT - 1) < total)
        def _():
            start_chunk(g + (DEC_NSLOT - 1))

        wait_chunk(g)
        slot = c % DEC_NSLOT
        r0 = (c % 2) * up_rows
        col0 = (c // 2) * up_n
        u = _dot(hn[r0:r0 + up_rows], wup_ref[:, col0:col0 + up_n])
        rows = []
        for hd in range(n_heads):
            hs = slice(hd * dh, (hd + 1) * dh)
            qh = qb_s[hs, :]
            ch = carry[hd:hd + 1]
            for j in range(DEC_G):
                p = n_pages - 1 - (c * DEC_G + j)
                s = jnp.sum(ring[slot, j, hs, :] * qh, axis=0, keepdims=True)
                revp = rev_s[p, hd:hd + 1, :]
                zbuf[hd:hd + 1, p * page:(p + 1) * page] = s + revp + ch
                ch = ch + jnp.broadcast_to(revp[:, 0:1] + lfbuf[sl, p, hd:hd + 1, 0:1], (1, page))
            rows.append(ch)
        carry = jnp.concatenate(rows, axis=0)
        u_s[r0:r0 + up_rows, col0:col0 + up_n] = jnp.square(jnp.maximum(u, 0.0)).astype(BF16)

    z_all = zbuf[...]
    z_new = head_sum(jnp.broadcast_to(knew_ref[0], (hw, page)) * qb)
    m = jnp.maximum(jnp.max(z_all, axis=1, keepdims=True), z_new[:, 0:1])
    p_all = jnp.exp(z_all - m)
    zbuf[...] = p_all
    p_new = jnp.exp(z_new - m)
    l = jnp.sum(p_all, axis=1, keepdims=True) + p_new

    acc_s[...] = jnp.zeros_like(acc_s)

    span = DEC_G * page
    for vc in range(nch):
        lo = (n_pages - (vc + 1) * DEC_G) * page
        live_s[vc] = jnp.where(jnp.max(p_all[:, lo:lo + span]) > 0.0, 1, 0).astype(jnp.int32)

    def v_body(c, carry):
        g = g0 + nch + c
        nxt = c + (DEC_NSLOT - 1)
        nxt_live = jnp.logical_or(nxt >= nch, live_s[jnp.minimum(nxt, nch - 1)] != 0)

        @pl.when(jnp.logical_and(g + (DEC_NSLOT - 1) < total, nxt_live))
        def _():
            start_chunk(g + (DEC_NSLOT - 1))

        @pl.when(jnp.logical_or(c < DEC_NSLOT - 1, live_s[c] != 0))
        def _():
            wait_chunk(g)
            slot = lax.rem(g, DEC_NSLOT)
            for hd in range(n_heads):
                hs = slice(hd * dh, (hd + 1) * dh)
                acc = acc_s[hs, :]
                for j in range(DEC_G):
                    p = n_pages - 1 - (c * DEC_G + j)
                    pp = zbuf[hd:hd + 1, pl.ds(pl.multiple_of(p * page, page), page)]
                    acc = acc + ring[slot, j, hs, :] * jnp.broadcast_to(pp, (dh, page))
                acc_s[hs, :] = acc

        return carry

    down = None
    for c in range(n_ffn):
        lax.fori_loop(c * grp, (c + 1) * grp, v_body, 0)
        d = _dot(u_s[:, c * FFN_CHUNK:(c + 1) * FFN_CHUNK], wdn_ref[c * FFN_CHUNK:(c + 1) * FFN_CHUNK, :])
        down = d if down is None else down + d
    y_ref[0] = _rms(h + down, gf_ref[...])

    num = jnp.sum(acc_s[...], axis=1, keepdims=True) + head_bcast(p_new) * jnp.broadcast_to(vnew_ref[0], (hw, page))
    o_ref[0] = (num / head_bcast(l))[:, 0:1]


def _gla_sample_kernel(s_ref, la_ref, k_ref, q_ref, v_ref, gg_ref, gnorm_ref, s_out, go_out, *, n_heads, dk, dv):
    for i in range(s_ref.shape[0]):
        for h in range(n_heads):
            la = la_ref[i, h * dk:(h + 1) * dk, :]
            kk = k_ref[i, h * dk:(h + 1) * dk, :]
            qq = q_ref[i, h * dk:(h + 1) * dk, :]
            vv = v_ref[i, :, h * dv:(h + 1) * dv]
            s_new = s_ref[i, h] * jnp.exp(la) + kk * vv
            s_out[i, h] = s_new
            o = jnp.sum(qq * s_new, axis=0, keepdims=True)
            o_n = _rms(o, gnorm_ref[...])
            go_out[i, :, h * dv:(h + 1) * dv] = o_n * _silu(gg_ref[i, :, h * dv:(h + 1) * dv])


def kernel(x_prompt, x_sample, cache_k, cache_v, cache_logf, state_gla, page_table, norm1_g, w_in, fox_b_f,
           gla_w_gate_up, gla_b_gate, gla_norm_g, w_o, norm2_g, w_up, w_down, final_g):
    B, S, D = x_prompt.shape
    Bd = x_sample.shape[0]
    depth, n_phys, page, H, dh = cache_k.shape
    _, _, Hg, dk, dv = state_gla.shape
    assert depth == 1 and x_sample.shape[1] == 1 and page == LANES
    fox_w = H * dh
    gla_kw = Hg * dk
    gla_vw = Hg * dv
    rank = gla_w_gate_up.shape[1]
    n_pages = page_table.shape[1]

    wt = jnp.transpose(w_in[0])
    o_fq = 0
    o_ff = 3 * fox_w
    o_gq = o_ff + H
    o_gk = o_gq + gla_kw
    o_gv = o_gk + gla_kw
    o_glr = o_gv + gla_vw
    o_gg = o_glr + rank
    misc_pad = LANES - FF_LANE0 - H
    w_tok = jnp.concatenate([
        wt[o_gq:o_gq + gla_kw], wt[o_gk:o_gk + gla_kw], wt[o_gv:o_gv + gla_vw],
        wt[o_gg:o_gg + gla_vw], wt[o_glr:o_glr + rank], jnp.zeros((FF_LANE0 - rank, D), F32),
        wt[o_ff:o_ff + H], jnp.zeros((misc_pad, D), F32)], axis=0).astype(BF16)
    w_dm = jnp.concatenate([
        wt[o_fq:o_fq + 3 * fox_w], wt[o_ff:o_ff + H], jnp.zeros((2 * SUBLANES - H, D), F32)], axis=0).astype(BF16)
    bf_row = jnp.concatenate([jnp.zeros((1, FF_LANE0), F32), fox_b_f.reshape(1, H), jnp.zeros((1, misc_pad), F32)],
                             axis=1)
    w_gate = jnp.concatenate([gla_w_gate_up[0], jnp.zeros((LANES - rank, gla_kw), F32)], axis=0).astype(BF16)
    wo_bf = w_o[0].astype(BF16)
    wup_bf = w_up[0].astype(BF16)
    wdn_bf = w_down[0].astype(BF16)
    g1 = norm1_g.reshape(1, D)
    g2 = norm2_g.reshape(1, D)
    gf = final_g.reshape(1, D)
    bgate = gla_b_gate.reshape(1, gla_kw)
    gnorm = gla_norm_g.reshape(1, dv)

    cparams = lambda sem: pltpu.CompilerParams(dimension_semantics=sem, vmem_limit_bytes=VMEM_LIMIT_BYTES)
    single = lambda shape: pl.BlockSpec(shape, lambda *_: (0,) * len(shape), pipeline_mode=pl.Buffered(1))

    tm = PROJ_TM
    nt = S // tm
    n_tok_cols = w_tok.shape[0]
    n_dm_rows = w_dm.shape[0]
    qT_p, kT_p, vT_p, lfT_p, cT_p, kaug_p, go_p, sfin_p = pl.pallas_call(
        functools.partial(_proj_prompt_kernel, tm=tm, fox_w=fox_w, fox_dh=dh, gla_kw=gla_kw, gla_vw=gla_vw,
                          n_heads_gla=Hg),
        grid=(B, nt),
        in_specs=[
            pl.BlockSpec((1, tm, D), lambda b, t: (b, t, 0)),
            single((1, D)), single((n_tok_cols, D)), single((n_dm_rows, D)), single((H, 1)), single((1, LANES)),
            single((LANES, gla_kw)), single((1, gla_kw)), single((1, dv)),
        ],
        out_specs=[
            pl.BlockSpec((1, fox_w, tm), lambda b, t: (b, 0, t)),
            pl.BlockSpec((1, fox_w, tm), lambda b, t: (b, 0, t)),
            pl.BlockSpec((1, fox_w, tm), lambda b, t: (b, 0, t)),
            pl.BlockSpec((1, H, tm), lambda b, t: (b, 0, t)),
            pl.BlockSpec((1, H, tm), lambda b, t: (b, 0, t)),
            pl.BlockSpec((1, H, tm, LANES), lambda b, t: (b, 0, t, 0)),
            pl.BlockSpec((1, tm, gla_vw), lambda b, t: (b, t, 0)),
            pl.BlockSpec((1, Hg, dk, dv), lambda b, t: (b, 0, 0, 0)),
        ],
        out_shape=[
            jax.ShapeDtypeStruct((B, fox_w, S), BF16),
            jax.ShapeDtypeStruct((B, fox_w, S), F32),
            jax.ShapeDtypeStruct((B, fox_w, S), F32),
            jax.ShapeDtypeStruct((B, H, S), F32),
            jax.ShapeDtypeStruct((B, H, S), F32),
            jax.ShapeDtypeStruct((B, H, S, LANES), BF16),
            jax.ShapeDtypeStruct((B, S, gla_vw), BF16),
            jax.ShapeDtypeStruct((B, Hg, dk, dv), F32),
        ],
        scratch_shapes=[
            pltpu.VMEM((H, LANES), F32),
            pltpu.VMEM((gla_vw, gla_kw), F32),
            pltpu.VMEM((tm, tm), BF16),
            pltpu.VMEM((tm, tm), BF16),
            pltpu.VMEM((GLA_CHUNK, GLA_CHUNK), BF16),
            pltpu.VMEM((tm, gla_kw), F32), pltpu.VMEM((tm, gla_kw), F32),
            pltpu.VMEM((tm, gla_vw), F32), pltpu.VMEM((tm, gla_vw), F32), pltpu.VMEM((tm, gla_kw), F32),
        ],
        compiler_params=cparams(("arbitrary", "arbitrary")),
        name="proj_gla_prompt",
    )(x_prompt, g1, w_tok, w_dm, fox_b_f.reshape(H, 1), bf_row, w_gate, bgate, gnorm)

    t_blk = ATT_T
    assert t_blk == tm
    nq = S // t_blk
    n_pairs = fox_w // LANES
    c4 = cT_p.reshape(B, n_pairs, 2, S)
    k5 = kaug_p.reshape(B, n_pairs, 2, S, LANES)
    n_vrows = dh + 2 * SUBLANES
    foT_p = pl.pallas_call(
        functools.partial(_attn_prompt_kernel, t_blk=t_blk, dh=dh),
        grid=(B, n_pairs, nq),
        in_specs=[
            pl.BlockSpec((1, LANES, t_blk), lambda b, p, i: (b, p, i)),
            pl.BlockSpec((1, 1, 2, S, LANES), lambda b, p, i: (b, p, 0, 0, 0)),
            pl.BlockSpec((1, LANES, S), lambda b, p, i: (b, p, 0)),
            pl.BlockSpec((1, 1, 2, S), lambda b, p, i: (b, p, 0, 0)),
        ],
        out_specs=pl.BlockSpec((1, LANES, t_blk), lambda b, p, i: (b, p, i)),
        out_shape=jax.ShapeDtypeStruct((B, fox_w, S), BF16),
        scratch_shapes=[
            pltpu.VMEM((2, n_vrows, S), BF16),
            pltpu.VMEM((2, LANES, t_blk), BF16),
            pltpu.VMEM((2, 1, t_blk), F32),
            pltpu.VMEM((2, n_vrows, t_blk), F32),
        ],
        compiler_params=cparams(("arbitrary", "arbitrary", "arbitrary")),
        name="fox_attn_prompt",
    )(qT_p, k5, vT_p, c4)

    xs = x_sample.reshape(Bd, D)
    full = lambda shape: pl.BlockSpec(shape, lambda: (0,) * len(shape))
    s_shapes = [(Bd, fox_w), (Bd, fox_w), (Bd, fox_w), (Bd, H), (Bd, gla_kw), (Bd, gla_kw), (Bd, gla_vw),
                (Bd, gla_vw), (Bd, gla_kw)]
    q_s, k_s, v_s, lf_s, gq_s, gk_s, gv_s, gg_s, la_s = pl.pallas_call(
        functools.partial(_proj_sample_kernel, fox_w=fox_w, fox_dh=dh, gla_kw=gla_kw, gla_vw=gla_vw, n_heads_gla=Hg,
                          n_heads_fox=H),
        in_specs=[full((Bd, D)), full((1, D)), full((n_tok_cols, D)), full((n_dm_rows, D)), full((1, H)),
                  full((LANES, gla_kw)), full((1, gla_kw))],
        out_specs=[full(s) for s in s_shapes],
        out_shape=[jax.ShapeDtypeStruct(s, F32) for s in s_shapes],
        compiler_params=pltpu.CompilerParams(vmem_limit_bytes=VMEM_LIMIT_BYTES),
        name="proj_sample",
    )(xs, g1, w_tok, w_dm, fox_b_f.reshape(1, H), w_gate, bgate)

    kc = jnp.transpose(cache_k[0], (0, 2, 3, 1)).reshape(n_phys, fox_w, page)
    vc = jnp.transpose(cache_v[0], (0, 2, 3, 1)).reshape(n_phys, fox_w, page)
    lfc = jnp.transpose(cache_logf[0], (0, 2, 1))
    col = lambda a: a.reshape(Bd, a.shape[1], 1)
    ftm = FFN_TM
    n_ft = S // ftm
    assert B * n_ft == Bd
    d_ff = wup_bf.shape[1]
    assert (n_pages // DEC_G) % (d_ff // FFN_CHUNK) == 0
    wconst = lambda shape: pl.BlockSpec(shape, lambda b, i, pt: (0, 0), pipeline_mode=pl.Buffered(1))
    seq = lambda b, i, pt: (b * n_ft + i, 0, 0)
    grid_spec = pltpu.PrefetchScalarGridSpec(
        num_scalar_prefetch=1,
        grid=(B, n_ft),
        in_specs=[
            pl.BlockSpec((1, ftm, D), lambda b, i, pt: (b, i, 0)),
            pl.BlockSpec((1, fox_w, ftm), lambda b, i, pt: (b, 0, i)),
            pl.BlockSpec((1, ftm, gla_vw), lambda b, i, pt: (b, i, 0)),
            wconst(wo_bf.shape), wconst(wup_bf.shape), wconst(wdn_bf.shape), wconst(g2.shape), wconst(gf.shape),
            pl.BlockSpec((1, fox_w, 1), seq), pl.BlockSpec((1, fox_w, 1), seq), pl.BlockSpec((1, fox_w, 1), seq),
            pl.BlockSpec((1, H, 1), seq),
            pl.BlockSpec(memory_space=pl.ANY), pl.BlockSpec(memory_space=pl.ANY), pl.BlockSpec(memory_space=pl.ANY),
        ],
        out_specs=[
            pl.BlockSpec((1, ftm, D), lambda b, i, pt: (b, i, 0)),
            pl.BlockSpec((1, fox_w, 1), seq),
        ],
        scratch_shapes=[
            pltpu.VMEM((ftm, d_ff), BF16),
            pltpu.VMEM((DEC_NSLOT, DEC_G, fox_w, page), F32),
            pltpu.VMEM((2, n_pages, H, page), F32),
            pltpu.VMEM((n_pages, H, page), F32),
            pltpu.VMEM((H, n_pages * page), F32),
            pltpu.VMEM((fox_w, page), F32),
            pltpu.VMEM((fox_w, page), F32),
            pltpu.VMEM((page, page), BF16),
            pltpu.SMEM((n_pages // DEC_G,), jnp.int32),
            pltpu.SemaphoreType.DMA((DEC_NSLOT,)),
            pltpu.SemaphoreType.DMA((2,)),
        ],
    )
    y_p, fo_s = pl.pallas_call(
        functools.partial(_ffn_decode_kernel, fox_w=fox_w, d_ff=d_ff, n_pages=n_pages, n_b=Bd, n_heads=H, dh=dh),
        grid_spec=grid_spec,
        out_shape=[jax.ShapeDtypeStruct((B, S, D), F32), jax.ShapeDtypeStruct((Bd, fox_w, 1), F32)],
        compiler_params=cparams(("arbitrary", "arbitrary")),
        name="ffn_prompt_fox_decode",
    )(page_table, x_prompt, foT_p, go_p, wo_bf, wup_bf, wdn_bf, g2, gf,
      col(q_s), col(k_s), col(v_s), col(lf_s), kc, vc, lfc)
    foT_s = fo_s.reshape(Bd, fox_w).T.astype(BF16)

    gb = GLA_SAMPLE_BLOCK
    s_new, go_s = pl.pallas_call(
        functools.partial(_gla_sample_kernel, n_heads=Hg, dk=dk, dv=dv),
        grid=(Bd // gb,),
        in_specs=[
            pl.BlockSpec((gb, Hg, dk, dv), lambda b: (b, 0, 0, 0)),
            pl.BlockSpec((gb, gla_kw, 1), lambda b: (b, 0, 0)),
            pl.BlockSpec((gb, gla_kw, 1), lambda b: (b, 0, 0)),
            pl.BlockSpec((gb, gla_kw, 1), lambda b: (b, 0, 0)),
            pl.BlockSpec((gb, 1, gla_vw), lambda b: (b, 0, 0)),
            pl.BlockSpec((gb, 1, gla_vw), lambda b: (b, 0, 0)),
            pl.BlockSpec((1, dv), lambda b: (0, 0)),
        ],
        out_specs=[
            pl.BlockSpec((gb, Hg, dk, dv), lambda b: (b, 0, 0, 0)),
            pl.BlockSpec((gb, 1, gla_vw), lambda b: (b, 0, 0)),
        ],
        out_shape=[jax.ShapeDtypeStruct((Bd, Hg, dk, dv), F32), jax.ShapeDtypeStruct((Bd, 1, gla_vw), F32)],
        compiler_params=cparams(("arbitrary",)),
        name="gla_sample",
    )(state_gla[0], col(la_s), col(gk_s), col(gq_s), gv_s.reshape(Bd, 1, gla_vw), gg_s.reshape(Bd, 1, gla_vw), gnorm)
    go_s = go_s.reshape(Bd, gla_vw).astype(BF16)

    y_s = _ffn_call(xs[None], foT_s[None], go_s[None], wo_bf, wup_bf, wdn_bf, g2, gf, Bd).reshape(Bd, 1, D)

    new_k_p = jnp.transpose(kT_p.reshape(1, B, H, dh, S), (0, 1, 4, 2, 3))
    new_v_p = jnp.transpose(vT_p.reshape(1, B, H, dh, S), (0, 1, 4, 2, 3))
    new_lf_p = jnp.transpose(lfT_p, (0, 2, 1)).reshape(1, B, S, H)
    return (y_p, y_s, new_k_p, new_v_p, new_lf_p, sfin_p.reshape(1, B, Hg, dk, dv),
            k_s.reshape(1, Bd, 1, H, dh), v_s.reshape(1, Bd, 1, H, dh), lf_s.reshape(1, Bd, 1, H),
            s_new.reshape(1, Bd, Hg, dk, dv))
```
